```python
import math
import jax, jax.numpy as jnp
from jax import lax
import numpy as np

D_MODEL = 2048
BATCH = 2
SEQ = 4096
DEPTH = 1

HEAD_DIM = 128
D_MIX = D_MODEL
RET_WIDTH = D_MIX // 2
FOX_WIDTH = D_MIX - RET_WIDTH
RET_HEADS = RET_WIDTH // HEAD_DIM
FOX_HEADS = FOX_WIDTH // HEAD_DIM
RET_CHUNK = 128
FOX_BLOCK = 128
ROPE_BASE = 10000.0
N_GROUPS = 4
EXPERTS_PER_GROUP = 8
N_EXPERTS = N_GROUPS * EXPERTS_PER_GROUP
TOP_K_FINE = 2
D_EXPERT = D_MODEL // 4
N_MOD = 6
EPS = 1e-6
IN_COLS = 4 * RET_WIDTH + 3 * FOX_WIDTH + FOX_HEADS

kernel_name = "hybrid_retention_fox_hmoe_layer"


def rms_norm(x, w):
    xf = x.astype(jnp.float32)
    y = xf * lax.rsqrt(jnp.mean(xf * xf, axis=-1, keepdims=True) + EPS)
    return y.astype(x.dtype) * w


def head_rms(x):
    xf = x.astype(jnp.float32)
    return (xf * lax.rsqrt(jnp.mean(xf * xf, axis=-1, keepdims=True) + EPS)).astype(x.dtype)


def head_layer_norm(x):
    xf = x.astype(jnp.float32)
    mu = jnp.mean(xf, axis=-1, keepdims=True)
    var = jnp.mean(jnp.square(xf - mu), axis=-1, keepdims=True)
    return ((xf - mu) * lax.rsqrt(var + EPS)).astype(x.dtype)


def split_heads(t, n_heads):
    b, s, _ = t.shape
    return t.reshape(b, s, n_heads, HEAD_DIM).transpose(0, 2, 1, 3)


def merge_heads(t):
    b, h, s, d = t.shape
    return t.transpose(0, 2, 1, 3).reshape(b, s, h * d)


def rope(x, cos, sin):
    x1, x2 = jnp.split(x, 2, axis=-1)
    return jnp.concatenate([x1 * cos - x2 * sin, x1 * sin + x2 * cos], axis=-1)


def retention_chunkwise(q, k, v):
    b, h, s, dh = q.shape
    c = RET_CHUNK
    n = s // c
    log_gamma = jnp.log1p(-jnp.exp2(-5.0 - jnp.arange(h, dtype=jnp.float32)))
    k = k * (dh ** -0.5)
    qc = q.reshape(b, h, n, c, dh)
    kc = k.reshape(b, h, n, c, dh)
    vc = v.reshape(b, h, n, c, dh)
    pos = jnp.arange(c, dtype=jnp.float32)
    diff = pos[:, None] - pos[None, :]
    decay = jnp.where(diff >= 0,
                      jnp.exp(log_gamma[:, None, None] * jnp.maximum(diff, 0.0)[None]),
                      0.0)
    zeta = jnp.exp(log_gamma[:, None] * (c - 1 - pos)[None])
    xi = jnp.exp(log_gamma[:, None] * (pos + 1)[None])
    chunk_decay = jnp.exp(log_gamma * c)
    scores = jnp.einsum('bhnid,bhnjd->bhnij', qc, kc) * decay[None, :, None]
    intra = jnp.einsum('bhnij,bhnjd->bhnid', scores, vc)
    kv = jnp.einsum('bhncd,bhnce->nbhde', kc * zeta[None, :, None, :, None], vc)

    def step(state, kv_n):
        return state * chunk_decay[None, :, None, None] + kv_n, state

    _, prev_states = lax.scan(step, jnp.zeros((b, h, dh, dh), kv.dtype), kv)
    cross = jnp.einsum('bhncd,nbhde->bhnce', qc, prev_states) * xi[None, :, None, :, None]
    return (intra + cross).reshape(b, h, s, dh)


def forgetting_attention(q, k, v, log_f):
    b, h, s, dh = q.shape
    nb = s // FOX_BLOCK
    cum = jnp.cumsum(log_f, axis=-1)
    qb = q.reshape(b, h, nb, FOX_BLOCK, dh).transpose(2, 0, 1, 3, 4)
    cb = cum.reshape(b, h, nb, FOX_BLOCK).transpose(2, 0, 1, 3)
    kpos = jnp.arange(s)
    scale = dh ** -0.5

    def block(args):
        i, q_i, c_i = args
        qpos = i * FOX_BLOCK + jnp.arange(FOX_BLOCK)
        logits = jnp.einsum('bhqd,bhkd->bhqk', q_i, k).astype(jnp.float32) * scale
        logits = logits + c_i[..., :, None] - cum[..., None, :]
        logits = jnp.where(kpos[None, :] <= qpos[:, None], logits, -jnp.inf)
        p = jax.nn.softmax(logits, axis=-1)
        return jnp.einsum('bhqk,bhkd->bhqd', p.astype(v.dtype), v)

    out = lax.map(block, (jnp.arange(nb), qb, cb))
    return out.transpose(1, 2, 0, 3, 4).reshape(b, h, s, dh)


def hybrid_mixer(h, w_in, fox_f_bias, ret_gn_w, fox_qn_w, fox_kn_w, fox_on_w, w_out, cos, sin):
    proj = h @ w_in
    r = RET_WIDTH
    f = FOX_WIDTH
    cuts = [r, 2 * r, 3 * r, 4 * r, 4 * r + f, 4 * r + 2 * f, 4 * r + 3 * f]
    rq, rk, rv, rg, fq, fk, fv, fgl = jnp.split(proj, cuts, axis=-1)
    rq = rope(split_heads(rq, RET_HEADS), cos, sin)
    rk = rope(split_heads(rk, RET_HEADS), cos, sin)
    rv = split_heads(rv, RET_HEADS)
    ret = merge_heads(head_layer_norm(retention_chunkwise(rq, rk, rv))) * ret_gn_w
    ret = jax.nn.silu(rg) * ret
    fq = head_rms(split_heads(fq, FOX_HEADS)) * fox_qn_w
    fk = head_rms(split_heads(fk, FOX_HEADS)) * fox_kn_w
    fv = split_heads(fv, FOX_HEADS)
    log_f = jax.nn.log_sigmoid(fgl.astype(jnp.float32) + fox_f_bias).transpose(0, 2, 1)
    fox = merge_heads(head_rms(forgetting_attention(fq, fk, fv, log_f))) * fox_on_w
    return jnp.concatenate([ret, fox], axis=-1) @ w_out


def hierarchical_moe(h, router_coarse_w, router_coarse_b, router_fine_w, router_fine_b, w1, w3, w2):
    b, s, d = h.shape
    t = h.reshape(b * s, d)
    coarse = (t @ router_coarse_w).astype(jnp.float32) + router_coarse_b
    g_prob, g_idx = lax.top_k(jax.nn.softmax(coarse, axis=-1), 1)
    fine = ((t @ router_fine_w).astype(jnp.float32) + router_fine_b).reshape(-1, N_GROUPS, EXPERTS_PER_GROUP)
    fine_sel = jnp.take_along_axis(fine, g_idx[:, :, None], axis=1)[:, 0]
    f_val, f_idx = lax.top_k(fine_sel, TOP_K_FINE)
    weights = g_prob * jax.nn.softmax(f_val, axis=-1)
    expert_id = g_idx * EXPERTS_PER_GROUP + f_idx
    combine = jnp.sum(jax.nn.one_hot(expert_id, N_EXPERTS, dtype=jnp.float32) * weights[..., None], axis=1)
    combine = combine.astype(t.dtype)

    def expert_step(acc, xs):
        w1_e, w3_e, w2_e, c_e = xs
        y = (jax.nn.silu(t @ w1_e) * (t @ w3_e)) @ w2_e
        return acc + c_e[:, None] * y, None

    acc, _ = lax.scan(expert_step, jnp.zeros_like(t), (w1, w3, w2, combine.T))
    return acc.reshape(b, s, d)


def setup_inputs(seed: int = 0) -> dict:
    key = jax.random.key(seed)
    ks = jax.random.split(key, 24)
    f32 = jnp.float32
    dm = D_MODEL
    nrm = lambda k, shape, sc: jax.random.normal(k, shape, f32) * sc
    gain = lambda k, shape: 1.0 + 0.02 * jax.random.normal(k, shape, f32)
    w_in_main = nrm(ks[2], (DEPTH, dm, IN_COLS - FOX_HEADS), dm ** -0.5)
    w_in_forget = nrm(ks[3], (DEPTH, dm, FOX_HEADS), 0.1 * dm ** -0.5)
    return {
        "x": jax.random.normal(ks[0], (BATCH, SEQ, dm), f32),
        "c": jax.random.normal(ks[1], (BATCH, dm), f32),
        "ada_w": nrm(ks[4], (DEPTH, dm, N_MOD * dm), 0.5 * dm ** -0.5),
        "ada_b": nrm(ks[5], (DEPTH, N_MOD * dm), 0.01),
        "norm_pre_mix": gain(ks[6], (DEPTH, dm)),
        "norm_post_mix": gain(ks[7], (DEPTH, dm)),
        "w_in": jnp.concatenate([w_in_main, w_in_forget], axis=-1),
        "fox_f_bias": 2.0 + 4.0 * jax.random.uniform(ks[8], (DEPTH, FOX_HEADS), f32),
        "ret_gn_w": gain(ks[9], (DEPTH, RET_WIDTH)),
        "fox_qn_w": gain(ks[10], (DEPTH, HEAD_DIM)),
        "fox_kn_w": gain(ks[11], (DEPTH, HEAD_DIM)),
        "fox_on_w": gain(ks[12], (DEPTH, FOX_WIDTH)),
        "w_out": nrm(ks[13], (DEPTH, D_MIX, dm), D_MIX ** -0.5),
        "norm_pre_ffn": gain(ks[14], (DEPTH, dm)),
        "norm_post_ffn": gain(ks[15], (DEPTH, dm)),
        "router_coarse_w": nrm(ks[16], (DEPTH, dm, N_GROUPS), dm ** -0.5),
        "router_coarse_b": nrm(ks[17], (DEPTH, N_GROUPS), 0.01),
        "router_fine_w": nrm(ks[18], (DEPTH, dm, N_EXPERTS), dm ** -0.5),
        "router_fine_b": nrm(ks[19], (DEPTH, N_EXPERTS), 0.01),
        "w1": nrm(ks[20], (DEPTH, N_EXPERTS, dm, D_EXPERT), dm ** -0.5),
        "w3": nrm(ks[21], (DEPTH, N_EXPERTS, dm, D_EXPERT), dm ** -0.5),
        "w2": nrm(ks[22], (DEPTH, N_EXPERTS, D_EXPERT, dm), D_EXPERT ** -0.5),
    }


def reference(x, c, ada_w, ada_b, norm_pre_mix, norm_post_mix, w_in, fox_f_bias, ret_gn_w,
              fox_qn_w, fox_kn_w, fox_on_w, w_out, norm_pre_ffn, norm_post_ffn,
              router_coarse_w, router_coarse_b, router_fine_w, router_fine_b, w1, w3, w2):
    s = x.shape[1]
    inv_freq = ROPE_BASE ** (-jnp.arange(0, HEAD_DIM, 2, dtype=jnp.float32) / HEAD_DIM)
    angle = jnp.arange(s, dtype=jnp.float32)[:, None] * inv_freq[None, :]
    cos = jnp.cos(angle).astype(x.dtype)
    sin = jnp.sin(angle).astype(x.dtype)
    c_act = jax.nn.silu(c)
    for l in range(DEPTH):
        mod = c_act @ ada_w[l] + ada_b[l]
        sh1, sc1, g1, sh2, sc2, g2 = [m[:, None, :] for m in jnp.split(mod, N_MOD, axis=-1)]
        h = rms_norm(x, norm_pre_mix[l]) * (1.0 + sc1) + sh1
        y = hybrid_mixer(h, w_in[l], fox_f_bias[l], ret_gn_w[l], fox_qn_w[l], fox_kn_w[l],
                         fox_on_w[l], w_out[l], cos, sin)
        x = x + g1 * rms_norm(y, norm_post_mix[l])
        h = rms_norm(x, norm_pre_ffn[l]) * (1.0 + sc2) + sh2
        y = hierarchical_moe(h, router_coarse_w[l], router_coarse_b[l], router_fine_w[l],
                             router_fine_b[l], w1[l], w3[l], w2[l])
        x = x + g2 * rms_norm(y, norm_post_ffn[l])
    return x
```

```python
import functools
import math

import jax
import jax.numpy as jnp
from jax import lax
from jax.experimental import pallas as pl
from jax.experimental.pallas import tpu as pltpu

F32 = jnp.float32
BF16 = jnp.bfloat16
I32 = jnp.int32
U32 = jnp.uint32

D_MODEL = 2048
HEAD_DIM = 128
RET_WIDTH = D_MODEL // 2
FOX_WIDTH = D_MODEL - RET_WIDTH
RET_HEADS = RET_WIDTH // HEAD_DIM
FOX_HEADS = FOX_WIDTH // HEAD_DIM
ROPE_BASE = 10000.0
N_GROUPS = 4
EXPERTS_PER_GROUP = 8
N_EXPERTS = N_GROUPS * EXPERTS_PER_GROUP
D_EXPERT = D_MODEL // 4
N_MOD = 6
EPS = 1e-6
MAIN_COLS = 4 * RET_WIDTH + 3 * FOX_WIDTH

LANES = 128
HALF = D_MODEL // 2
NEG = -1e30

VMEM_LIMIT = 56 * 1024 * 1024

MOD_TN = 1024
IN_TM, IN_TN = 1024, 512
RET_CHUNK = 256
FOX_T = 512
OUT_TM = 256
EXP_TM = 256
CMB_TM = 256
DMA_LAG = 64


def _cparams(sem):
    return pltpu.CompilerParams(dimension_semantics=sem, vmem_limit_bytes=VMEM_LIMIT)


def _nt_dot(a, b):
    return lax.dot_general(a, b, (((1,), (1,)), ((), ())), preferred_element_type=F32)


def _tn_dot(a, b):
    return lax.dot_general(a, b, (((0,), (0,)), ((), ())), preferred_element_type=F32)


def _sigmoid(x):
    return 1.0 / (1.0 + jnp.exp(-x))


def _mod_kernel(cb_ref, w_ref, b_ref, o_ref):
    nb = cb_ref.shape[0]
    for b in range(nb):
        cv = cb_ref[b]
        ca = cv * _sigmoid(cv)
        for j in range(MOD_TN // LANES):
            sl = slice(j * LANES, (j + 1) * LANES)
            col = jnp.sum(w_ref[:, sl] * ca, axis=0, keepdims=True)
            o_ref[b:b + 1, sl] = col + b_ref[:, sl]


def _modulation(c, ada_w, ada_b):
    nb, d = c.shape
    n = ada_w.shape[1]
    cb = jnp.broadcast_to(c[:, :, None], (nb, d, LANES))
    return pl.pallas_call(
        _mod_kernel,
        grid=(n // MOD_TN,),
        in_specs=[pl.BlockSpec((nb, d, LANES), lambda j: (0, 0, 0)),
                  pl.BlockSpec((d, MOD_TN), lambda j: (0, j)),
                  pl.BlockSpec((1, MOD_TN), lambda j: (0, j))],
        out_specs=pl.BlockSpec((nb, MOD_TN), lambda j: (0, j)),
        out_shape=jax.ShapeDtypeStruct((nb, n), F32),
        compiler_params=_cparams(("arbitrary",)),
        name="adaln_mod",
    )(cb, ada_w, ada_b.reshape(1, n))


def _rms_rows(x):
    return x * lax.rsqrt(jnp.mean(x * x, axis=-1, keepdims=True) + EPS)


def _inproj_kernel(x_ref, mod_ref, nw_ref, w_ref, wf_ref, fb_ref, cos_ref, sin_ref, qn_ref, kn_ref,
                   o_ref, lf_ref, h_scr):
    j = pl.program_id(1)
    tiles_per_sec = RET_WIDTH // IN_TN
    heads_per_tile = IN_TN // HEAD_DIM

    @pl.when(j == 0)
    def _():
        shift = mod_ref[0, 0:1, :]
        scale = 1.0 + mod_ref[0, 1:2, :]
        nw = nw_ref[...]
        rows = 128
        for r in range(IN_TM // rows):
            rs = slice(r * rows, (r + 1) * rows)
            h = _rms_rows(x_ref[rs, :]) * nw * scale + shift
            h_scr[rs, :] = h.astype(BF16)
        z = _nt_dot(wf_ref[...], h_scr[...]) + fb_ref[...]
        lf_ref[0] = jnp.minimum(z, 0.0) - jnp.log1p(jnp.exp(-jnp.abs(z)))

    acc = jnp.dot(h_scr[...], w_ref[...].astype(BF16), preferred_element_type=F32)
    sec = j // tiles_per_sec

    def heads():
        for hh in range(heads_per_tile):
            sl = slice(hh * HEAD_DIM, (hh + 1) * HEAD_DIM)
            yield sl, acc[:, sl]

    def rope(v):
        return v * cos_ref[...] + pltpu.roll(v, HEAD_DIM // 2, 1) * sin_ref[...]

    @pl.when(sec == 0)
    def _():
        for sl, v in heads():
            o_ref[:, sl] = rope(v).astype(BF16)

    @pl.when(sec == 1)
    def _():
        for sl, v in heads():
            o_ref[:, sl] = (rope(v) * (HEAD_DIM ** -0.5)).astype(BF16)

    @pl.when((sec == 2) | (sec == 6))
    def _():
        o_ref[...] = acc.astype(BF16)

    @pl.when(sec == 3)
    def _():
        o_ref[...] = (acc * _sigmoid(acc)).astype(BF16)

    @pl.when(sec == 4)
    def _():
        for sl, v in heads():
            o_ref[:, sl] = (_rms_rows(v) * qn_ref[...] * (HEAD_DIM ** -0.5)).astype(BF16)

    @pl.when(sec == 5)
    def _():
        for sl, v in heads():
            o_ref[:, sl] = (_rms_rows(v) * kn_ref[...]).astype(BF16)


def _in_projection(x2, mod3, norm_w, w_in, fox_f_bias, fox_qn_w, fox_kn_w, cosf, sinf, nb, s):
    t, d = x2.shape
    wf_t = w_in[:, MAIN_COLS:].T.astype(BF16)
    per_b = s // IN_TM
    grid = (t // IN_TM, MAIN_COLS // IN_TN)
    return pl.pallas_call(
        _inproj_kernel,
        grid=grid,
        in_specs=[
            pl.BlockSpec((IN_TM, d), lambda i, j: (i, 0)),
            pl.BlockSpec((1, N_MOD, d), lambda i, j: (i // per_b, 0, 0)),
            pl.BlockSpec((1, d), lambda i, j: (0, 0)),
            pl.BlockSpec((d, IN_TN), lambda i, j: (0, j)),
            pl.BlockSpec((FOX_HEADS, d), lambda i, j: (0, 0)),
            pl.BlockSpec((FOX_HEADS, 1), lambda i, j: (0, 0)),
            pl.BlockSpec((IN_TM, HEAD_DIM), lambda i, j: (i % per_b, 0)),
            pl.BlockSpec((IN_TM, HEAD_DIM), lambda i, j: (i % per_b, 0)),
            pl.BlockSpec((1, HEAD_DIM), lambda i, j: (0, 0)),
            pl.BlockSpec((1, HEAD_DIM), lambda i, j: (0, 0)),
        ],
        out_specs=[
            pl.BlockSpec((IN_TM, IN_TN), lambda i, j: (i, j)),
            pl.BlockSpec((1, FOX_HEADS, IN_TM), lambda i, j: (i // per_b, 0, i % per_b)),
        ],
        out_shape=[jax.ShapeDtypeStruct((t, MAIN_COLS), BF16),
                   jax.ShapeDtypeStruct((nb, FOX_HEADS, s), F32)],
        scratch_shapes=[pltpu.VMEM((IN_TM, d), BF16)],
        compiler_params=_cparams(("arbitrary", "arbitrary")),
        name="in_projection",
    )(x2, mod3, norm_w.reshape(1, d), w_in, wf_t, fox_f_bias.reshape(FOX_HEADS, 1), cosf, sinf,
      fox_qn_w.reshape(1, HEAD_DIM), fox_kn_w.reshape(1, HEAD_DIM))


def _split3(x):
    hi = x.astype(BF16)
    r1 = x - hi.astype(F32)
    mid = r1.astype(BF16)
    lo = (r1 - mid.astype(F32)).astype(BF16)
    return hi, mid, lo


def _cumsum_kernel(x_ref, o_ref, *, rows_per_seq):
    x = x_ref[...]
    r = x.shape[0]
    ri = lax.broadcasted_iota(I32, (LANES, LANES), 0)
    ci = lax.broadcasted_iota(I32, (LANES, LANES), 1)
    upper = jnp.where(ri <= ci, 1.0, 0.0).astype(BF16)
    rowcum = sum(jnp.dot(p, upper, preferred_element_type=F32) for p in _split3(x))
    tot = jnp.broadcast_to(rowcum[:, LANES - 1:LANES], (r, LANES))
    gi = lax.broadcasted_iota(I32, (r, r), 0)
    gj = lax.broadcasted_iota(I32, (r, r), 1)
    same_seq = (gi // rows_per_seq) == (gj // rows_per_seq)
    lower = jnp.where(same_seq & (gj < gi), 1.0, 0.0).astype(BF16)
    prefix = sum(jnp.dot(lower, p, preferred_element_type=F32) for p in _split3(tot))
    o_ref[...] = rowcum + prefix


def _cum_forget(lf):
    nb, nh, s = lf.shape
    rows = nb * nh * s // LANES
    out = pl.pallas_call(
        functools.partial(_cumsum_kernel, rows_per_seq=s // LANES),
        out_shape=jax.ShapeDtypeStruct((rows, LANES), F32),
        compiler_params=pltpu.CompilerParams(vmem_limit_bytes=VMEM_LIMIT),
        name="forget_cumsum",
    )(lf.reshape(rows, LANES))
    return out.reshape(nb * nh, 1, s)


def _ret_kernel(lg_ref, q_ref, k_ref, v_ref, g_ref, gnw_ref, o_ref):
    c = RET_CHUNK
    s = q_ref.shape[1]
    lg = lg_ref[0][:, 0:1]
    ri = lax.broadcasted_iota(I32, (c, c), 0)
    ci = lax.broadcasted_iota(I32, (c, c), 1)
    diff = (ri - ci).astype(F32)
    decay = jnp.where(diff >= 0.0, jnp.exp(lg * jnp.maximum(diff, 0.0)), 0.0)
    pos = lax.broadcasted_iota(I32, (c, 1), 0).astype(F32)
    zeta = jnp.exp(lg * (float(c - 1) - pos))
    xi = jnp.exp(lg * (pos + 1.0))
    chunk_decay = jnp.exp(lg * float(c))
    gnw = gnw_ref[...]

    def body(n, state):
        r0 = pl.multiple_of(n * c, c)
        q = q_ref[0, pl.ds(r0, c), :]
        k = k_ref[0, pl.ds(r0, c), :]
        v = v_ref[0, pl.ds(r0, c), :]
        scores = _nt_dot(q, k) * decay
        intra = jnp.dot(scores.astype(BF16), v, preferred_element_type=F32)
        cross = jnp.dot(q, state.astype(BF16), preferred_element_type=F32) * xi
        kz = (k.astype(F32) * zeta).astype(BF16)
        kv = _tn_dot(kz, v)
        o = intra + cross
        mu = jnp.mean(o, axis=-1, keepdims=True)
        oc = o - mu
        var = jnp.mean(oc * oc, axis=-1, keepdims=True)
        y = oc * lax.rsqrt(var + EPS) * gnw * g_ref[0, pl.ds(r0, c), :].astype(F32)
        o_ref[0, pl.ds(r0, c), :] = y.astype(BF16)
        return state * chunk_decay + kv

    lax.fori_loop(0, s // c, body, jnp.zeros((HEAD_DIM, HEAD_DIM), F32))


def _retention(proj3, ret_gn_w, lg_tab):
    nb, s, _ = proj3.shape
    hpg = RET_WIDTH // HEAD_DIM

    def col(sec):
        return pl.BlockSpec((1, s, HEAD_DIM), lambda b, h: (b, 0, sec * hpg + h))

    return pl.pallas_call(
        _ret_kernel,
        grid=(nb, RET_HEADS),
        in_specs=[pl.BlockSpec((1, 1, LANES), lambda b, h: (h, 0, 0)),
                  col(0), col(1), col(2), col(3),
                  pl.BlockSpec((1, HEAD_DIM), lambda b, h: (0, h))],
        out_specs=pl.BlockSpec((1, s, HEAD_DIM), lambda b, h: (b, 0, h)),
        out_shape=jax.ShapeDtypeStruct((nb, s, RET_WIDTH), BF16),
        compiler_params=_cparams(("arbitrary", "arbitrary")),
        name="retention",
    )(lg_tab, proj3, proj3, proj3, proj3, ret_gn_w.reshape(1, RET_WIDTH))


def _fox_kernel(q_ref, k_ref, v_ref, cum_ref, onw_ref, o_ref):
    t = FOX_T
    qi = pl.program_id(2)
    q = q_ref[0]
    q0 = pl.multiple_of(qi * t, t)
    ri = lax.broadcasted_iota(I32, (t, t), 0)
    ci = lax.broadcasted_iota(I32, (t, t), 1)
    cq_row = cum_ref[0, :, pl.ds(q0, t)]
    cq = jnp.sum(jnp.where(ri == ci, cq_row, 0.0), axis=1, keepdims=True)

    def step(ki, carry, masked):
        m, l, acc = carry
        k0 = pl.multiple_of(ki * t, t)
        k = k_ref[0, pl.ds(k0, t), :]
        v = v_ref[0, pl.ds(k0, t), :]
        sc = _nt_dot(q, k) + (cq - cum_ref[0, :, pl.ds(k0, t)])
        if masked:
            sc = jnp.where(ci <= ri, sc, NEG)
        m_new = jnp.maximum(m, jnp.max(sc, axis=1, keepdims=True))
        p = jnp.exp(sc - m_new)
        alpha = jnp.exp(m - m_new)
        l = alpha * l + jnp.sum(p, axis=1, keepdims=True)
        acc = alpha * acc + jnp.dot(p.astype(BF16), v, preferred_element_type=F32)
        return m_new, l, acc

    init = (jnp.full((t, 1), NEG, F32), jnp.zeros((t, 1), F32), jnp.zeros((t, HEAD_DIM), F32))
    carry = lax.fori_loop(0, qi, lambda ki, cr: step(ki, cr, False), init)
    _, l, acc = step(qi, carry, True)
    o = acc / l
    o_ref[0] = (_rms_rows(o) * onw_ref[...]).astype(BF16)


def _fox_attention(proj3, cum, fox_on_w):
    nb, s, _ = proj3.shape
    base = 4 * RET_WIDTH // HEAD_DIM
    hpg = FOX_WIDTH // HEAD_DIM
    return pl.pallas_call(
        _fox_kernel,
        grid=(nb, FOX_HEADS, s // FOX_T),
        in_specs=[pl.BlockSpec((1, FOX_T, HEAD_DIM), lambda b, h, i: (b, i, base + h)),
                  pl.BlockSpec((1, s, HEAD_DIM), lambda b, h, i: (b, 0, base + hpg + h)),
                  pl.BlockSpec((1, s, HEAD_DIM), lambda b, h, i: (b, 0, base + 2 * hpg + h)),
                  pl.BlockSpec((1, 1, s), lambda b, h, i: (b * FOX_HEADS + h, 0, 0)),
                  pl.BlockSpec((1, HEAD_DIM), lambda b, h, i: (0, h))],
        out_specs=pl.BlockSpec((1, FOX_T, HEAD_DIM), lambda b, h, i: (b, i, h)),
        out_shape=jax.ShapeDtypeStruct((nb, s, FOX_WIDTH), BF16),
        compiler_params=_cparams(("arbitrary", "arbitrary", "arbitrary")),
        name="fox_attention",
    )(proj3, proj3, proj3, cum, fox_on_w.reshape(1, FOX_WIDTH))


def _pack_halves(v):
    lo = pltpu.bitcast(v[:, :HALF].astype(BF16).astype(F32), U32) >> 16
    hi = pltpu.bitcast(v[:, HALF:].astype(BF16).astype(F32), U32) & jnp.uint32(0xFFFF0000)
    return hi | lo


def _unpack_halves(w):
    lo = pltpu.bitcast(w << 16, F32)
    hi = pltpu.bitcast(w & jnp.uint32(0xFFFF0000), F32)
    return lo, hi


def _outproj_kernel(ret_ref, fox_ref, wo_ref, x_ref, mod_ref, npost_ref, npre_ref, rw_ref, rb_ref,
                    x1_ref, h2_ref, info_ref, wts_ref, cnt_ref, carry_scr):
    i = pl.program_id(0)
    tm = OUT_TM

    @pl.when(i == 0)
    def _():
        carry_scr[...] = jnp.zeros_like(carry_scr)

    y = (jnp.dot(ret_ref[...], wo_ref[:RET_WIDTH, :], preferred_element_type=F32)
         + jnp.dot(fox_ref[...], wo_ref[RET_WIDTH:, :], preferred_element_type=F32))
    gate1 = mod_ref[0, 2:3, :]
    shift2 = mod_ref[0, 3:4, :]
    scale2 = 1.0 + mod_ref[0, 4:5, :]
    x1 = x_ref[...] + gate1 * (_rms_rows(y) * npost_ref[...])
    x1_ref[...] = x1
    h2 = _rms_rows(x1) * npre_ref[...] * scale2 + shift2
    h2_ref[...] = _pack_halves(h2)

    hi = h2.astype(BF16)
    lo = (h2 - hi.astype(F32)).astype(BF16)
    rw = rw_ref[...]
    lg2 = jnp.dot(hi, rw, preferred_element_type=F32) + jnp.dot(lo, rw, preferred_element_type=F32)
    logits = lg2 + pltpu.roll(lg2, LANES // 2, 1) + rb_ref[...]

    lane = lax.broadcasted_iota(I32, (tm, LANES), 1)
    lanef = lane.astype(F32)
    big = float(LANES)

    def first_argmax(vals):
        vmax = jnp.max(vals, axis=1, keepdims=True)
        idx = jnp.min(jnp.where(vals == vmax, lanef, big), axis=1, keepdims=True)
        return vmax, idx

    is_coarse = (lane >= N_EXPERTS) & (lane < N_EXPERTS + N_GROUPS)
    cvals = jnp.where(is_coarse, logits, NEG)
    cmax, cidx = first_argmax(cvals)
    gidx = cidx - float(N_EXPERTS)
    gprob = 1.0 / jnp.sum(jnp.where(is_coarse, jnp.exp(cvals - cmax), 0.0), axis=1, keepdims=True)
    glo = gidx * float(EXPERTS_PER_GROUP)
    in_group = (lanef >= glo) & (lanef < glo + float(EXPERTS_PER_GROUP))
    fvals = jnp.where(in_group, logits, NEG)
    v1, e1 = first_argmax(fvals)
    fvals2 = jnp.where(lanef == e1, NEG, fvals)
    v2, e2 = first_argmax(fvals2)
    ex = jnp.exp(v2 - v1)
    w1 = gprob * (1.0 / (1.0 + ex))
    w2 = gprob * (ex / (1.0 + ex))

    sel1 = lanef == e1
    sel2 = lanef == e2
    onehot = jnp.where(sel1 | sel2, 1.0, 0.0)
    ri = lax.broadcasted_iota(I32, (tm, tm), 0)
    ci = lax.broadcasted_iota(I32, (tm, tm), 1)
    lower = jnp.where(ci < ri, 1.0, 0.0).astype(BF16)
    before = jnp.dot(lower, onehot.astype(BF16), preferred_element_type=F32) + carry_scr[...]
    r1 = jnp.sum(jnp.where(sel1, before, 0.0), axis=1, keepdims=True)
    r2 = jnp.sum(jnp.where(sel2, before, 0.0), axis=1, keepdims=True)
    carry_new = carry_scr[...] + jnp.sum(onehot, axis=0, keepdims=True)
    carry_scr[...] = carry_new
    cnt_ref[...] = carry_new.astype(I32)

    fields = jnp.where(lane == 0, e1, jnp.where(lane == 1, e2, jnp.where(lane == 2, r1, jnp.where(lane == 3, r2, 0.0))))
    info_ref[...] = fields.astype(I32)
    wts_ref[...] = jnp.where(lane == 0, w1, jnp.where(lane == 1, w2, 0.0))


def _out_projection(ret2, fox2, w_out_bf, x2, mod3, npost, npre, rw, rb, s):
    t, d = x2.shape
    per_b = s // OUT_TM
    row = lambda i: (i, 0)
    const = lambda i: (0, 0)
    return pl.pallas_call(
        _outproj_kernel,
        grid=(t // OUT_TM,),
        in_specs=[pl.BlockSpec((OUT_TM, RET_WIDTH), row),
                  pl.BlockSpec((OUT_TM, FOX_WIDTH), row),
                  pl.BlockSpec((d, d), const),
                  pl.BlockSpec((OUT_TM, d), row),
                  pl.BlockSpec((1, N_MOD, d), lambda i: (i // per_b, 0, 0)),
                  pl.BlockSpec((1, d), const),
                  pl.BlockSpec((1, d), const),
                  pl.BlockSpec((d, LANES), const),
                  pl.BlockSpec((1, LANES), const)],
        out_specs=[pl.BlockSpec((OUT_TM, d), row),
                   pl.BlockSpec((OUT_TM, HALF), row),
                   pl.BlockSpec((OUT_TM, LANES), row),
                   pl.BlockSpec((OUT_TM, LANES), row),
                   pl.BlockSpec((1, LANES), const)],
        out_shape=[jax.ShapeDtypeStruct((t, d), F32),
                   jax.ShapeDtypeStruct((t, HALF), U32),
                   jax.ShapeDtypeStruct((t, LANES), I32),
                   jax.ShapeDtypeStruct((t, LANES), F32),
                   jax.ShapeDtypeStruct((1, LANES), I32)],
        scratch_shapes=[pltpu.VMEM((1, LANES), F32)],
        compiler_params=_cparams(("arbitrary",)),
        name="out_projection_router",
    )(ret2, fox2, w_out_bf, x2, mod3, npost.reshape(1, d), npre.reshape(1, d), rw, rb)


def _cast_kernel(w_ref, o_ref):
    o_ref[...] = w_ref[...].astype(BF16)


def _cast_bf16(w, tr=256):
    r, c = w.shape
    return pl.pallas_call(
        _cast_kernel,
        grid=(r // tr,),
        in_specs=[pl.BlockSpec((tr, c), lambda i: (i, 0))],
        out_specs=pl.BlockSpec((tr, c), lambda i: (i, 0)),
        out_shape=jax.ShapeDtypeStruct((r, c), BF16),
        compiler_params=_cparams(("arbitrary",)),
        name="cast_bf16",
    )(w)


def _dispatch_kernel(pa_ref, pb_ref, zoff_ref, h2_hbm, xs_hbm, zbuf, sem, zsem):
    t = h2_hbm.shape[0]

    zbuf[...] = jnp.zeros_like(zbuf)

    def zero_copy(e):
        return pltpu.make_async_copy(zbuf, xs_hbm.at[pl.ds(pl.multiple_of(zoff_ref[e], EXP_TM), EXP_TM), :], zsem)

    for e in range(N_EXPERTS):
        @pl.when(zoff_ref[e] >= 0)
        def _():
            zero_copy(e).start()
    for e in range(N_EXPERTS):
        @pl.when(zoff_ref[e] >= 0)
        def _():
            zero_copy(e).wait()

    def row_copy(tok, pos_ref):
        return pltpu.make_async_copy(h2_hbm.at[pl.ds(tok, 1), :], xs_hbm.at[pl.ds(pos_ref[tok], 1), :], sem)

    def body(tok, _):
        row_copy(tok, pa_ref).start()
        row_copy(tok, pb_ref).start()

        @pl.when(tok >= DMA_LAG)
        def _():
            row_copy(tok - DMA_LAG, pa_ref).wait()
            row_copy(tok - DMA_LAG, pb_ref).wait()
        return 0

    lax.fori_loop(0, t, body, 0)

    def drain(tok, _):
        row_copy(tok, pa_ref).wait()
        row_copy(tok, pb_ref).wait()
        return 0

    lax.fori_loop(t - DMA_LAG, t, drain, 0)


def _dispatch(h2p, pos_a, pos_b, zoff, n_rows):
    t, hw = h2p.shape
    return pl.pallas_call(
        _dispatch_kernel,
        grid_spec=pltpu.PrefetchScalarGridSpec(
            num_scalar_prefetch=3,
            grid=(1,),
            in_specs=[pl.BlockSpec(memory_space=pl.ANY)],
            out_specs=pl.BlockSpec(memory_space=pl.ANY),
            scratch_shapes=[pltpu.VMEM((EXP_TM, hw), U32),
                            pltpu.SemaphoreType.DMA(()),
                            pltpu.SemaphoreType.DMA(())]),
        out_shape=jax.ShapeDtypeStruct((n_rows, hw), U32),
        compiler_params=_cparams(("arbitrary",)),
        name="moe_dispatch",
    )(pos_a, pos_b, zoff, h2p)


def _expert_kernel(te_ref, nt_ref, xs_ref, w1_ref, w3_ref, w2_ref, ys_ref, w1_s, w3_s, w2_s):
    i = pl.program_id(0)
    prev = te_ref[jnp.maximum(i - 1, 0)]
    valid = i < nt_ref[0]

    @pl.when(valid & ((i == 0) | (te_ref[i] != prev)))
    def _():
        w1_s[...] = w1_ref[0].astype(BF16)
        w3_s[...] = w3_ref[0].astype(BF16)
        w2_s[...] = w2_ref[0].astype(BF16)

    @pl.when(valid)
    def _():
        lo, hi = _unpack_halves(xs_ref[...])
        lo = lo.astype(BF16)
        hi = hi.astype(BF16)
        a = (jnp.dot(lo, w1_s[:HALF, :], preferred_element_type=F32)
             + jnp.dot(hi, w1_s[HALF:, :], preferred_element_type=F32))
        g = (jnp.dot(lo, w3_s[:HALF, :], preferred_element_type=F32)
             + jnp.dot(hi, w3_s[HALF:, :], preferred_element_type=F32))
        act = (a * _sigmoid(a) * g).astype(BF16)
        y = jnp.dot(act, w2_s[...], preferred_element_type=F32)
        ys_ref[...] = _pack_halves(y)


def _expert_gemm(xs, tile_expert, num_tiles, w1, w3, w2):
    n_rows, hw = xs.shape
    n_tiles = n_rows // EXP_TM
    d, de = w1.shape[1], w1.shape[2]

    def row(i, te, nt):
        return (jnp.minimum(i, nt[0] - 1), 0)

    def wsel(i, te, nt):
        return (te[i], 0, 0)

    return pl.pallas_call(
        _expert_kernel,
        grid_spec=pltpu.PrefetchScalarGridSpec(
            num_scalar_prefetch=2,
            grid=(n_tiles,),
            in_specs=[pl.BlockSpec((EXP_TM, hw), row),
                      pl.BlockSpec((1, d, de), wsel),
                      pl.BlockSpec((1, d, de), wsel),
                      pl.BlockSpec((1, de, d), wsel)],
            out_specs=pl.BlockSpec((EXP_TM, hw), row),
            scratch_shapes=[pltpu.VMEM((d, de), BF16),
                            pltpu.VMEM((d, de), BF16),
                            pltpu.VMEM((de, d), BF16)]),
        out_shape=jax.ShapeDtypeStruct((n_rows, hw), U32),
        compiler_params=_cparams(("arbitrary",)),
        name="moe_expert_gemm",
    )(tile_expert, num_tiles, xs, w1, w3, w2)


def _combine_kernel(pa_ref, pb_ref, ys_hbm, x1_ref, wts_ref, mod_ref, npost_ref, o_ref, buf_a, buf_b, sems):
    i = pl.program_id(0)
    n = pl.num_programs(0)
    tm = CMB_TM

    def copies(tile, slot, r):
        tok = tile * tm + r
        ca = pltpu.make_async_copy(ys_hbm.at[pl.ds(pa_ref[tok], 1), :], buf_a.at[slot, pl.ds(r, 1), :], sems.at[slot])
        cb = pltpu.make_async_copy(ys_hbm.at[pl.ds(pb_ref[tok], 1), :], buf_b.at[slot, pl.ds(r, 1), :], sems.at[slot])
        return ca, cb

    def start_tile(tile, slot):
        def body(r, _):
            ca, cb = copies(tile, slot, r)
            ca.start()
            cb.start()
            return 0
        lax.fori_loop(0, tm, body, 0)

    def wait_tile(tile, slot):
        def body(r, _):
            ca, cb = copies(tile, slot, r)
            ca.wait()
            cb.wait()
            return 0
        lax.fori_loop(0, tm, body, 0)

    slot = i % 2

    @pl.when(i == 0)
    def _():
        start_tile(0, 0)

    @pl.when(i + 1 < n)
    def _():
        start_tile(i + 1, 1 - slot)

    wait_tile(i, slot)

    wa = wts_ref[:, 0:1]
    wb = wts_ref[:, 1:2]
    alo, ahi = _unpack_halves(buf_a[slot])
    blo, bhi = _unpack_halves(buf_b[slot])
    ylo = wa * alo + wb * blo
    yhi = wa * ahi + wb * bhi
    ms = (jnp.sum(ylo * ylo, axis=-1, keepdims=True) + jnp.sum(yhi * yhi, axis=-1, keepdims=True)) / float(D_MODEL)
    inv = lax.rsqrt(ms + EPS)
    gate2 = mod_ref[0, 5:6, :]
    npost = npost_ref[...]
    o_ref[:, :HALF] = x1_ref[:, :HALF] + gate2[:, :HALF] * (ylo * inv * npost[:, :HALF])
    o_ref[:, HALF:] = x1_ref[:, HALF:] + gate2[:, HALF:] * (yhi * inv * npost[:, HALF:])


def _combine(ys, pos_a, pos_b, x1, wts, mod3, npost, s):
    t, d = x1.shape
    hw = ys.shape[1]
    per_b = s // CMB_TM
    row = lambda i, pa, pb: (i, 0)
    return pl.pallas_call(
        _combine_kernel,
        grid_spec=pltpu.PrefetchScalarGridSpec(
            num_scalar_prefetch=2,
            grid=(t // CMB_TM,),
            in_specs=[pl.BlockSpec(memory_space=pl.ANY),
                      pl.BlockSpec((CMB_TM, d), row),
                      pl.BlockSpec((CMB_TM, LANES), row),
                      pl.BlockSpec((1, N_MOD, d), lambda i, pa, pb: (i // per_b, 0, 0)),
                      pl.BlockSpec((1, d), lambda i, pa, pb: (0, 0))],
            out_specs=pl.BlockSpec((CMB_TM, d), row),
            scratch_shapes=[pltpu.VMEM((2, CMB_TM, hw), U32),
                            pltpu.VMEM((2, CMB_TM, hw), U32),
                            pltpu.SemaphoreType.DMA((2,))]),
        out_shape=jax.ShapeDtypeStruct((t, d), F32),
        compiler_params=_cparams(("arbitrary",)),
        name="moe_combine",
    )(pos_a, pos_b, ys, x1, wts, mod3, npost.reshape(1, d))


def _rope_tables(s):
    inv_freq = ROPE_BASE ** (-jnp.arange(0, HEAD_DIM, 2, dtype=F32) / HEAD_DIM)
    angle = jnp.arange(s, dtype=F32)[:, None] * inv_freq[None, :]
    cos = jnp.cos(angle)
    sin = jnp.sin(angle)
    return jnp.concatenate([cos, cos], axis=-1), jnp.concatenate([-sin, sin], axis=-1)


def _router_weights(router_coarse_w, router_fine_w, router_coarse_b, router_fine_b):
    d = router_fine_w.shape[0]
    w = jnp.concatenate([router_fine_w, router_coarse_w], axis=1)
    n = w.shape[1]
    w_hi = w.astype(BF16)
    w_lo = (w - w_hi.astype(F32)).astype(BF16)
    pad = jnp.zeros((d, LANES // 2 - n), BF16)
    rw = jnp.concatenate([w_hi, pad, w_lo, pad], axis=1)
    rb = jnp.concatenate([router_fine_b, router_coarse_b, jnp.zeros((LANES - n,), F32)]).reshape(1, LANES)
    return rw, rb


def _layer(x2, c, nb, s, cosf, sinf, lg_tab, ada_w, ada_b, norm_pre_mix, norm_post_mix, w_in, fox_f_bias,
           ret_gn_w, fox_qn_w, fox_kn_w, fox_on_w, w_out, norm_pre_ffn, norm_post_ffn,
           router_coarse_w, router_coarse_b, router_fine_w, router_fine_b, w1, w3, w2):
    t, d = x2.shape
    mod3 = _modulation(c, ada_w, ada_b).reshape(nb, N_MOD, d)

    proj, lf = _in_projection(x2, mod3, norm_pre_mix, w_in, fox_f_bias, fox_qn_w, fox_kn_w, cosf, sinf, nb, s)
    cum = _cum_forget(lf)
    proj3 = proj.reshape(nb, s, MAIN_COLS)
    ret = _retention(proj3, ret_gn_w, lg_tab)
    fox = _fox_attention(proj3, cum, fox_on_w)

    rw, rb = _router_weights(router_coarse_w, router_fine_w, router_coarse_b, router_fine_b)
    x1, h2p, info, wts, counts = _out_projection(
        ret.reshape(t, RET_WIDTH), fox.reshape(t, FOX_WIDTH), _cast_bf16(w_out), x2, mod3,
        norm_post_mix, norm_pre_ffn, rw, rb, s)

    n_tiles = (t * 2) // EXP_TM + N_EXPERTS
    cnt = counts[0, :N_EXPERTS]
    tiles_per = (cnt + EXP_TM - 1) // EXP_TM
    tile_end = jnp.cumsum(tiles_per)
    tile_start = tile_end - tiles_per
    num_tiles = tile_end[-1:]
    tile_expert = jnp.searchsorted(tile_end, jnp.arange(n_tiles, dtype=I32), side="right").astype(I32)
    last_expert = jnp.searchsorted(tile_end, num_tiles[0] - 1, side="right").astype(I32)
    tile_expert = jnp.minimum(tile_expert, last_expert)
    row_start = (tile_start * EXP_TM).astype(I32)
    pos_a = row_start[info[:, 0]] + info[:, 2]
    pos_b = row_start[info[:, 1]] + info[:, 3]
    zoff = jnp.where(tiles_per > 0, (tile_end - 1) * EXP_TM, -1).astype(I32)

    xs = _dispatch(h2p, pos_a, pos_b, zoff, n_tiles * EXP_TM)
    ys = _expert_gemm(xs, tile_expert, num_tiles.astype(I32), w1, w3, w2)
    return _combine(ys, pos_a, pos_b, x1, wts, mod3, norm_post_ffn, s)


def kernel(x, c, ada_w, ada_b, norm_pre_mix, norm_post_mix, w_in, fox_f_bias, ret_gn_w, fox_qn_w, fox_kn_w,
           fox_on_w, w_out, norm_pre_ffn, norm_post_ffn, router_coarse_w, router_coarse_b, router_fine_w,
           router_fine_b, w1, w3, w2):
    nb, s, d = x.shape
    depth = ada_w.shape[0]
    cosf, sinf = _rope_tables(s)
    log_gamma = jnp.log1p(-jnp.exp2(-5.0 - jnp.arange(RET_HEADS, dtype=F32)))
    lg_tab = jnp.broadcast_to(log_gamma[:, None, None], (RET_HEADS, 1, LANES))
    x2 = x.reshape(nb * s, d)
    for l in range(depth):
        x2 = _layer(x2, c, nb, s, cosf, sinf, lg_tab, ada_w[l], ada_b[l], norm_pre_mix[l], norm_post_mix[l],
                    w_in[l], fox_f_bias[l], ret_gn_w[l], fox_qn_w[l], fox_kn_w[l], fox_on_w[l], w_out[l],
                    norm_pre_ffn[l], norm_post_ffn[l], router_coarse_w[l], router_coarse_b[l],
                    router_fine_w[l], router_fine_b[l], w1[l], w3[l], w2[l])
    return x2.reshape(nb, s, d)
```

```python
import functools

import jax
import jax.numpy as jnp
from jax import lax
from jax.experimental import pallas as pl
from jax.experimental.pallas import tpu as pltpu

F32 = jnp.float32
BF16 = jnp.bfloat16
I32 = jnp.int32
U32 = jnp.uint32

D_MODEL = 2048
HEAD_DIM = 128
RET_WIDTH = D_MODEL // 2
FOX_WIDTH = D_MODEL - RET_WIDTH
RET_HEADS = RET_WIDTH // HEAD_DIM
FOX_HEADS = FOX_WIDTH // HEAD_DIM
ROPE_BASE = 10000.0
N_GROUPS = 4
EXPERTS_PER_GROUP = 8
N_EXPERTS = N_GROUPS * EXPERTS_PER_GROUP
D_EXPERT = D_MODEL // 4
N_MOD = 6
EPS = 1e-6
MAIN_COLS = 4 * RET_WIDTH + 3 * FOX_WIDTH

LANES = 128
SUBLANES = 8
HALF = D_MODEL // 2
ROW_TILES = HALF // LANES
NEG = -1e30

VMEM_LIMIT = 56 * 1024 * 1024

MOD_TN = 1024
IN_TM, IN_TN = 1024, 512
RET_CHUNK = 256
FOX_T = 512
OUT_TM = 256
EXP_TM = 256
CMB_TM = 256
DMA_UNROLL = 8


def _cparams(sem):
    return pltpu.CompilerParams(dimension_semantics=sem, vmem_limit_bytes=VMEM_LIMIT)


def _nt_dot(a, b):
    return lax.dot_general(a, b, (((1,), (1,)), ((), ())), preferred_element_type=F32)


def _tn_dot(a, b):
    return lax.dot_general(a, b, (((0,), (0,)), ((), ())), preferred_element_type=F32)


def _sigmoid(x):
    return 1.0 / (1.0 + jnp.exp(-x))


def _col_to_row(col):
    n = col.shape[0]
    ri = lax.broadcasted_iota(I32, (n, n), 0)
    ci = lax.broadcasted_iota(I32, (n, n), 1)
    return jnp.sum(jnp.where(ri == ci, col, 0.0), axis=0, keepdims=True)


def _row_to_col(row):
    n = row.shape[1]
    ri = lax.broadcasted_iota(I32, (n, n), 0)
    ci = lax.broadcasted_iota(I32, (n, n), 1)
    return jnp.sum(jnp.where(ri == ci, row, 0.0), axis=1, keepdims=True)


def _mod_kernel(cb_ref, w_ref, b_ref, o_ref):
    nb = cb_ref.shape[0]
    for b in range(nb):
        cv = cb_ref[b]
        ca = cv * _sigmoid(cv)
        for j in range(MOD_TN // LANES):
            sl = slice(j * LANES, (j + 1) * LANES)
            col = jnp.sum(w_ref[:, sl] * ca, axis=0, keepdims=True)
            o_ref[b:b + 1, sl] = col + b_ref[:, sl]


def _modulation(c, ada_w, ada_b, layer):
    nb, d = c.shape
    n = ada_w.shape[2]
    cb = jnp.broadcast_to(c[:, :, None], (nb, d, LANES))
    return pl.pallas_call(
        _mod_kernel,
        grid=(n // MOD_TN,),
        in_specs=[pl.BlockSpec((nb, d, LANES), lambda j: (0, 0, 0)),
                  pl.BlockSpec((None, d, MOD_TN), lambda j: (layer, 0, j)),
                  pl.BlockSpec((None, 1, MOD_TN), lambda j: (layer, 0, j))],
        out_specs=pl.BlockSpec((nb, MOD_TN), lambda j: (0, j)),
        out_shape=jax.ShapeDtypeStruct((nb, n), F32),
        compiler_params=_cparams(("arbitrary",)),
        name="adaln_mod",
    )(cb, ada_w, ada_b.reshape(ada_b.shape[0], 1, n))


def _rms_rows(x):
    return x * lax.rsqrt(jnp.mean(x * x, axis=-1, keepdims=True) + EPS)


def _inproj_kernel(x_ref, mod_ref, nw_ref, w_ref, wf_ref, fb_ref, cos_ref, sin_ref, qn_ref, kn_ref,
                   o_ref, lf_ref, h_scr):
    j = pl.program_id(1)
    tiles_per_sec = RET_WIDTH // IN_TN
    heads_per_tile = IN_TN // HEAD_DIM

    @pl.when(j == 0)
    def _():
        shift = mod_ref[0, 0:1, :]
        scale = 1.0 + mod_ref[0, 1:2, :]
        nw = nw_ref[...]
        rows = 128
        for r in range(IN_TM // rows):
            rs = slice(r * rows, (r + 1) * rows)
            h = _rms_rows(x_ref[rs, :]) * nw * scale + shift
            h_scr[rs, :] = h.astype(BF16)
        z = _nt_dot(wf_ref[...], h_scr[...]) + fb_ref[...]
        lf_ref[0] = jnp.minimum(z, 0.0) - jnp.log1p(jnp.exp(-jnp.abs(z)))

    acc = jnp.dot(h_scr[...], w_ref[...].astype(BF16), preferred_element_type=F32)
    sec = j // tiles_per_sec

    def heads():
        for hh in range(heads_per_tile):
            sl = slice(hh * HEAD_DIM, (hh + 1) * HEAD_DIM)
            yield sl, acc[:, sl]

    def rope(v):
        return v * cos_ref[...] + pltpu.roll(v, HEAD_DIM // 2, 1) * sin_ref[...]

    @pl.when(sec == 0)
    def _():
        for sl, v in heads():
            o_ref[:, sl] = rope(v).astype(BF16)

    @pl.when(sec == 1)
    def _():
        for sl, v in heads():
            o_ref[:, sl] = (rope(v) * (HEAD_DIM ** -0.5)).astype(BF16)

    @pl.when((sec == 2) | (sec == 6))
    def _():
        o_ref[...] = acc.astype(BF16)

    @pl.when(sec == 3)
    def _():
        o_ref[...] = (acc * _sigmoid(acc)).astype(BF16)

    @pl.when(sec == 4)
    def _():
        for sl, v in heads():
            o_ref[:, sl] = (_rms_rows(v) * qn_ref[...] * (HEAD_DIM ** -0.5)).astype(BF16)

    @pl.when(sec == 5)
    def _():
        for sl, v in heads():
            o_ref[:, sl] = (_rms_rows(v) * kn_ref[...]).astype(BF16)


def _in_projection(x2, mod3, norm_w, w_in, fox_f_bias, fox_qn_w, fox_kn_w, cosf, sinf, nb, s, layer):
    t, d = x2.shape
    wf_t = w_in[layer, :, MAIN_COLS:].T.astype(BF16)
    per_b = s // IN_TM
    grid = (t // IN_TM, MAIN_COLS // IN_TN)
    return pl.pallas_call(
        _inproj_kernel,
        grid=grid,
        in_specs=[
            pl.BlockSpec((IN_TM, d), lambda i, j: (i, 0)),
            pl.BlockSpec((1, N_MOD, d), lambda i, j: (i // per_b, 0, 0)),
            pl.BlockSpec((1, d), lambda i, j: (0, 0)),
            pl.BlockSpec((None, d, IN_TN), lambda i, j: (layer, 0, j)),
            pl.BlockSpec((FOX_HEADS, d), lambda i, j: (0, 0)),
            pl.BlockSpec((FOX_HEADS, 1), lambda i, j: (0, 0)),
            pl.BlockSpec((IN_TM, HEAD_DIM), lambda i, j: (i % per_b, 0)),
            pl.BlockSpec((IN_TM, HEAD_DIM), lambda i, j: (i % per_b, 0)),
            pl.BlockSpec((1, HEAD_DIM), lambda i, j: (0, 0)),
            pl.BlockSpec((1, HEAD_DIM), lambda i, j: (0, 0)),
        ],
        out_specs=[
            pl.BlockSpec((IN_TM, IN_TN), lambda i, j: (i, j)),
            pl.BlockSpec((1, FOX_HEADS, IN_TM), lambda i, j: (i // per_b, 0, i % per_b)),
        ],
        out_shape=[jax.ShapeDtypeStruct((t, MAIN_COLS), BF16),
                   jax.ShapeDtypeStruct((nb, FOX_HEADS, s), F32)],
        scratch_shapes=[pltpu.VMEM((IN_TM, d), BF16)],
        compiler_params=_cparams(("arbitrary", "arbitrary")),
        name="in_projection",
    )(x2, mod3, norm_w.reshape(1, d), w_in, wf_t, fox_f_bias.reshape(FOX_HEADS, 1), cosf, sinf,
      fox_qn_w.reshape(1, HEAD_DIM), fox_kn_w.reshape(1, HEAD_DIM))


def _split3(x):
    hi = x.astype(BF16)
    r1 = x - hi.astype(F32)
    mid = r1.astype(BF16)
    lo = (r1 - mid.astype(F32)).astype(BF16)
    return hi, mid, lo


def _cumsum_kernel(x_ref, o_ref, *, rows_per_seq):
    x = x_ref[...]
    r = x.shape[0]
    ri = lax.broadcasted_iota(I32, (LANES, LANES), 0)
    ci = lax.broadcasted_iota(I32, (LANES, LANES), 1)
    upper = jnp.where(ri <= ci, 1.0, 0.0).astype(BF16)
    rowcum = sum(jnp.dot(p, upper, preferred_element_type=F32) for p in _split3(x))
    tot = jnp.broadcast_to(rowcum[:, LANES - 1:LANES], (r, LANES))
    gi = lax.broadcasted_iota(I32, (r, r), 0)
    gj = lax.broadcasted_iota(I32, (r, r), 1)
    same_seq = (gi // rows_per_seq) == (gj // rows_per_seq)
    lower = jnp.where(same_seq & (gj < gi), 1.0, 0.0).astype(BF16)
    prefix = sum(jnp.dot(lower, p, preferred_element_type=F32) for p in _split3(tot))
    o_ref[...] = rowcum + prefix


def _cum_forget(lf):
    nb, nh, s = lf.shape
    rows = nb * nh * s // LANES
    out = pl.pallas_call(
        functools.partial(_cumsum_kernel, rows_per_seq=s // LANES),
        out_shape=jax.ShapeDtypeStruct((rows, LANES), F32),
        compiler_params=pltpu.CompilerParams(vmem_limit_bytes=VMEM_LIMIT),
        name="forget_cumsum",
    )(lf.reshape(rows, LANES))
    return out.reshape(nb * nh, 1, s)


def _ret_kernel(lg_ref, q_ref, k_ref, v_ref, g_ref, gnw_ref, o_ref):
    c = RET_CHUNK
    s = q_ref.shape[1]
    lg = lg_ref[0][:, 0:1]
    ri = lax.broadcasted_iota(I32, (c, c), 0)
    ci = lax.broadcasted_iota(I32, (c, c), 1)
    diff = (ri - ci).astype(F32)
    decay = jnp.where(diff >= 0.0, jnp.exp(lg * jnp.maximum(diff, 0.0)), 0.0)
    pos = lax.broadcasted_iota(I32, (c, 1), 0).astype(F32)
    zeta = jnp.exp(lg * (float(c - 1) - pos))
    xi = jnp.exp(lg * (pos + 1.0))
    chunk_decay = jnp.exp(lg * float(c))
    gnw = gnw_ref[...]

    def body(n, state):
        r0 = pl.multiple_of(n * c, c)
        q = q_ref[0, pl.ds(r0, c), :]
        k = k_ref[0, pl.ds(r0, c), :]
        v = v_ref[0, pl.ds(r0, c), :]
        scores = _nt_dot(q, k) * decay
        intra = jnp.dot(scores.astype(BF16), v, preferred_element_type=F32)
        cross = jnp.dot(q, state.astype(BF16), preferred_element_type=F32) * xi
        kz = (k.astype(F32) * zeta).astype(BF16)
        kv = _tn_dot(kz, v)
        o = intra + cross
        mu = jnp.mean(o, axis=-1, keepdims=True)
        oc = o - mu
        var = jnp.mean(oc * oc, axis=-1, keepdims=True)
        y = oc * lax.rsqrt(var + EPS) * gnw * g_ref[0, pl.ds(r0, c), :].astype(F32)
        o_ref[0, pl.ds(r0, c), :] = y.astype(BF16)
        return state * chunk_decay + kv

    lax.fori_loop(0, s // c, body, jnp.zeros((HEAD_DIM, HEAD_DIM), F32))


def _retention(proj3, ret_gn_w, lg_tab):
    nb, s, _ = proj3.shape
    hpg = RET_WIDTH // HEAD_DIM

    def col(sec):
        return pl.BlockSpec((1, s, HEAD_DIM), lambda b, h: (b, 0, sec * hpg + h))

    return pl.pallas_call(
        _ret_kernel,
        grid=(nb, RET_HEADS),
        in_specs=[pl.BlockSpec((1, 1, LANES), lambda b, h: (h, 0, 0)),
                  col(0), col(1), col(2), col(3),
                  pl.BlockSpec((1, HEAD_DIM), lambda b, h: (0, h))],
        out_specs=pl.BlockSpec((1, s, HEAD_DIM), lambda b, h: (b, 0, h)),
        out_shape=jax.ShapeDtypeStruct((nb, s, RET_WIDTH), BF16),
        compiler_params=_cparams(("arbitrary", "arbitrary")),
        name="retention",
    )(lg_tab, proj3, proj3, proj3, proj3, ret_gn_w.reshape(1, RET_WIDTH))


def _fox_kernel(q_ref, k_ref, v_ref, cum_ref, onw_ref, o_ref):
    t = FOX_T
    qi = pl.program_id(2)
    q = q_ref[0]
    q0 = pl.multiple_of(qi * t, t)
    ri = lax.broadcasted_iota(I32, (t, t), 0)
    ci = lax.broadcasted_iota(I32, (t, t), 1)
    cq = _row_to_col(cum_ref[0, :, pl.ds(q0, t)])

    def step(ki, carry, masked):
        m, l, acc = carry
        k0 = pl.multiple_of(ki * t, t)
        k = k_ref[0, pl.ds(k0, t), :]
        v = v_ref[0, pl.ds(k0, t), :]
        sc = _nt_dot(q, k) + (cq - cum_ref[0, :, pl.ds(k0, t)])
        if masked:
            sc = jnp.where(ci <= ri, sc, NEG)
        m_new = jnp.maximum(m, jnp.max(sc, axis=1, keepdims=True))
        p = jnp.exp(sc - m_new)
        alpha = jnp.exp(m - m_new)
        l = alpha * l + jnp.sum(p, axis=1, keepdims=True)
        acc = alpha * acc + jnp.dot(p.astype(BF16), v, preferred_element_type=F32)
        return m_new, l, acc

    init = (jnp.full((t, 1), NEG, F32), jnp.zeros((t, 1), F32), jnp.zeros((t, HEAD_DIM), F32))
    carry = lax.fori_loop(0, qi, lambda ki, cr: step(ki, cr, False), init)
    _, l, acc = step(qi, carry, True)
    o = acc / l
    o_ref[0] = (_rms_rows(o) * onw_ref[...]).astype(BF16)


def _fox_attention(proj3, cum, fox_on_w):
    nb, s, _ = proj3.shape
    base = 4 * RET_WIDTH // HEAD_DIM
    hpg = FOX_WIDTH // HEAD_DIM
    return pl.pallas_call(
        _fox_kernel,
        grid=(nb, FOX_HEADS, s // FOX_T),
        in_specs=[pl.BlockSpec((1, FOX_T, HEAD_DIM), lambda b, h, i: (b, i, base + h)),
                  pl.BlockSpec((1, s, HEAD_DIM), lambda b, h, i: (b, 0, base + hpg + h)),
                  pl.BlockSpec((1, s, HEAD_DIM), lambda b, h, i: (b, 0, base + 2 * hpg + h)),
                  pl.BlockSpec((1, 1, s), lambda b, h, i: (b * FOX_HEADS + h, 0, 0)),
                  pl.BlockSpec((1, HEAD_DIM), lambda b, h, i: (0, h))],
        out_specs=pl.BlockSpec((1, FOX_T, HEAD_DIM), lambda b, h, i: (b, i, h)),
        out_shape=jax.ShapeDtypeStruct((nb, s, FOX_WIDTH), BF16),
        compiler_params=_cparams(("arbitrary", "arbitrary", "arbitrary")),
        name="fox_attention",
    )(proj3, proj3, proj3, cum, fox_on_w.reshape(1, FOX_WIDTH))


def _pack_halves(v):
    lo = pltpu.bitcast(v[:, :HALF].astype(BF16).astype(F32), U32) >> 16
    hi = pltpu.bitcast(v[:, HALF:].astype(BF16).astype(F32), U32) & jnp.uint32(0xFFFF0000)
    return hi | lo


def _unpack_halves(w):
    lo = pltpu.bitcast(w << 16, F32)
    hi = pltpu.bitcast(w & jnp.uint32(0xFFFF0000), F32)
    return lo, hi


def _outproj_kernel(ret_ref, fox_ref, wo_ref, x_ref, mod_ref, npost_ref, npre_ref, rw_ref, rb_ref,
                    x1_ref, h2_ref, idx_ref, wts_ref, cnt_ref, carry_scr):
    i = pl.program_id(0)
    tm = OUT_TM

    @pl.when(i == 0)
    def _():
        carry_scr[...] = jnp.zeros_like(carry_scr)

    y = (jnp.dot(ret_ref[...], wo_ref[:RET_WIDTH, :], preferred_element_type=F32)
         + jnp.dot(fox_ref[...], wo_ref[RET_WIDTH:, :], preferred_element_type=F32))
    gate1 = mod_ref[0, 2:3, :]
    shift2 = mod_ref[0, 3:4, :]
    scale2 = 1.0 + mod_ref[0, 4:5, :]
    x1 = x_ref[...] + gate1 * (_rms_rows(y) * npost_ref[...])
    x1_ref[...] = x1
    h2 = _rms_rows(x1) * npre_ref[...] * scale2 + shift2
    h2_ref[...] = _pack_halves(h2).reshape(tm, ROW_TILES, LANES)

    hi = h2.astype(BF16)
    lo = (h2 - hi.astype(F32)).astype(BF16)
    rw = rw_ref[...]
    lg2 = jnp.dot(hi, rw, preferred_element_type=F32) + jnp.dot(lo, rw, preferred_element_type=F32)
    logits = lg2 + pltpu.roll(lg2, LANES // 2, 1) + rb_ref[...]

    lane = lax.broadcasted_iota(I32, (tm, LANES), 1)
    lanef = lane.astype(F32)
    big = float(LANES)

    def first_argmax(vals):
        vmax = jnp.max(vals, axis=1, keepdims=True)
        idx = jnp.min(jnp.where(vals == vmax, lanef, big), axis=1, keepdims=True)
        return vmax, idx

    is_coarse = (lane >= N_EXPERTS) & (lane < N_EXPERTS + N_GROUPS)
    cvals = jnp.where(is_coarse, logits, NEG)
    cmax, cidx = first_argmax(cvals)
    gidx = cidx - float(N_EXPERTS)
    gprob = 1.0 / jnp.sum(jnp.where(is_coarse, jnp.exp(cvals - cmax), 0.0), axis=1, keepdims=True)
    glo = gidx * float(EXPERTS_PER_GROUP)
    in_group = (lanef >= glo) & (lanef < glo + float(EXPERTS_PER_GROUP))
    fvals = jnp.where(in_group, logits, NEG)
    v1, e1 = first_argmax(fvals)
    fvals2 = jnp.where(lanef == e1, NEG, fvals)
    v2, e2 = first_argmax(fvals2)
    ex = jnp.exp(v2 - v1)
    w1 = gprob * (1.0 / (1.0 + ex))
    w2 = gprob * (ex / (1.0 + ex))

    sel1 = lanef == e1
    sel2 = lanef == e2
    onehot = jnp.where(sel1 | sel2, 1.0, 0.0)
    ri = lax.broadcasted_iota(I32, (tm, tm), 0)
    ci = lax.broadcasted_iota(I32, (tm, tm), 1)
    lower = jnp.where(ci < ri, 1.0, 0.0).astype(BF16)
    before = jnp.dot(lower, onehot.astype(BF16), preferred_element_type=F32) + carry_scr[...]
    r1 = jnp.sum(jnp.where(sel1, before, 0.0), axis=1, keepdims=True)
    r2 = jnp.sum(jnp.where(sel2, before, 0.0), axis=1, keepdims=True)
    carry_new = carry_scr[...] + jnp.sum(onehot, axis=0, keepdims=True)
    carry_scr[...] = carry_new
    cnt_ref[...] = carry_new.astype(I32)

    idx_ref[...] = jnp.zeros_like(idx_ref)
    for row, col in enumerate((e1, e2, r1, r2)):
        idx_ref[row:row + 1, :] = _col_to_row(col).astype(I32)
    wts_ref[...] = jnp.where(lane == 0, w1, jnp.where(lane == 1, w2, 0.0))


def _out_projection(ret2, fox2, w_out_bf, x2, mod3, npost, npre, rw, rb, s):
    t, d = x2.shape
    per_b = s // OUT_TM
    row = lambda i: (i, 0)
    const = lambda i: (0, 0)
    return pl.pallas_call(
        _outproj_kernel,
        grid=(t // OUT_TM,),
        in_specs=[pl.BlockSpec((OUT_TM, RET_WIDTH), row),
                  pl.BlockSpec((OUT_TM, FOX_WIDTH), row),
                  pl.BlockSpec((d, d), const),
                  pl.BlockSpec((OUT_TM, d), row),
                  pl.BlockSpec((1, N_MOD, d), lambda i: (i // per_b, 0, 0)),
                  pl.BlockSpec((1, d), const),
                  pl.BlockSpec((1, d), const),
                  pl.BlockSpec((d, LANES), const),
                  pl.BlockSpec((1, LANES), const)],
        out_specs=[pl.BlockSpec((OUT_TM, d), row),
                   pl.BlockSpec((OUT_TM, ROW_TILES, LANES), lambda i: (i, 0, 0)),
                   pl.BlockSpec((SUBLANES, OUT_TM), lambda i: (0, i)),
                   pl.BlockSpec((OUT_TM, LANES), row),
                   pl.BlockSpec((1, LANES), const)],
        out_shape=[jax.ShapeDtypeStruct((t, d), F32),
                   jax.ShapeDtypeStruct((t, ROW_TILES, LANES), U32),
                   jax.ShapeDtypeStruct((SUBLANES, t), I32),
                   jax.ShapeDtypeStruct((t, LANES), F32),
                   jax.ShapeDtypeStruct((1, LANES), I32)],
        scratch_shapes=[pltpu.VMEM((1, LANES), F32)],
        compiler_params=_cparams(("arbitrary",)),
        name="out_projection_router",
    )(ret2, fox2, w_out_bf, x2, mod3, npost.reshape(1, d), npre.reshape(1, d), rw, rb)


def _cast_kernel(w_ref, o_ref):
    o_ref[...] = w_ref[...].astype(BF16)


def _cast_bf16(w, layer, tr=256):
    _, r, c = w.shape
    return pl.pallas_call(
        _cast_kernel,
        grid=(r // tr,),
        in_specs=[pl.BlockSpec((None, tr, c), lambda i: (layer, i, 0))],
        out_specs=pl.BlockSpec((tr, c), lambda i: (i, 0)),
        out_shape=jax.ShapeDtypeStruct((r, c), BF16),
        compiler_params=_cparams(("arbitrary",)),
        name="cast_bf16",
    )(w)


PLAN_ROW_START = 0
PLAN_TILE_EXPERT = 1
PLAN_NUM_TILES = 2
PLAN_TILE_END = 3
PLAN_COUNT = 4


def _plan_kernel(idx_ref, cnt_ref, pos_ref, plan_ref):
    lane = lax.broadcasted_iota(I32, (1, LANES), 1)
    cnt = jnp.where(lane < N_EXPERTS, cnt_ref[...], 0).astype(F32)
    tiles = jnp.floor((cnt + float(EXP_TM - 1)) * (1.0 / EXP_TM))
    ri = lax.broadcasted_iota(I32, (LANES, LANES), 0)
    ci = lax.broadcasted_iota(I32, (LANES, LANES), 1)
    upper = jnp.where(ri <= ci, 1.0, 0.0).astype(BF16)
    tiles8 = jnp.broadcast_to(tiles, (SUBLANES, LANES)).astype(BF16)
    tile_end = jnp.dot(tiles8, upper, preferred_element_type=F32)[0:1, :]
    row_start = (tile_end - tiles) * float(EXP_TM)
    num_tiles = jnp.max(tile_end, axis=1, keepdims=True)
    end_col = _row_to_col(tile_end)
    owned = jnp.where((end_col <= ci.astype(F32)) & (ri < N_EXPERTS), 1.0, 0.0)
    tile_expert = jnp.sum(owned, axis=0, keepdims=True)
    last_expert = jnp.max(jnp.where(lane.astype(F32) < num_tiles, tile_expert, 0.0), axis=1, keepdims=True)
    tile_expert = jnp.minimum(tile_expert, last_expert)
    plan_ref[...] = jnp.zeros_like(plan_ref)
    plan_ref[PLAN_ROW_START:PLAN_ROW_START + 1, :] = row_start.astype(I32)
    plan_ref[PLAN_TILE_EXPERT:PLAN_TILE_EXPERT + 1, :] = tile_expert.astype(I32)
    plan_ref[PLAN_NUM_TILES:PLAN_NUM_TILES + 1, :] = jnp.broadcast_to(num_tiles, (1, LANES)).astype(I32)
    plan_ref[PLAN_TILE_END:PLAN_TILE_END + 1, :] = tile_end.astype(I32)
    plan_ref[PLAN_COUNT:PLAN_COUNT + 1, :] = cnt.astype(I32)

    experts = idx_ref[0:2, :]
    start_i = row_start.astype(I32)
    base = jnp.zeros(experts.shape, I32)
    for e in range(N_EXPERTS):
        base = jnp.where(experts == e, start_i[:, e:e + 1], base)
    pos_ref[...] = jnp.zeros_like(pos_ref)
    pos_ref[0:2, :] = idx_ref[2:4, :] + base


def _route_plan(idx, counts):
    t = idx.shape[1]
    return pl.pallas_call(
        _plan_kernel,
        out_shape=[jax.ShapeDtypeStruct((SUBLANES, t), I32),
                   jax.ShapeDtypeStruct((SUBLANES, LANES), I32)],
        compiler_params=pltpu.CompilerParams(vmem_limit_bytes=VMEM_LIMIT),
        name="route_plan",
    )(idx, counts)


def _expert_kernel(pos_ref, plan_ref, h2_hbm, w1_hbm, w3_hbm, w2_hbm, ys_ref,
                   inv, gbuf, gsem, w1s, w3s, w2s, wsem, w1b, w3b, w2b, *, layer):
    i = pl.program_id(0)
    tm = EXP_TM
    n_tok = h2_hbm.shape[0]
    nt = plan_ref[PLAN_NUM_TILES, 0]
    expert = plan_ref[PLAN_TILE_EXPERT, i]

    def weight_copies(e):
        return (pltpu.make_async_copy(w1_hbm.at[layer, e], w1s, wsem.at[0]),
                pltpu.make_async_copy(w3_hbm.at[layer, e], w3s, wsem.at[1]),
                pltpu.make_async_copy(w2_hbm.at[layer, e], w2s, wsem.at[2]))

    def gather_start(tile, slot):
        base = tile * tm

        def body(j, _):
            for u in range(DMA_UNROLL):
                r = j * DMA_UNROLL + u
                tok = jnp.minimum(inv[base + r] >> 1, n_tok - 1)
                pltpu.make_async_copy(h2_hbm.at[tok], gbuf.at[slot, r], gsem.at[slot]).start()
            return 0
        lax.fori_loop(0, tm // DMA_UNROLL, body, 0)

    def gather_wait(slot):
        pltpu.make_async_copy(h2_hbm.at[pl.ds(0, tm)], gbuf.at[slot], gsem.at[slot]).wait()

    @pl.when(i == 0)
    def _():
        for cp in weight_copies(plan_ref[PLAN_TILE_EXPERT, 0]):
            cp.start()

        def fill(p, _):
            inv[p] = 2 * n_tok
            return 0
        for e in range(N_EXPERTS):
            lax.fori_loop(plan_ref[PLAN_ROW_START, e] + plan_ref[PLAN_COUNT, e],
                          plan_ref[PLAN_TILE_END, e] * tm, fill, 0)

        def scatter(j, _):
            for u in range(DMA_UNROLL):
                tok = j * DMA_UNROLL + u
                inv[pos_ref[0, tok]] = 2 * tok
                inv[pos_ref[1, tok]] = 2 * tok + 1
            return 0
        lax.fori_loop(0, n_tok // DMA_UNROLL, scatter, 0)
        gather_start(0, 0)

    valid = i < nt
    first = (i == 0) | (expert != plan_ref[PLAN_TILE_EXPERT, jnp.maximum(i - 1, 0)])

    @pl.when(valid & first)
    def _():
        for cp in weight_copies(expert):
            cp.wait()
        w1b[...] = w1s[...].astype(BF16)
        w3b[...] = w3s[...].astype(BF16)
        w2b[...] = w2s[...].astype(BF16)
        nxt = plan_ref[PLAN_TILE_END, expert]

        @pl.when(nxt < nt)
        def _():
            for cp in weight_copies(plan_ref[PLAN_TILE_EXPERT, nxt]):
                cp.start()

    @pl.when(valid)
    def _():
        slot = i % 2

        @pl.when(i + 1 < nt)
        def _():
            gather_start(i + 1, 1 - slot)

        gather_wait(slot)
        lo, hi = _unpack_halves(gbuf[slot].reshape(tm, HALF))
        lo = lo.astype(BF16)
        hi = hi.astype(BF16)
        a = (jnp.dot(lo, w1b[:HALF, :], preferred_element_type=F32)
             + jnp.dot(hi, w1b[HALF:, :], preferred_element_type=F32))
        g = (jnp.dot(lo, w3b[:HALF, :], preferred_element_type=F32)
             + jnp.dot(hi, w3b[HALF:, :], preferred_element_type=F32))
        act = (a * _sigmoid(a) * g).astype(BF16)
        y = jnp.dot(act, w2b[...], preferred_element_type=F32)
        ys_ref[...] = _pack_halves(y).reshape(tm, ROW_TILES, LANES)

    @pl.when(jnp.logical_not(valid))
    def _():
        ys_ref[...] = jnp.zeros_like(ys_ref)


def _expert_gemm(h2t, pos, plan, w1, w3, w2, layer):
    t = h2t.shape[0]
    n_tiles = (t * 2) // EXP_TM + N_EXPERTS
    n_rows = n_tiles * EXP_TM
    d, de = w1.shape[2], w1.shape[3]
    any_spec = pl.BlockSpec(memory_space=pl.ANY)
    return pl.pallas_call(
        functools.partial(_expert_kernel, layer=layer),
        grid_spec=pltpu.PrefetchScalarGridSpec(
            num_scalar_prefetch=2,
            grid=(n_tiles,),
            in_specs=[any_spec, any_spec, any_spec, any_spec],
            out_specs=pl.BlockSpec((EXP_TM, ROW_TILES, LANES), lambda i, pos, plan: (i, 0, 0)),
            scratch_shapes=[pltpu.SMEM((n_rows,), I32),
                            pltpu.VMEM((2, EXP_TM, ROW_TILES, LANES), U32),
                            pltpu.SemaphoreType.DMA((2,)),
                            pltpu.VMEM((d, de), F32),
                            pltpu.VMEM((d, de), F32),
                            pltpu.VMEM((de, d), F32),
                            pltpu.SemaphoreType.DMA((3,)),
                            pltpu.VMEM((d, de), BF16),
                            pltpu.VMEM((d, de), BF16),
                            pltpu.VMEM((de, d), BF16)]),
        out_shape=jax.ShapeDtypeStruct((n_rows, ROW_TILES, LANES), U32),
        compiler_params=_cparams(("arbitrary",)),
        name="moe_expert_gemm",
    )(pos, plan, h2t, w1, w3, w2)


def _combine_kernel(pos_ref, ys_hbm, x1_ref, wts_ref, mod_ref, npost_ref, o_ref, buf_a, buf_b, sems):
    i = pl.program_id(0)
    n = pl.num_programs(0)
    tm = CMB_TM

    def start_tile(tile, slot):
        def body(j, _):
            for u in range(DMA_UNROLL):
                r = j * DMA_UNROLL + u
                tok = tile * tm + r
                pltpu.make_async_copy(ys_hbm.at[pos_ref[0, tok]], buf_a.at[slot, r], sems.at[0, slot]).start()
                pltpu.make_async_copy(ys_hbm.at[pos_ref[1, tok]], buf_b.at[slot, r], sems.at[1, slot]).start()
            return 0
        lax.fori_loop(0, tm // DMA_UNROLL, body, 0)

    def wait_tile(slot):
        pltpu.make_async_copy(ys_hbm.at[pl.ds(0, tm)], buf_a.at[slot], sems.at[0, slot]).wait()
        pltpu.make_async_copy(ys_hbm.at[pl.ds(0, tm)], buf_b.at[slot], sems.at[1, slot]).wait()

    slot = i % 2

    @pl.when(i == 0)
    def _():
        start_tile(0, 0)

    @pl.when(i + 1 < n)
    def _():
        start_tile(i + 1, 1 - slot)

    wait_tile(slot)

    wa = wts_ref[:, 0:1]
    wb = wts_ref[:, 1:2]
    alo, ahi = _unpack_halves(buf_a[slot].reshape(tm, HALF))
    blo, bhi = _unpack_halves(buf_b[slot].reshape(tm, HALF))
    ylo = wa * alo + wb * blo
    yhi = wa * ahi + wb * bhi
    ms = (jnp.sum(ylo * ylo, axis=-1, keepdims=True) + jnp.sum(yhi * yhi, axis=-1, keepdims=True)) / float(D_MODEL)
    inv = lax.rsqrt(ms + EPS)
    gate2 = mod_ref[0, 5:6, :]
    npost = npost_ref[...]
    o_ref[:, :HALF] = x1_ref[:, :HALF] + gate2[:, :HALF] * (ylo * inv * npost[:, :HALF])
    o_ref[:, HALF:] = x1_ref[:, HALF:] + gate2[:, HALF:] * (yhi * inv * npost[:, HALF:])


def _combine(ys, pos, x1, wts, mod3, npost, s):
    t, d = x1.shape
    per_b = s // CMB_TM
    row = lambda i, pos: (i, 0)
    return pl.pallas_call(
        _combine_kernel,
        grid_spec=pltpu.PrefetchScalarGridSpec(
            num_scalar_prefetch=1,
            grid=(t // CMB_TM,),
            in_specs=[pl.BlockSpec(memory_space=pl.ANY),
                      pl.BlockSpec((CMB_TM, d), row),
                      pl.BlockSpec((CMB_TM, LANES), row),
                      pl.BlockSpec((1, N_MOD, d), lambda i, pos: (i // per_b, 0, 0)),
                      pl.BlockSpec((1, d), lambda i, pos: (0, 0))],
            out_specs=pl.BlockSpec((CMB_TM, d), row),
            scratch_shapes=[pltpu.VMEM((2, CMB_TM, ROW_TILES, LANES), U32),
                            pltpu.VMEM((2, CMB_TM, ROW_TILES, LANES), U32),
                            pltpu.SemaphoreType.DMA((2, 2))]),
        out_shape=jax.ShapeDtypeStruct((t, d), F32),
        compiler_params=_cparams(("arbitrary",)),
        name="moe_combine",
    )(pos, ys, x1, wts, mod3, npost.reshape(1, d))


def _rope_tables(s):
    inv_freq = ROPE_BASE ** (-jnp.arange(0, HEAD_DIM, 2, dtype=F32) / HEAD_DIM)
    angle = jnp.arange(s, dtype=F32)[:, None] * inv_freq[None, :]
    cos = jnp.cos(angle)
    sin = jnp.sin(angle)
    return jnp.concatenate([cos, cos], axis=-1), jnp.concatenate([-sin, sin], axis=-1)


def _router_weights(router_coarse_w, router_fine_w, router_coarse_b, router_fine_b):
    d = router_fine_w.shape[0]
    w = jnp.concatenate([router_fine_w, router_coarse_w], axis=1)
    n = w.shape[1]
    w_hi = w.astype(BF16)
    w_lo = (w - w_hi.astype(F32)).astype(BF16)
    pad = jnp.zeros((d, LANES // 2 - n), BF16)
    rw = jnp.concatenate([w_hi, pad, w_lo, pad], axis=1)
    rb = jnp.concatenate([router_fine_b, router_coarse_b, jnp.zeros((LANES - n,), F32)]).reshape(1, LANES)
    return rw, rb


def _layer(x2, c, nb, s, cosf, sinf, lg_tab, layer, ada_w, ada_b, norm_pre_mix, norm_post_mix, w_in, fox_f_bias,
           ret_gn_w, fox_qn_w, fox_kn_w, fox_on_w, w_out, norm_pre_ffn, norm_post_ffn,
           router_coarse_w, router_coarse_b, router_fine_w, router_fine_b, w1, w3, w2):
    t, d = x2.shape
    mod3 = _modulation(c, ada_w, ada_b, layer).reshape(nb, N_MOD, d)

    proj, lf = _in_projection(x2, mod3, norm_pre_mix[layer], w_in, fox_f_bias[layer], fox_qn_w[layer],
                              fox_kn_w[layer], cosf, sinf, nb, s, layer)
    cum = _cum_forget(lf)
    proj3 = proj.reshape(nb, s, MAIN_COLS)
    ret = _retention(proj3, ret_gn_w[layer], lg_tab)
    fox = _fox_attention(proj3, cum, fox_on_w[layer])

    rw, rb = _router_weights(router_coarse_w[layer], router_fine_w[layer], router_coarse_b[layer],
                             router_fine_b[layer])
    x1, h2t, idx, wts, counts = _out_projection(
        ret.reshape(t, RET_WIDTH), fox.reshape(t, FOX_WIDTH), _cast_bf16(w_out, layer), x2, mod3,
        norm_post_mix[layer], norm_pre_ffn[layer], rw, rb, s)

    pos, plan = _route_plan(idx, counts)
    ys = _expert_gemm(h2t, pos, plan, w1, w3, w2, layer)
    return _combine(ys, pos, x1, wts, mod3, norm_post_ffn[layer], s)


def kernel(x, c, ada_w, ada_b, norm_pre_mix, norm_post_mix, w_in, fox_f_bias, ret_gn_w, fox_qn_w, fox_kn_w,
           fox_on_w, w_out, norm_pre_ffn, norm_post_ffn, router_coarse_w, router_coarse_b, router_fine_w,
           router_fine_b, w1, w3, w2):
    nb, s, d = x.shape
    depth = ada_w.shape[0]
    cosf, sinf = _rope_tables(s)
    log_gamma = jnp.log1p(-jnp.exp2(-5.0 - jnp.arange(RET_HEADS, dtype=F32)))
    lg_tab = jnp.broadcast_to(log_gamma[:, None, None], (RET_HEADS, 1, LANES))
    x2 = x.reshape(nb * s, d)
    for layer in range(depth):
        x2 = _layer(x2, c, nb, s, cosf, sinf, lg_tab, layer, ada_w, ada_b, norm_pre_mix, norm_post_mix,
                    w_in, fox_f_bias, ret_gn_w, fox_qn_w, fox_kn_w, fox_on_w, w_out, norm_pre_ffn,
                    norm_post_ffn, router_coarse_w, router_coarse_b, router_fine_w, router_fine_b, w1, w3, w2)
    return x2.reshape(nb, s, d)
```

```python
import functools

import jax
import jax.numpy as jnp
from jax import lax
from jax.experimental import pallas as pl
from jax.experimental.pallas import tpu as pltpu

F32 = jnp.float32
BF16 = jnp.bfloat16
I32 = jnp.int32

D_MODEL = 2048
HEAD_DIM = 128
RET_WIDTH = D_MODEL // 2
FOX_WIDTH = D_MODEL - RET_WIDTH
RET_HEADS = RET_WIDTH // HEAD_DIM
FOX_HEADS = FOX_WIDTH // HEAD_DIM
ROPE_BASE = 10000.0
N_GROUPS = 4
EXPERTS_PER_GROUP = 8
N_EXPERTS = N_GROUPS * EXPERTS_PER_GROUP
D_EXPERT = D_MODEL // 4
N_MOD = 6
EPS = 1e-6
MAIN_COLS = 4 * RET_WIDTH + 3 * FOX_WIDTH

LANES = 128
SUBLANES = 8
ROW_TILES = D_MODEL // LANES
NEG = -1e30
LOG2E = 1.4426950408889634

VMEM_LIMIT = 56 * 1024 * 1024

MOD_TN = 1024
IN_TM, IN_TN = 1024, 512
RET_CHUNK = 256
RET_HPS = 2
FOX_T = 512
FOX_HPS = 2
OUT_TM = 256
EXP_TM = 256
CMB_TM = 256
DMA_UNROLL = 8


def _cparams(sem):
    return pltpu.CompilerParams(dimension_semantics=sem, vmem_limit_bytes=VMEM_LIMIT)


def _nt_dot(a, b):
    return lax.dot_general(a, b, (((1,), (1,)), ((), ())), preferred_element_type=F32)


def _tn_dot(a, b):
    return lax.dot_general(a, b, (((0,), (0,)), ((), ())), preferred_element_type=F32)


def _sigmoid(x):
    return 1.0 / (1.0 + jnp.exp(-x))


def _col_to_row(col):
    n = col.shape[0]
    ri = lax.broadcasted_iota(I32, (n, n), 0)
    ci = lax.broadcasted_iota(I32, (n, n), 1)
    return jnp.sum(jnp.where(ri == ci, col, 0.0), axis=0, keepdims=True)


def _row_to_col(row):
    n = row.shape[1]
    ri = lax.broadcasted_iota(I32, (n, n), 0)
    ci = lax.broadcasted_iota(I32, (n, n), 1)
    return jnp.sum(jnp.where(ri == ci, row, 0.0), axis=1, keepdims=True)


def _mod_kernel(cb_ref, w_ref, b_ref, o_ref):
    nb = cb_ref.shape[0]
    for b in range(nb):
        cv = cb_ref[b]
        ca = cv * _sigmoid(cv)
        for j in range(MOD_TN // LANES):
            sl = slice(j * LANES, (j + 1) * LANES)
            col = jnp.sum(w_ref[:, sl] * ca, axis=0, keepdims=True)
            o_ref[b:b + 1, sl] = col + b_ref[:, sl]


def _modulation(c, ada_w, ada_b, layer):
    nb, d = c.shape
    n = ada_w.shape[2]
    cb = jnp.broadcast_to(c[:, :, None], (nb, d, LANES))
    return pl.pallas_call(
        _mod_kernel,
        grid=(n // MOD_TN,),
        in_specs=[pl.BlockSpec((nb, d, LANES), lambda j: (0, 0, 0)),
                  pl.BlockSpec((None, d, MOD_TN), lambda j: (layer, 0, j)),
                  pl.BlockSpec((None, 1, MOD_TN), lambda j: (layer, 0, j))],
        out_specs=pl.BlockSpec((nb, MOD_TN), lambda j: (0, j)),
        out_shape=jax.ShapeDtypeStruct((nb, n), F32),
        compiler_params=_cparams(("arbitrary",)),
        name="adaln_mod",
    )(cb, ada_w, ada_b.reshape(ada_b.shape[0], 1, n))


def _rms_rows(x):
    return x * lax.rsqrt(jnp.mean(x * x, axis=-1, keepdims=True) + EPS)


def _inproj_kernel(x_ref, mod_ref, nw_ref, w_ref, wf_ref, fb_ref, cos_ref, sin_ref, qn_ref, kn_ref,
                   o_ref, lf_ref, h_scr):
    j = pl.program_id(1)
    tiles_per_sec = RET_WIDTH // IN_TN
    heads_per_tile = IN_TN // HEAD_DIM

    @pl.when(j == 0)
    def _():
        shift = mod_ref[0, 0:1, :]
        scale = 1.0 + mod_ref[0, 1:2, :]
        nw = nw_ref[...]
        rows = 128
        for r in range(IN_TM // rows):
            rs = slice(r * rows, (r + 1) * rows)
            h = _rms_rows(x_ref[rs, :]) * nw * scale + shift
            h_scr[rs, :] = h.astype(BF16)
        z = _nt_dot(wf_ref[...], h_scr[...]) + fb_ref[...]
        lf_ref[0] = jnp.minimum(z, 0.0) - jnp.log1p(jnp.exp(-jnp.abs(z)))

    acc = _nt_dot(h_scr[...], w_ref[...].astype(BF16))
    sec = j // tiles_per_sec

    def heads():
        for hh in range(heads_per_tile):
            sl = slice(hh * HEAD_DIM, (hh + 1) * HEAD_DIM)
            yield sl, acc[:, sl]

    def rope(v):
        return v * cos_ref[...] + pltpu.roll(v, HEAD_DIM // 2, 1) * sin_ref[...]

    @pl.when(sec == 0)
    def _():
        for sl, v in heads():
            o_ref[:, sl] = rope(v).astype(BF16)

    @pl.when(sec == 1)
    def _():
        for sl, v in heads():
            o_ref[:, sl] = (rope(v) * (HEAD_DIM ** -0.5)).astype(BF16)

    @pl.when((sec == 2) | (sec == 6))
    def _():
        o_ref[...] = acc.astype(BF16)

    @pl.when(sec == 3)
    def _():
        o_ref[...] = (acc * _sigmoid(acc)).astype(BF16)

    @pl.when(sec == 4)
    def _():
        for sl, v in heads():
            o_ref[:, sl] = (_rms_rows(v) * qn_ref[...] * (LOG2E * HEAD_DIM ** -0.5)).astype(BF16)

    @pl.when(sec == 5)
    def _():
        for sl, v in heads():
            o_ref[:, sl] = (_rms_rows(v) * kn_ref[...]).astype(BF16)


def _in_projection(x2, mod3, norm_w, w_in, fox_f_bias, fox_qn_w, fox_kn_w, cosf, sinf, nb, s, layer):
    t, d = x2.shape
    w_t = jnp.swapaxes(w_in, 1, 2)
    wf_t = w_t[layer, MAIN_COLS:, :].astype(BF16)
    per_b = s // IN_TM
    grid = (t // IN_TM, MAIN_COLS // IN_TN)
    return pl.pallas_call(
        _inproj_kernel,
        grid=grid,
        in_specs=[
            pl.BlockSpec((IN_TM, d), lambda i, j: (i, 0)),
            pl.BlockSpec((1, N_MOD, d), lambda i, j: (i // per_b, 0, 0)),
            pl.BlockSpec((1, d), lambda i, j: (0, 0)),
            pl.BlockSpec((None, IN_TN, d), lambda i, j: (layer, j, 0)),
            pl.BlockSpec((FOX_HEADS, d), lambda i, j: (0, 0)),
            pl.BlockSpec((FOX_HEADS, 1), lambda i, j: (0, 0)),
            pl.BlockSpec((IN_TM, HEAD_DIM), lambda i, j: (i % per_b, 0)),
            pl.BlockSpec((IN_TM, HEAD_DIM), lambda i, j: (i % per_b, 0)),
            pl.BlockSpec((1, HEAD_DIM), lambda i, j: (0, 0)),
            pl.BlockSpec((1, HEAD_DIM), lambda i, j: (0, 0)),
        ],
        out_specs=[
            pl.BlockSpec((IN_TM, IN_TN), lambda i, j: (i, j)),
            pl.BlockSpec((1, FOX_HEADS, IN_TM), lambda i, j: (i // per_b, 0, i % per_b)),
        ],
        out_shape=[jax.ShapeDtypeStruct((t, MAIN_COLS), BF16),
                   jax.ShapeDtypeStruct((nb, FOX_HEADS, s), F32)],
        scratch_shapes=[pltpu.VMEM((IN_TM, d), BF16)],
        compiler_params=_cparams(("arbitrary", "arbitrary")),
        name="in_projection",
    )(x2, mod3, norm_w.reshape(1, d), w_t, wf_t, fox_f_bias.reshape(FOX_HEADS, 1), cosf, sinf,
      fox_qn_w.reshape(1, HEAD_DIM), fox_kn_w.reshape(1, HEAD_DIM))


def _split3(x):
    hi = x.astype(BF16)
    r1 = x - hi.astype(F32)
    mid = r1.astype(BF16)
    lo = (r1 - mid.astype(F32)).astype(BF16)
    return hi, mid, lo


def _cumsum_kernel(x_ref, o_ref, *, rows_per_seq):
    x = x_ref[...]
    r = x.shape[0]
    ri = lax.broadcasted_iota(I32, (LANES, LANES), 0)
    ci = lax.broadcasted_iota(I32, (LANES, LANES), 1)
    upper = jnp.where(ri <= ci, 1.0, 0.0).astype(BF16)
    rowcum = sum(jnp.dot(p, upper, preferred_element_type=F32) for p in _split3(x))
    tot = jnp.broadcast_to(rowcum[:, LANES - 1:LANES], (r, LANES))
    gi = lax.broadcasted_iota(I32, (r, r), 0)
    gj = lax.broadcasted_iota(I32, (r, r), 1)
    same_seq = (gi // rows_per_seq) == (gj // rows_per_seq)
    lower = jnp.where(same_seq & (gj < gi), 1.0, 0.0).astype(BF16)
    prefix = sum(jnp.dot(lower, p, preferred_element_type=F32) for p in _split3(tot))
    o_ref[...] = rowcum + prefix


def _cum_forget(lf):
    nb, nh, s = lf.shape
    rows = nb * nh * s // LANES
    out = pl.pallas_call(
        functools.partial(_cumsum_kernel, rows_per_seq=s // LANES),
        out_shape=jax.ShapeDtypeStruct((rows, LANES), F32),
        compiler_params=pltpu.CompilerParams(vmem_limit_bytes=VMEM_LIMIT),
        name="forget_cumsum",
    )(lf.reshape(rows, LANES))
    return out.reshape(nb * nh, 1, s)


def _ret_kernel(lg_ref, q_ref, k_ref, v_ref, g_ref, gnw_ref, o_ref):
    c = RET_CHUNK
    s = q_ref.shape[1]
    ri = lax.broadcasted_iota(I32, (c, c), 0)
    ci = lax.broadcasted_iota(I32, (c, c), 1)
    diff = (ri - ci).astype(F32)
    pos = lax.broadcasted_iota(I32, (c, 1), 0).astype(F32)
    consts = []
    for hh in range(RET_HPS):
        lg = lg_ref[hh][:, 0:1]
        decay = jnp.where(diff >= 0.0, jnp.exp(lg * jnp.maximum(diff, 0.0)), 0.0)
        zeta = jnp.exp(lg * (float(c - 1) - pos))
        xi = jnp.exp(lg * (pos + 1.0))
        chunk_decay = jnp.exp(lg * float(c))
        consts.append((decay, zeta, xi, chunk_decay))

    def body(n, states):
        r0 = pl.multiple_of(n * c, c)
        new_states = []
        for hh in range(RET_HPS):
            decay, zeta, xi, chunk_decay = consts[hh]
            sl = slice(hh * HEAD_DIM, (hh + 1) * HEAD_DIM)
            q = q_ref[0, pl.ds(r0, c), sl]
            k = k_ref[0, pl.ds(r0, c), sl]
            v = v_ref[0, pl.ds(r0, c), sl]
            scores = _nt_dot(q, k) * decay
            intra = jnp.dot(scores.astype(BF16), v, preferred_element_type=F32)
            cross = jnp.dot(q, states[hh].astype(BF16), preferred_element_type=F32) * xi
            kz = (k.astype(F32) * zeta).astype(BF16)
            kv = _tn_dot(kz, v)
            o = intra + cross
            mu = jnp.mean(o, axis=-1, keepdims=True)
            oc = o - mu
            var = jnp.mean(oc * oc, axis=-1, keepdims=True)
            y = oc * lax.rsqrt(var + EPS) * gnw_ref[:, sl] * g_ref[0, pl.ds(r0, c), sl].astype(F32)
            o_ref[0, pl.ds(r0, c), sl] = y.astype(BF16)
            new_states.append(states[hh] * chunk_decay + kv)
        return tuple(new_states)

    init = tuple(jnp.zeros((HEAD_DIM, HEAD_DIM), F32) for _ in range(RET_HPS))
    lax.fori_loop(0, s // c, body, init)


def _retention(proj3, ret_gn_w, lg_tab):
    nb, s, _ = proj3.shape
    w = RET_HPS * HEAD_DIM
    gpg = RET_WIDTH // w

    def col(sec):
        return pl.BlockSpec((1, s, w), lambda b, h: (b, 0, sec * gpg + h))

    return pl.pallas_call(
        _ret_kernel,
        grid=(nb, gpg),
        in_specs=[pl.BlockSpec((RET_HPS, 1, LANES), lambda b, h: (h, 0, 0)),
                  col(0), col(1), col(2), col(3),
                  pl.BlockSpec((1, w), lambda b, h: (0, h))],
        out_specs=pl.BlockSpec((1, s, w), lambda b, h: (b, 0, h)),
        out_shape=jax.ShapeDtypeStruct((nb, s, RET_WIDTH), BF16),
        compiler_params=_cparams(("arbitrary", "arbitrary")),
        name="retention",
    )(lg_tab, proj3, proj3, proj3, proj3, ret_gn_w.reshape(1, RET_WIDTH))


def _fox_kernel(q_ref, k_ref, v_ref, cum_ref, onw_ref, o_ref):
    t = FOX_T
    qi = pl.program_id(2)
    ri = lax.broadcasted_iota(I32, (t, t), 0)
    ci = lax.broadcasted_iota(I32, (t, t), 1)

    def step(ki, carry, masked):
        k0 = pl.multiple_of(ki * t, t)
        out = []
        for hh in range(FOX_HPS):
            m, l, acc = carry[hh]
            sl = slice(hh * HEAD_DIM, (hh + 1) * HEAD_DIM)
            k = k_ref[0, pl.ds(k0, t), sl]
            v = v_ref[0, pl.ds(k0, t), sl]
            sc = _nt_dot(q_ref[0, :, sl], k) - cum_ref[hh, :, pl.ds(k0, t)] * LOG2E
            if masked:
                sc = jnp.where(ci <= ri, sc, NEG)
            m_new = jnp.maximum(m, jnp.max(sc, axis=1, keepdims=True))
            p = jnp.exp2(sc - m_new)
            alpha = jnp.exp2(m - m_new)
            l = alpha * l + jnp.sum(p, axis=1, keepdims=True)
            acc = alpha * acc + jnp.dot(p.astype(BF16), v, preferred_element_type=F32)
            out.append((m_new, l, acc))
        return tuple(out)

    init = tuple((jnp.full((t, 1), NEG, F32), jnp.zeros((t, 1), F32), jnp.zeros((t, HEAD_DIM), F32))
                 for _ in range(FOX_HPS))
    carry = lax.fori_loop(0, qi, lambda ki, cr: step(ki, cr, False), init)
    carry = step(qi, carry, True)
    for hh in range(FOX_HPS):
        _, l, acc = carry[hh]
        sl = slice(hh * HEAD_DIM, (hh + 1) * HEAD_DIM)
        o = acc / l
        o_ref[0, :, sl] = (_rms_rows(o) * onw_ref[:, sl]).astype(BF16)


def _fox_attention(proj3, cum, fox_on_w):
    nb, s, _ = proj3.shape
    w = FOX_HPS * HEAD_DIM
    base = 4 * RET_WIDTH // w
    gpg = FOX_WIDTH // w
    return pl.pallas_call(
        _fox_kernel,
        grid=(nb, gpg, s // FOX_T),
        in_specs=[pl.BlockSpec((1, FOX_T, w), lambda b, h, i: (b, i, base + h)),
                  pl.BlockSpec((1, s, w), lambda b, h, i: (b, 0, base + gpg + h)),
                  pl.BlockSpec((1, s, w), lambda b, h, i: (b, 0, base + 2 * gpg + h)),
                  pl.BlockSpec((FOX_HPS, 1, s), lambda b, h, i: (b * gpg + h, 0, 0)),
                  pl.BlockSpec((1, w), lambda b, h, i: (0, h))],
        out_specs=pl.BlockSpec((1, FOX_T, w), lambda b, h, i: (b, i, h)),
        out_shape=jax.ShapeDtypeStruct((nb, s, FOX_WIDTH), BF16),
        compiler_params=_cparams(("arbitrary", "arbitrary", "arbitrary")),
        name="fox_attention",
    )(proj3, proj3, proj3, cum, fox_on_w.reshape(1, FOX_WIDTH))


def _outproj_kernel(ret_ref, fox_ref, wo_ref, x_ref, mod_ref, npost_ref, npre_ref, rw_ref, rb_ref,
                    x1_ref, h2_ref, idx_ref, wts_ref, cnt_ref, carry_scr):
    i = pl.program_id(0)
    tm = OUT_TM

    @pl.when(i == 0)
    def _():
        carry_scr[...] = jnp.zeros_like(carry_scr)

    y = (jnp.dot(ret_ref[...], wo_ref[:RET_WIDTH, :], preferred_element_type=F32)
         + jnp.dot(fox_ref[...], wo_ref[RET_WIDTH:, :], preferred_element_type=F32))
    gate1 = mod_ref[0, 2:3, :]
    shift2 = mod_ref[0, 3:4, :]
    scale2 = 1.0 + mod_ref[0, 4:5, :]
    x1 = x_ref[...] + gate1 * (_rms_rows(y) * npost_ref[...])
    x1_ref[...] = x1
    h2 = _rms_rows(x1) * npre_ref[...] * scale2 + shift2
    h2_ref[...] = h2.astype(BF16).reshape(tm, ROW_TILES, LANES)

    hi = h2.astype(BF16)
    lo = (h2 - hi.astype(F32)).astype(BF16)
    rw = rw_ref[...]
    lg2 = jnp.dot(hi, rw, preferred_element_type=F32) + jnp.dot(lo, rw, preferred_element_type=F32)
    logits = lg2 + pltpu.roll(lg2, LANES // 2, 1) + rb_ref[...]

    lane = lax.broadcasted_iota(I32, (tm, LANES), 1)
    lanef = lane.astype(F32)
    big = float(LANES)

    def first_argmax(vals):
        vmax = jnp.max(vals, axis=1, keepdims=True)
        idx = jnp.min(jnp.where(vals == vmax, lanef, big), axis=1, keepdims=True)
        return vmax, idx

    is_coarse = (lane >= N_EXPERTS) & (lane < N_EXPERTS + N_GROUPS)
    cvals = jnp.where(is_coarse, logits, NEG)
    cmax, cidx = first_argmax(cvals)
    gidx = cidx - float(N_EXPERTS)
    gprob = 1.0 / jnp.sum(jnp.where(is_coarse, jnp.exp(cvals - cmax), 0.0), axis=1, keepdims=True)
    glo = gidx * float(EXPERTS_PER_GROUP)
    in_group = (lanef >= glo) & (lanef < glo + float(EXPERTS_PER_GROUP))
    fvals = jnp.where(in_group, logits, NEG)
    v1, e1 = first_argmax(fvals)
    fvals2 = jnp.where(lanef == e1, NEG, fvals)
    v2, e2 = first_argmax(fvals2)
    ex = jnp.exp(v2 - v1)
    w1 = gprob * (1.0 / (1.0 + ex))
    w2 = gprob * (ex / (1.0 + ex))

    sel1 = lanef == e1
    sel2 = lanef == e2
    onehot = jnp.where(sel1 | sel2, 1.0, 0.0)
    ri = lax.broadcasted_iota(I32, (tm, tm), 0)
    ci = lax.broadcasted_iota(I32, (tm, tm), 1)
    lower = jnp.where(ci < ri, 1.0, 0.0).astype(BF16)
    before = jnp.dot(lower, onehot.astype(BF16), preferred_element_type=F32) + carry_scr[...]
    r1 = jnp.sum(jnp.where(sel1, before, 0.0), axis=1, keepdims=True)
    r2 = jnp.sum(jnp.where(sel2, before, 0.0), axis=1, keepdims=True)
    carry_new = carry_scr[...] + jnp.sum(onehot, axis=0, keepdims=True)
    carry_scr[...] = carry_new
    cnt_ref[...] = carry_new.astype(I32)

    idx_ref[...] = jnp.zeros_like(idx_ref)
    for row, col in enumerate((e1, e2, r1, r2)):
        idx_ref[row:row + 1, :] = _col_to_row(col).astype(I32)
    wts_ref[...] = jnp.where(lane == 0, w1, jnp.where(lane == 1, w2, 0.0))


def _out_projection(ret2, fox2, w_out_bf, x2, mod3, npost, npre, rw, rb, s):
    t, d = x2.shape
    per_b = s // OUT_TM
    row = lambda i: (i, 0)
    const = lambda i: (0, 0)
    return pl.pallas_call(
        _outproj_kernel,
        grid=(t // OUT_TM,),
        in_specs=[pl.BlockSpec((OUT_TM, RET_WIDTH), row),
                  pl.BlockSpec((OUT_TM, FOX_WIDTH), row),
                  pl.BlockSpec((d, d), const),
                  pl.BlockSpec((OUT_TM, d), row),
                  pl.BlockSpec((1, N_MOD, d), lambda i: (i // per_b, 0, 0)),
                  pl.BlockSpec((1, d), const),
                  pl.BlockSpec((1, d), const),
                  pl.BlockSpec((d, LANES), const),
                  pl.BlockSpec((1, LANES), const)],
        out_specs=[pl.BlockSpec((OUT_TM, d), row),
                   pl.BlockSpec((OUT_TM, ROW_TILES, LANES), lambda i: (i, 0, 0)),
                   pl.BlockSpec((SUBLANES, OUT_TM), lambda i: (0, i)),
                   pl.BlockSpec((OUT_TM, LANES), row),
                   pl.BlockSpec((1, LANES), const)],
        out_shape=[jax.ShapeDtypeStruct((t, d), F32),
                   jax.ShapeDtypeStruct((t, ROW_TILES, LANES), BF16),
                   jax.ShapeDtypeStruct((SUBLANES, t), I32),
                   jax.ShapeDtypeStruct((t, LANES), F32),
                   jax.ShapeDtypeStruct((1, LANES), I32)],
        scratch_shapes=[pltpu.VMEM((1, LANES), F32)],
        compiler_params=_cparams(("arbitrary",)),
        name="out_projection_router",
    )(ret2, fox2, w_out_bf, x2, mod3, npost.reshape(1, d), npre.reshape(1, d), rw, rb)


def _cast_kernel(w_ref, o_ref):
    o_ref[...] = w_ref[...].astype(BF16)


def _cast_bf16(w, layer, tr=256):
    _, r, c = w.shape
    return pl.pallas_call(
        _cast_kernel,
        grid=(r // tr,),
        in_specs=[pl.BlockSpec((None, tr, c), lambda i: (layer, i, 0))],
        out_specs=pl.BlockSpec((tr, c), lambda i: (i, 0)),
        out_shape=jax.ShapeDtypeStruct((r, c), BF16),
        compiler_params=_cparams(("arbitrary",)),
        name="cast_bf16",
    )(w)


PLAN_ROW_START = 0
PLAN_TILE_EXPERT = 1
PLAN_NUM_TILES = 2
PLAN_TILE_END = 3
PLAN_COUNT = 4


def _plan_kernel(idx_ref, cnt_ref, pos_ref, plan_ref):
    lane = lax.broadcasted_iota(I32, (1, LANES), 1)
    cnt = jnp.where(lane < N_EXPERTS, cnt_ref[...], 0).astype(F32)
    tiles = jnp.floor((cnt + float(EXP_TM - 1)) * (1.0 / EXP_TM))
    ri = lax.broadcasted_iota(I32, (LANES, LANES), 0)
    ci = lax.broadcasted_iota(I32, (LANES, LANES), 1)
    upper = jnp.where(ri <= ci, 1.0, 0.0).astype(BF16)
    tiles8 = jnp.broadcast_to(tiles, (SUBLANES, LANES)).astype(BF16)
    tile_end = jnp.dot(tiles8, upper, preferred_element_type=F32)[0:1, :]
    row_start = (tile_end - tiles) * float(EXP_TM)
    num_tiles = jnp.max(tile_end, axis=1, keepdims=True)
    end_col = _row_to_col(tile_end)
    owned = jnp.where((end_col <= ci.astype(F32)) & (ri < N_EXPERTS), 1.0, 0.0)
    tile_expert = jnp.sum(owned, axis=0, keepdims=True)
    last_expert = jnp.max(jnp.where(lane.astype(F32) < num_tiles, tile_expert, 0.0), axis=1, keepdims=True)
    tile_expert = jnp.minimum(tile_expert, last_expert)
    plan_ref[...] = jnp.zeros_like(plan_ref)
    plan_ref[PLAN_ROW_START:PLAN_ROW_START + 1, :] = row_start.astype(I32)
    plan_ref[PLAN_TILE_EXPERT:PLAN_TILE_EXPERT + 1, :] = tile_expert.astype(I32)
    plan_ref[PLAN_NUM_TILES:PLAN_NUM_TILES + 1, :] = jnp.broadcast_to(num_tiles, (1, LANES)).astype(I32)
    plan_ref[PLAN_TILE_END:PLAN_TILE_END + 1, :] = tile_end.astype(I32)
    plan_ref[PLAN_COUNT:PLAN_COUNT + 1, :] = cnt.astype(I32)

    experts = idx_ref[0:2, :]
    start_i = row_start.astype(I32)
    base = jnp.zeros(experts.shape, I32)
    for e in range(N_EXPERTS):
        base = jnp.where(experts == e, start_i[:, e:e + 1], base)
    pos_ref[...] = jnp.zeros_like(pos_ref)
    pos_ref[0:2, :] = idx_ref[2:4, :] + base


def _route_plan(idx, counts):
    t = idx.shape[1]
    return pl.pallas_call(
        _plan_kernel,
        out_shape=[jax.ShapeDtypeStruct((SUBLANES, t), I32),
                   jax.ShapeDtypeStruct((SUBLANES, LANES), I32)],
        compiler_params=pltpu.CompilerParams(vmem_limit_bytes=VMEM_LIMIT),
        name="route_plan",
    )(idx, counts)


def _expert_kernel(pa_ref, pb_ref, plan_ref, h2_hbm, w1_hbm, w3_hbm, w2_hbm, ys_ref,
                   inv, gbuf, gsem, w1s, w3s, w2s, wsem, w1b, w3b, w2b, *, layer):
    i = pl.program_id(0)
    tm = EXP_TM
    n_tok = h2_hbm.shape[0]
    nt = plan_ref[PLAN_NUM_TILES, 0]
    expert = plan_ref[PLAN_TILE_EXPERT, i]

    def weight_copies(e):
        return (pltpu.make_async_copy(w1_hbm.at[layer, e], w1s, wsem.at[0]),
                pltpu.make_async_copy(w3_hbm.at[layer, e], w3s, wsem.at[1]),
                pltpu.make_async_copy(w2_hbm.at[layer, e], w2s, wsem.at[2]))

    def weight_start(e):
        for cp in weight_copies(e):
            cp.start(priority=1)

    def gather_start(tile, slot):
        base = tile * tm

        def body(j, _):
            for u in range(DMA_UNROLL):
                r = j * DMA_UNROLL + u
                pltpu.make_async_copy(h2_hbm.at[inv[base + r]], gbuf.at[slot, r], gsem.at[slot]).start()
            return 0
        lax.fori_loop(0, tm // DMA_UNROLL, body, 0)

    def gather_wait(slot):
        pltpu.make_async_copy(h2_hbm.at[pl.ds(0, tm)], gbuf.at[slot], gsem.at[slot]).wait()

    @pl.when(i == 0)
    def _():
        weight_start(plan_ref[PLAN_TILE_EXPERT, 0])

        def fill(p, _):
            inv[p] = 0
            return 0
        for e in range(N_EXPERTS):
            lax.fori_loop(plan_ref[PLAN_ROW_START, e] + plan_ref[PLAN_COUNT, e],
                          plan_ref[PLAN_TILE_END, e] * tm, fill, 0)

        def scatter(j, _):
            for u in range(DMA_UNROLL):
                tok = j * DMA_UNROLL + u
                inv[pa_ref[tok]] = tok
                inv[pb_ref[tok]] = tok
            return 0
        lax.fori_loop(0, n_tok // DMA_UNROLL, scatter, 0)
        gather_start(0, 0)

    valid = i < nt
    first = (i == 0) | (expert != plan_ref[PLAN_TILE_EXPERT, jnp.maximum(i - 1, 0)])

    @pl.when(valid & first)
    def _():
        for cp in weight_copies(expert):
            cp.wait()
        w1b[...] = w1s[...].astype(BF16)
        w3b[...] = w3s[...].astype(BF16)
        w2b[...] = w2s[...].astype(BF16)
        nxt = plan_ref[PLAN_TILE_END, expert]

        @pl.when(nxt < nt)
        def _():
            weight_start(plan_ref[PLAN_TILE_EXPERT, nxt])

    @pl.when(valid)
    def _():
        slot = i % 2

        @pl.when(i + 1 < nt)
        def _():
            gather_start(i + 1, 1 - slot)

        gather_wait(slot)
        xt = gbuf[slot].reshape(tm, D_MODEL)
        a = jnp.dot(xt, w1b[...], preferred_element_type=F32)
        g = jnp.dot(xt, w3b[...], preferred_element_type=F32)
        act = (a * _sigmoid(a) * g).astype(BF16)
        y = jnp.dot(act, w2b[...], preferred_element_type=F32)
        ys_ref[...] = y.astype(BF16).reshape(tm, ROW_TILES, LANES)

    @pl.when(jnp.logical_not(valid))
    def _():
        ys_ref[...] = jnp.zeros_like(ys_ref)


def _expert_gemm(h2t, pos_a, pos_b, plan, w1, w3, w2, layer):
    t = h2t.shape[0]
    n_tiles = (t * 2) // EXP_TM + N_EXPERTS
    n_rows = n_tiles * EXP_TM
    d, de = w1.shape[2], w1.shape[3]
    any_spec = pl.BlockSpec(memory_space=pl.ANY)
    return pl.pallas_call(
        functools.partial(_expert_kernel, layer=layer),
        grid_spec=pltpu.PrefetchScalarGridSpec(
            num_scalar_prefetch=3,
            grid=(n_tiles,),
            in_specs=[any_spec, any_spec, any_spec, any_spec],
            out_specs=pl.BlockSpec((EXP_TM, ROW_TILES, LANES), lambda i, pa, pb, plan: (i, 0, 0)),
            scratch_shapes=[pltpu.SMEM((n_rows,), I32),
                            pltpu.VMEM((2, EXP_TM, ROW_TILES, LANES), BF16),
                            pltpu.SemaphoreType.DMA((2,)),
                            pltpu.VMEM((d, de), F32),
                            pltpu.VMEM((d, de), F32),
                            pltpu.VMEM((de, d), F32),
                            pltpu.SemaphoreType.DMA((3,)),
                            pltpu.VMEM((d, de), BF16),
                            pltpu.VMEM((d, de), BF16),
                            pltpu.VMEM((de, d), BF16)]),
        out_shape=jax.ShapeDtypeStruct((n_rows, ROW_TILES, LANES), BF16),
        compiler_params=_cparams(("arbitrary",)),
        name="moe_expert_gemm",
    )(pos_a, pos_b, plan, h2t, w1, w3, w2)


def _combine_kernel(pa_ref, pb_ref, ys_hbm, x1_ref, wts_ref, mod_ref, npost_ref, o_ref, buf_a, buf_b, sems):
    i = pl.program_id(0)
    n = pl.num_programs(0)
    tm = CMB_TM

    def start_tile(tile, slot):
        def body(j, _):
            for u in range(DMA_UNROLL):
                r = j * DMA_UNROLL + u
                tok = tile * tm + r
                pltpu.make_async_copy(ys_hbm.at[pa_ref[tok]], buf_a.at[slot, r], sems.at[0, slot]).start()
                pltpu.make_async_copy(ys_hbm.at[pb_ref[tok]], buf_b.at[slot, r], sems.at[1, slot]).start()
            return 0
        lax.fori_loop(0, tm // DMA_UNROLL, body, 0)

    def wait_tile(slot):
        pltpu.make_async_copy(ys_hbm.at[pl.ds(0, tm)], buf_a.at[slot], sems.at[0, slot]).wait()
        pltpu.make_async_copy(ys_hbm.at[pl.ds(0, tm)], buf_b.at[slot], sems.at[1, slot]).wait()

    slot = i % 2

    @pl.when(i == 0)
    def _():
        start_tile(0, 0)

    @pl.when(i + 1 < n)
    def _():
        start_tile(i + 1, 1 - slot)

    wait_tile(slot)

    wa = wts_ref[:, 0:1]
    wb = wts_ref[:, 1:2]
    ya = buf_a[slot].reshape(tm, D_MODEL).astype(F32)
    yb = buf_b[slot].reshape(tm, D_MODEL).astype(F32)
    y = wa * ya + wb * yb
    gate2 = mod_ref[0, 5:6, :]
    o_ref[...] = x1_ref[...] + gate2 * (_rms_rows(y) * npost_ref[...])


def _combine(ys, pos_a, pos_b, x1, wts, mod3, npost, s):
    t, d = x1.shape
    per_b = s // CMB_TM
    row = lambda i, pa, pb: (i, 0)
    return pl.pallas_call(
        _combine_kernel,
        grid_spec=pltpu.PrefetchScalarGridSpec(
            num_scalar_prefetch=2,
            grid=(t // CMB_TM,),
            in_specs=[pl.BlockSpec(memory_space=pl.ANY),
                      pl.BlockSpec((CMB_TM, d), row),
                      pl.BlockSpec((CMB_TM, LANES), row),
                      pl.BlockSpec((1, N_MOD, d), lambda i, pa, pb: (i // per_b, 0, 0)),
                      pl.BlockSpec((1, d), lambda i, pa, pb: (0, 0))],
            out_specs=pl.BlockSpec((CMB_TM, d), row),
            scratch_shapes=[pltpu.VMEM((2, CMB_TM, ROW_TILES, LANES), BF16),
                            pltpu.VMEM((2, CMB_TM, ROW_TILES, LANES), BF16),
                            pltpu.SemaphoreType.DMA((2, 2))]),
        out_shape=jax.ShapeDtypeStruct((t, d), F32),
        compiler_params=_cparams(("arbitrary",)),
        name="moe_combine",
    )(pos_a, pos_b, ys, x1, wts, mod3, npost.reshape(1, d))


def _rope_tables(s):
    inv_freq = ROPE_BASE ** (-jnp.arange(0, HEAD_DIM, 2, dtype=F32) / HEAD_DIM)
    angle = jnp.arange(s, dtype=F32)[:, None] * inv_freq[None, :]
    cos = jnp.cos(angle)
    sin = jnp.sin(angle)
    return jnp.concatenate([cos, cos], axis=-1), jnp.concatenate([-sin, sin], axis=-1)


def _router_weights(router_coarse_w, router_fine_w, router_coarse_b, router_fine_b):
    d = router_fine_w.shape[0]
    w = jnp.concatenate([router_fine_w, router_coarse_w], axis=1)
    n = w.shape[1]
    w_hi = w.astype(BF16)
    w_lo = (w - w_hi.astype(F32)).astype(BF16)
    pad = jnp.zeros((d, LANES // 2 - n), BF16)
    rw = jnp.concatenate([w_hi, pad, w_lo, pad], axis=1)
    rb = jnp.concatenate([router_fine_b, router_coarse_b, jnp.zeros((LANES - n,), F32)]).reshape(1, LANES)
    return rw, rb


def _layer(x2, c, nb, s, cosf, sinf, lg_tab, layer, ada_w, ada_b, norm_pre_mix, norm_post_mix, w_in, fox_f_bias,
           ret_gn_w, fox_qn_w, fox_kn_w, fox_on_w, w_out, norm_pre_ffn, norm_post_ffn,
           router_coarse_w, router_coarse_b, router_fine_w, router_fine_b, w1, w3, w2):
    t, d = x2.shape
    mod3 = _modulation(c, ada_w, ada_b, layer).reshape(nb, N_MOD, d)

    proj, lf = _in_projection(x2, mod3, norm_pre_mix[layer], w_in, fox_f_bias[layer], fox_qn_w[layer],
                              fox_kn_w[layer], cosf, sinf, nb, s, layer)
    cum = _cum_forget(lf)
    proj3 = proj.reshape(nb, s, MAIN_COLS)
    ret = _retention(proj3, ret_gn_w[layer], lg_tab)
    fox = _fox_attention(proj3, cum, fox_on_w[layer])

    rw, rb = _router_weights(router_coarse_w[layer], router_fine_w[layer], router_coarse_b[layer],
                             router_fine_b[layer])
    x1, h2t, idx, wts, counts = _out_projection(
        ret.reshape(t, RET_WIDTH), fox.reshape(t, FOX_WIDTH), _cast_bf16(w_out, layer), x2, mod3,
        norm_post_mix[layer], norm_pre_ffn[layer], rw, rb, s)

    pos, plan = _route_plan(idx, counts)
    pos_a, pos_b = pos[0], pos[1]
    ys = _expert_gemm(h2t, pos_a, pos_b, plan, w1, w3, w2, layer)
    return _combine(ys, pos_a, pos_b, x1, wts, mod3, norm_post_ffn[layer], s)


def kernel(x, c, ada_w, ada_b, norm_pre_mix, norm_post_mix, w_in, fox_f_bias, ret_gn_w, fox_qn_w, fox_kn_w,
           fox_on_w, w_out, norm_pre_ffn, norm_post_ffn, router_coarse_w, router_coarse_b, router_fine_w,
           router_fine_b, w1, w3, w2):
    nb, s, d = x.shape
    depth = ada_w.shape[0]
    cosf, sinf = _rope_tables(s)
    log_gamma = jnp.log1p(-jnp.exp2(-5.0 - jnp.arange(RET_HEADS, dtype=F32)))
    lg_tab = jnp.broadcast_to(log_gamma[:, None, None], (RET_HEADS, 1, LANES))
    x2 = x.reshape(nb * s, d)
    for layer in range(depth):
        x2 = _layer(x2, c, nb, s, cosf, sinf, lg_tab, layer, ada_w, ada_b, norm_pre_mix, norm_post_mix,
                    w_in, fox_f_bias, ret_gn_w, fox_qn_w, fox_kn_w, fox_on_w, w_out, norm_pre_ffn,
                    norm_post_ffn, router_coarse_w, router_coarse_b, router_fine_w, router_fine_b, w1, w3, w2)
    return x2.reshape(nb, s, d)
```

```python
import functools

import jax
import jax.numpy as jnp
from jax import lax
from jax.experimental import pallas as pl
from jax.experimental.pallas import tpu as pltpu

F32 = jnp.float32
BF16 = jnp.bfloat16
I32 = jnp.int32

D_MODEL = 2048
HEAD_DIM = 128
RET_WIDTH = D_MODEL // 2
FOX_WIDTH = D_MODEL - RET_WIDTH
RET_HEADS = RET_WIDTH // HEAD_DIM
FOX_HEADS = FOX_WIDTH // HEAD_DIM
ROPE_BASE = 10000.0
N_GROUPS = 4
EXPERTS_PER_GROUP = 8
N_EXPERTS = N_GROUPS * EXPERTS_PER_GROUP
D_EXPERT = D_MODEL // 4
N_MOD = 6
EPS = 1e-6
MAIN_COLS = 4 * RET_WIDTH + 3 * FOX_WIDTH

LANES = 128
SUBLANES = 8
ROW_TILES = D_MODEL // LANES
NEG = -1e30
LOG2E = 1.4426950408889634

VMEM_LIMIT = 56 * 1024 * 1024

MOD_TN = 1024
IN_TM, IN_TN = 1024, 512
IN_RC = 256
RET_CHUNK = 256
RET_HPS = 2
FOX_T = 512
FOX_HPS = 2
OUT_TM = 512
OUT_RC = 256
EXP_TM = 256
CMB_TM = 256
DMA_UNROLL = 8


def _cparams(sem):
    return pltpu.CompilerParams(dimension_semantics=sem, vmem_limit_bytes=VMEM_LIMIT)


def _nt_dot(a, b):
    return lax.dot_general(a, b, (((1,), (1,)), ((), ())), preferred_element_type=F32)


def _tn_dot(a, b):
    return lax.dot_general(a, b, (((0,), (0,)), ((), ())), preferred_element_type=F32)


def _sigmoid(x):
    return 1.0 / (1.0 + jnp.exp(-x))


def _col_to_row(col):
    n = col.shape[0]
    ri = lax.broadcasted_iota(I32, (n, n), 0)
    ci = lax.broadcasted_iota(I32, (n, n), 1)
    return jnp.sum(jnp.where(ri == ci, col, 0.0), axis=0, keepdims=True)


def _row_to_col(row):
    n = row.shape[1]
    ri = lax.broadcasted_iota(I32, (n, n), 0)
    ci = lax.broadcasted_iota(I32, (n, n), 1)
    return jnp.sum(jnp.where(ri == ci, row, 0.0), axis=1, keepdims=True)


def _mod_kernel(cb_ref, w_ref, b_ref, o_ref):
    nb = cb_ref.shape[0]
    for b in range(nb):
        cv = cb_ref[b]
        ca = cv * _sigmoid(cv)
        for j in range(MOD_TN // LANES):
            sl = slice(j * LANES, (j + 1) * LANES)
            col = jnp.sum(w_ref[:, sl] * ca, axis=0, keepdims=True)
            o_ref[b:b + 1, sl] = col + b_ref[:, sl]


def _modulation(c, ada_w, ada_b, layer):
    nb, d = c.shape
    n = ada_w.shape[2]
    cb = jnp.broadcast_to(c[:, :, None], (nb, d, LANES))
    return pl.pallas_call(
        _mod_kernel,
        grid=(n // MOD_TN,),
        in_specs=[pl.BlockSpec((nb, d, LANES), lambda j: (0, 0, 0)),
                  pl.BlockSpec((None, d, MOD_TN), lambda j: (layer, 0, j)),
                  pl.BlockSpec((None, 1, MOD_TN), lambda j: (layer, 0, j))],
        out_specs=pl.BlockSpec((nb, MOD_TN), lambda j: (0, j)),
        out_shape=jax.ShapeDtypeStruct((nb, n), F32),
        compiler_params=_cparams(("arbitrary",)),
        name="adaln_mod",
    )(cb, ada_w, ada_b.reshape(ada_b.shape[0], 1, n))


def _rms_rows(x):
    return x * lax.rsqrt(jnp.mean(x * x, axis=-1, keepdims=True) + EPS)


def _inproj_kernel(x_ref, mod_ref, nw_ref, w_ref, wf_ref, fb_ref, cos_ref, sin_ref, qn_ref, kn_ref,
                   o_ref, lf_ref, h_scr, wb_scr):
    j = pl.program_id(1)
    tiles_per_sec = RET_WIDTH // IN_TN
    heads_per_tile = IN_TN // HEAD_DIM

    @pl.when(j == 0)
    def _():
        shift = mod_ref[0, 0:1, :]
        scale = 1.0 + mod_ref[0, 1:2, :]
        nw = nw_ref[...]
        rows = 128
        for r in range(IN_TM // rows):
            rs = slice(r * rows, (r + 1) * rows)
            h = _rms_rows(x_ref[rs, :]) * nw * scale + shift
            h_scr[rs, :] = h.astype(BF16)
        z = _nt_dot(wf_ref[...], h_scr[...]) + fb_ref[...]
        lf_ref[0] = jnp.minimum(z, 0.0) - jnp.log1p(jnp.exp(-jnp.abs(z)))

    wb_scr[...] = w_ref[...].astype(BF16)
    sec = j // tiles_per_sec

    def section(epilogue):
        for r in range(IN_TM // IN_RC):
            rs = slice(r * IN_RC, (r + 1) * IN_RC)
            acc = _nt_dot(h_scr[rs, :], wb_scr[...])
            epilogue(acc, rs)

    def per_head(fn):
        def epilogue(acc, rs):
            for hh in range(heads_per_tile):
                sl = slice(hh * HEAD_DIM, (hh + 1) * HEAD_DIM)
                o_ref[rs, sl] = fn(acc[:, sl], rs).astype(BF16)
        return epilogue

    def whole(fn):
        def epilogue(acc, rs):
            o_ref[rs, :] = fn(acc).astype(BF16)
        return epilogue

    def rope(v, rs):
        return v * cos_ref[rs, :] + pltpu.roll(v, HEAD_DIM // 2, 1) * sin_ref[rs, :]

    @pl.when(sec == 0)
    def _():
        section(per_head(rope))

    @pl.when(sec == 1)
    def _():
        section(per_head(lambda v, rs: rope(v, rs) * (HEAD_DIM ** -0.5)))

    @pl.when((sec == 2) | (sec == 6))
    def _():
        section(whole(lambda a: a))

    @pl.when(sec == 3)
    def _():
        section(whole(lambda a: a * _sigmoid(a)))

    @pl.when(sec == 4)
    def _():
        section(per_head(lambda v, rs: _rms_rows(v) * qn_ref[...] * (LOG2E * HEAD_DIM ** -0.5)))

    @pl.when(sec == 5)
    def _():
        section(per_head(lambda v, rs: _rms_rows(v) * kn_ref[...]))


def _in_projection(x2, mod3, norm_w, w_in, fox_f_bias, fox_qn_w, fox_kn_w, cosf, sinf, nb, s, layer):
    t, d = x2.shape
    w_t = jnp.swapaxes(w_in, 1, 2)
    wf_t = w_t[layer, MAIN_COLS:, :].astype(BF16)
    per_b = s // IN_TM
    grid = (t // IN_TM, MAIN_COLS // IN_TN)
    return pl.pallas_call(
        _inproj_kernel,
        grid=grid,
        in_specs=[
            pl.BlockSpec((IN_TM, d), lambda i, j: (i, 0)),
            pl.BlockSpec((1, N_MOD, d), lambda i, j: (i // per_b, 0, 0)),
            pl.BlockSpec((1, d), lambda i, j: (0, 0)),
            pl.BlockSpec((None, IN_TN, d), lambda i, j: (layer, j, 0)),
            pl.BlockSpec((FOX_HEADS, d), lambda i, j: (0, 0)),
            pl.BlockSpec((FOX_HEADS, 1), lambda i, j: (0, 0)),
            pl.BlockSpec((IN_TM, HEAD_DIM), lambda i, j: (i % per_b, 0)),
            pl.BlockSpec((IN_TM, HEAD_DIM), lambda i, j: (i % per_b, 0)),
            pl.BlockSpec((1, HEAD_DIM), lambda i, j: (0, 0)),
            pl.BlockSpec((1, HEAD_DIM), lambda i, j: (0, 0)),
        ],
        out_specs=[
            pl.BlockSpec((IN_TM, IN_TN), lambda i, j: (i, j)),
            pl.BlockSpec((1, FOX_HEADS, IN_TM), lambda i, j: (i // per_b, 0, i % per_b)),
        ],
        out_shape=[jax.ShapeDtypeStruct((t, MAIN_COLS), BF16),
                   jax.ShapeDtypeStruct((nb, FOX_HEADS, s), F32)],
        scratch_shapes=[pltpu.VMEM((IN_TM, d), BF16), pltpu.VMEM((IN_TN, d), BF16)],
        compiler_params=_cparams(("arbitrary", "arbitrary")),
        name="in_projection",
    )(x2, mod3, norm_w.reshape(1, d), w_t, wf_t, fox_f_bias.reshape(FOX_HEADS, 1), cosf, sinf,
      fox_qn_w.reshape(1, HEAD_DIM), fox_kn_w.reshape(1, HEAD_DIM))


def _split3(x):
    hi = x.astype(BF16)
    r1 = x - hi.astype(F32)
    mid = r1.astype(BF16)
    lo = (r1 - mid.astype(F32)).astype(BF16)
    return hi, mid, lo


def _cumsum_kernel(x_ref, o_ref, *, rows_per_seq):
    x = x_ref[...]
    r = x.shape[0]
    ri = lax.broadcasted_iota(I32, (LANES, LANES), 0)
    ci = lax.broadcasted_iota(I32, (LANES, LANES), 1)
    upper = jnp.where(ri <= ci, 1.0, 0.0).astype(BF16)
    rowcum = sum(jnp.dot(p, upper, preferred_element_type=F32) for p in _split3(x))
    tot = jnp.broadcast_to(rowcum[:, LANES - 1:LANES], (r, LANES))
    gi = lax.broadcasted_iota(I32, (r, r), 0)
    gj = lax.broadcasted_iota(I32, (r, r), 1)
    same_seq = (gi // rows_per_seq) == (gj // rows_per_seq)
    lower = jnp.where(same_seq & (gj < gi), 1.0, 0.0).astype(BF16)
    prefix = sum(jnp.dot(lower, p, preferred_element_type=F32) for p in _split3(tot))
    o_ref[...] = rowcum + prefix


def _cum_forget(lf):
    nb, nh, s = lf.shape
    rows = nb * nh * s // LANES
    out = pl.pallas_call(
        functools.partial(_cumsum_kernel, rows_per_seq=s // LANES),
        out_shape=jax.ShapeDtypeStruct((rows, LANES), F32),
        compiler_params=pltpu.CompilerParams(vmem_limit_bytes=VMEM_LIMIT),
        name="forget_cumsum",
    )(lf.reshape(rows, LANES))
    return out.reshape(nb * nh, 1, s)


def _ret_kernel(lg_ref, q_ref, k_ref, v_ref, g_ref, gnw_ref, o_ref):
    c = RET_CHUNK
    s = q_ref.shape[1]
    ri = lax.broadcasted_iota(I32, (c, c), 0)
    ci = lax.broadcasted_iota(I32, (c, c), 1)
    diff = (ri - ci).astype(F32)
    pos = lax.broadcasted_iota(I32, (c, 1), 0).astype(F32)
    consts = []
    for hh in range(RET_HPS):
        lg = lg_ref[hh][:, 0:1]
        decay = jnp.where(diff >= 0.0, jnp.exp(lg * jnp.maximum(diff, 0.0)), 0.0)
        zeta = jnp.exp(lg * (float(c - 1) - pos))
        xi = jnp.exp(lg * (pos + 1.0))
        chunk_decay = jnp.exp(lg * float(c))
        consts.append((decay, zeta, xi, chunk_decay))

    def body(n, states):
        r0 = pl.multiple_of(n * c, c)
        new_states = []
        for hh in range(RET_HPS):
            decay, zeta, xi, chunk_decay = consts[hh]
            sl = slice(hh * HEAD_DIM, (hh + 1) * HEAD_DIM)
            q = q_ref[0, pl.ds(r0, c), sl]
            k = k_ref[0, pl.ds(r0, c), sl]
            v = v_ref[0, pl.ds(r0, c), sl]
            scores = _nt_dot(q, k) * decay
            intra = jnp.dot(scores.astype(BF16), v, preferred_element_type=F32)
            cross = jnp.dot(q, states[hh].astype(BF16), preferred_element_type=F32) * xi
            kz = (k.astype(F32) * zeta).astype(BF16)
            kv = _tn_dot(kz, v)
            o = intra + cross
            mu = jnp.mean(o, axis=-1, keepdims=True)
            oc = o - mu
            var = jnp.mean(oc * oc, axis=-1, keepdims=True)
            y = oc * lax.rsqrt(var + EPS) * gnw_ref[:, sl] * g_ref[0, pl.ds(r0, c), sl].astype(F32)
            o_ref[0, pl.ds(r0, c), sl] = y.astype(BF16)
            new_states.append(states[hh] * chunk_decay + kv)
        return tuple(new_states)

    init = tuple(jnp.zeros((HEAD_DIM, HEAD_DIM), F32) for _ in range(RET_HPS))
    lax.fori_loop(0, s // c, body, init)


def _retention(proj3, ret_gn_w, lg_tab):
    nb, s, _ = proj3.shape
    w = RET_HPS * HEAD_DIM
    gpg = RET_WIDTH // w

    def col(sec):
        return pl.BlockSpec((1, s, w), lambda b, h: (b, 0, sec * gpg + h))

    return pl.pallas_call(
        _ret_kernel,
        grid=(nb, gpg),
        in_specs=[pl.BlockSpec((RET_HPS, 1, LANES), lambda b, h: (h, 0, 0)),
                  col(0), col(1), col(2), col(3),
                  pl.BlockSpec((1, w), lambda b, h: (0, h))],
        out_specs=pl.BlockSpec((1, s, w), lambda b, h: (b, 0, h)),
        out_shape=jax.ShapeDtypeStruct((nb, s, RET_WIDTH), BF16),
        compiler_params=_cparams(("arbitrary", "arbitrary")),
        name="retention",
    )(lg_tab, proj3, proj3, proj3, proj3, ret_gn_w.reshape(1, RET_WIDTH))


def _fox_kernel(q_ref, k_ref, v_ref, cum_ref, onw_ref, o_ref):
    t = FOX_T
    qi = pl.program_id(2)
    ri = lax.broadcasted_iota(I32, (t, t), 0)
    ci = lax.broadcasted_iota(I32, (t, t), 1)

    def step(ki, carry, masked):
        k0 = pl.multiple_of(ki * t, t)
        out = []
        for hh in range(FOX_HPS):
            m, l, acc = carry[hh]
            sl = slice(hh * HEAD_DIM, (hh + 1) * HEAD_DIM)
            k = k_ref[0, pl.ds(k0, t), sl]
            v = v_ref[0, pl.ds(k0, t), sl]
            sc = _nt_dot(q_ref[0, :, sl], k) - cum_ref[hh, :, pl.ds(k0, t)] * LOG2E
            if masked:
                sc = jnp.where(ci <= ri, sc, NEG)
            m_new = jnp.maximum(m, jnp.max(sc, axis=1, keepdims=True))
            p = jnp.exp2(sc - m_new)
            alpha = jnp.exp2(m - m_new)
            l = alpha * l + jnp.sum(p, axis=1, keepdims=True)
            acc = alpha * acc + jnp.dot(p.astype(BF16), v, preferred_element_type=F32)
            out.append((m_new, l, acc))
        return tuple(out)

    init = tuple((jnp.full((t, 1), NEG, F32), jnp.zeros((t, 1), F32), jnp.zeros((t, HEAD_DIM), F32))
                 for _ in range(FOX_HPS))
    carry = lax.fori_loop(0, qi, lambda ki, cr: step(ki, cr, False), init)
    carry = step(qi, carry, True)
    for hh in range(FOX_HPS):
        _, l, acc = carry[hh]
        sl = slice(hh * HEAD_DIM, (hh + 1) * HEAD_DIM)
        o = acc / l
        o_ref[0, :, sl] = (_rms_rows(o) * onw_ref[:, sl]).astype(BF16)


def _fox_attention(proj3, cum, fox_on_w):
    nb, s, _ = proj3.shape
    w = FOX_HPS * HEAD_DIM
    base = 4 * RET_WIDTH // w
    gpg = FOX_WIDTH // w
    return pl.pallas_call(
        _fox_kernel,
        grid=(nb, gpg, s // FOX_T),
        in_specs=[pl.BlockSpec((1, FOX_T, w), lambda b, h, i: (b, i, base + h)),
                  pl.BlockSpec((1, s, w), lambda b, h, i: (b, 0, base + gpg + h)),
                  pl.BlockSpec((1, s, w), lambda b, h, i: (b, 0, base + 2 * gpg + h)),
                  pl.BlockSpec((FOX_HPS, 1, s), lambda b, h, i: (b * gpg + h, 0, 0)),
                  pl.BlockSpec((1, w), lambda b, h, i: (0, h))],
        out_specs=pl.BlockSpec((1, FOX_T, w), lambda b, h, i: (b, i, h)),
        out_shape=jax.ShapeDtypeStruct((nb, s, FOX_WIDTH), BF16),
        compiler_params=_cparams(("arbitrary", "arbitrary", "arbitrary")),
        name="fox_attention",
    )(proj3, proj3, proj3, cum, fox_on_w.reshape(1, FOX_WIDTH))


def _outproj_kernel(ret_ref, fox_ref, wo_ref, x_ref, mod_ref, npost_ref, npre_ref, rw_ref, rb_ref,
                    x1_ref, h2_ref, idx_ref, wts_ref, cnt_ref, carry_scr):
    i = pl.program_id(0)

    @pl.when(i == 0)
    def _():
        carry_scr[...] = jnp.zeros_like(carry_scr)

    idx_ref[...] = jnp.zeros_like(idx_ref)
    subtiles = [slice(r * OUT_RC, (r + 1) * OUT_RC) for r in range(OUT_TM // OUT_RC)]
    ys = [jnp.dot(ret_ref[rs, :], wo_ref[:RET_WIDTH, :], preferred_element_type=F32)
          + jnp.dot(fox_ref[rs, :], wo_ref[RET_WIDTH:, :], preferred_element_type=F32) for rs in subtiles]
    for rs, y in zip(subtiles, ys):
        _outproj_rows(rs, y, x_ref, mod_ref, npost_ref, npre_ref, rw_ref, rb_ref, x1_ref, h2_ref, idx_ref,
                      wts_ref, cnt_ref, carry_scr)


def _outproj_rows(rs, y, x_ref, mod_ref, npost_ref, npre_ref, rw_ref, rb_ref,
                  x1_ref, h2_ref, idx_ref, wts_ref, cnt_ref, carry_scr):
    tm = OUT_RC
    gate1 = mod_ref[0, 2:3, :]
    shift2 = mod_ref[0, 3:4, :]
    scale2 = 1.0 + mod_ref[0, 4:5, :]
    x1 = x_ref[rs, :] + gate1 * (_rms_rows(y) * npost_ref[...])
    x1_ref[rs, :] = x1
    h2 = _rms_rows(x1) * npre_ref[...] * scale2 + shift2
    h2_ref[rs, :, :] = h2.astype(BF16).reshape(tm, ROW_TILES, LANES)

    hi = h2.astype(BF16)
    lo = (h2 - hi.astype(F32)).astype(BF16)
    rw = rw_ref[...]
    lg2 = jnp.dot(hi, rw, preferred_element_type=F32) + jnp.dot(lo, rw, preferred_element_type=F32)
    logits = lg2 + pltpu.roll(lg2, LANES // 2, 1) + rb_ref[...]

    lane = lax.broadcasted_iota(I32, (tm, LANES), 1)
    lanef = lane.astype(F32)
    big = float(LANES)

    def first_argmax(vals):
        vmax = jnp.max(vals, axis=1, keepdims=True)
        idx = jnp.min(jnp.where(vals == vmax, lanef, big), axis=1, keepdims=True)
        return vmax, idx

    is_coarse = (lane >= N_EXPERTS) & (lane < N_EXPERTS + N_GROUPS)
    cvals = jnp.where(is_coarse, logits, NEG)
    cmax, cidx = first_argmax(cvals)
    gidx = cidx - float(N_EXPERTS)
    gprob = 1.0 / jnp.sum(jnp.where(is_coarse, jnp.exp(cvals - cmax), 0.0), axis=1, keepdims=True)
    glo = gidx * float(EXPERTS_PER_GROUP)
    in_group = (lanef >= glo) & (lanef < glo + float(EXPERTS_PER_GROUP))
    fvals = jnp.where(in_group, logits, NEG)
    v1, e1 = first_argmax(fvals)
    fvals2 = jnp.where(lanef == e1, NEG, fvals)
    v2, e2 = first_argmax(fvals2)
    ex = jnp.exp(v2 - v1)
    w1 = gprob * (1.0 / (1.0 + ex))
    w2 = gprob * (ex / (1.0 + ex))

    sel1 = lanef == e1
    sel2 = lanef == e2
    onehot = jnp.where(sel1 | sel2, 1.0, 0.0)
    ri = lax.broadcasted_iota(I32, (tm, tm), 0)
    ci = lax.broadcasted_iota(I32, (tm, tm), 1)
    lower = jnp.where(ci < ri, 1.0, 0.0).astype(BF16)
    before = jnp.dot(lower, onehot.astype(BF16), preferred_element_type=F32) + carry_scr[...]
    r1 = jnp.sum(jnp.where(sel1, before, 0.0), axis=1, keepdims=True)
    r2 = jnp.sum(jnp.where(sel2, before, 0.0), axis=1, keepdims=True)
    carry_new = carry_scr[...] + jnp.sum(onehot, axis=0, keepdims=True)
    carry_scr[...] = carry_new
    cnt_ref[...] = carry_new.astype(I32)

    for row, col in enumerate((e1, e2, r1, r2)):
        idx_ref[row:row + 1, rs] = _col_to_row(col).astype(I32)
    wts_ref[rs, :] = jnp.where(lane == 0, w1, jnp.where(lane == 1, w2, 0.0))


def _out_projection(ret2, fox2, w_out_bf, x2, mod3, npost, npre, rw, rb, s):
    t, d = x2.shape
    per_b = s // OUT_TM
    row = lambda i: (i, 0)
    const = lambda i: (0, 0)
    return pl.pallas_call(
        _outproj_kernel,
        grid=(t // OUT_TM,),
        in_specs=[pl.BlockSpec((OUT_TM, RET_WIDTH), row),
                  pl.BlockSpec((OUT_TM, FOX_WIDTH), row),
                  pl.BlockSpec((d, d), const),
                  pl.BlockSpec((OUT_TM, d), row),
                  pl.BlockSpec((1, N_MOD, d), lambda i: (i // per_b, 0, 0)),
                  pl.BlockSpec((1, d), const),
                  pl.BlockSpec((1, d), const),
                  pl.BlockSpec((d, LANES), const),
                  pl.BlockSpec((1, LANES), const)],
        out_specs=[pl.BlockSpec((OUT_TM, d), row),
                   pl.BlockSpec((OUT_TM, ROW_TILES, LANES), lambda i: (i, 0, 0)),
                   pl.BlockSpec((SUBLANES, OUT_TM), lambda i: (0, i)),
                   pl.BlockSpec((OUT_TM, LANES), row),
                   pl.BlockSpec((1, LANES), const)],
        out_shape=[jax.ShapeDtypeStruct((t, d), F32),
                   jax.ShapeDtypeStruct((t, ROW_TILES, LANES), BF16),
                   jax.ShapeDtypeStruct((SUBLANES, t), I32),
                   jax.ShapeDtypeStruct((t, LANES), F32),
                   jax.ShapeDtypeStruct((1, LANES), I32)],
        scratch_shapes=[pltpu.VMEM((1, LANES), F32)],
        compiler_params=_cparams(("arbitrary",)),
        name="out_projection_router",
    )(ret2, fox2, w_out_bf, x2, mod3, npost.reshape(1, d), npre.reshape(1, d), rw, rb)


def _cast_kernel(w_ref, o_ref):
    o_ref[...] = w_ref[...].astype(BF16)


def _cast_bf16(w, layer, tr=256):
    _, r, c = w.shape
    return pl.pallas_call(
        _cast_kernel,
        grid=(r // tr,),
        in_specs=[pl.BlockSpec((None, tr, c), lambda i: (layer, i, 0))],
        out_specs=pl.BlockSpec((tr, c), lambda i: (i, 0)),
        out_shape=jax.ShapeDtypeStruct((r, c), BF16),
        compiler_params=_cparams(("arbitrary",)),
        name="cast_bf16",
    )(w)


PLAN_ROW_START = 0
PLAN_TILE_EXPERT = 1
PLAN_NUM_TILES = 2
PLAN_TILE_END = 3
PLAN_COUNT = 4


def _plan_kernel(idx_ref, cnt_ref, pos_ref, plan_ref):
    lane = lax.broadcasted_iota(I32, (1, LANES), 1)
    cnt = jnp.where(lane < N_EXPERTS, cnt_ref[...], 0).astype(F32)
    tiles = jnp.floor((cnt + float(EXP_TM - 1)) * (1.0 / EXP_TM))
    ri = lax.broadcasted_iota(I32, (LANES, LANES), 0)
    ci = lax.broadcasted_iota(I32, (LANES, LANES), 1)
    upper = jnp.where(ri <= ci, 1.0, 0.0).astype(BF16)
    tiles8 = jnp.broadcast_to(tiles, (SUBLANES, LANES)).astype(BF16)
    tile_end = jnp.dot(tiles8, upper, preferred_element_type=F32)[0:1, :]
    row_start = (tile_end - tiles) * float(EXP_TM)
    num_tiles = jnp.max(tile_end, axis=1, keepdims=True)
    end_col = _row_to_col(tile_end)
    owned = jnp.where((end_col <= ci.astype(F32)) & (ri < N_EXPERTS), 1.0, 0.0)
    tile_expert = jnp.sum(owned, axis=0, keepdims=True)
    last_expert = jnp.max(jnp.where(lane.astype(F32) < num_tiles, tile_expert, 0.0), axis=1, keepdims=True)
    tile_expert = jnp.minimum(tile_expert, last_expert)
    plan_ref[...] = jnp.zeros_like(plan_ref)
    plan_ref[PLAN_ROW_START:PLAN_ROW_START + 1, :] = row_start.astype(I32)
    plan_ref[PLAN_TILE_EXPERT:PLAN_TILE_EXPERT + 1, :] = tile_expert.astype(I32)
    plan_ref[PLAN_NUM_TILES:PLAN_NUM_TILES + 1, :] = jnp.broadcast_to(num_tiles, (1, LANES)).astype(I32)
    plan_ref[PLAN_TILE_END:PLAN_TILE_END + 1, :] = tile_end.astype(I32)
    plan_ref[PLAN_COUNT:PLAN_COUNT + 1, :] = cnt.astype(I32)

    experts = idx_ref[0:2, :]
    start_i = row_start.astype(I32)
    base = jnp.zeros(experts.shape, I32)
    for e in range(N_EXPERTS):
        base = jnp.where(experts == e, start_i[:, e:e + 1], base)
    pos_ref[...] = jnp.zeros_like(pos_ref)
    pos_ref[0:2, :] = idx_ref[2:4, :] + base


def _route_plan(idx, counts):
    t = idx.shape[1]
    return pl.pallas_call(
        _plan_kernel,
        out_shape=[jax.ShapeDtypeStruct((SUBLANES, t), I32),
                   jax.ShapeDtypeStruct((SUBLANES, LANES), I32)],
        compiler_params=pltpu.CompilerParams(vmem_limit_bytes=VMEM_LIMIT),
        name="route_plan",
    )(idx, counts)


def _expert_kernel(pa_ref, pb_ref, plan_ref, h2_hbm, w1_hbm, w3_hbm, w2_hbm, ys_ref,
                   inv, seq, gbuf, gsem, w1s, w3s, w2s, wsem, w1b, w3b, w2b, *, layer):
    i = pl.program_id(0)
    tm = EXP_TM
    n_tok = h2_hbm.shape[0]
    nt = plan_ref[PLAN_NUM_TILES, 0]
    expert = plan_ref[PLAN_TILE_EXPERT, i]

    def weight_copies(e, buf):
        return (pltpu.make_async_copy(w1_hbm.at[layer, e], w1s.at[buf], wsem.at[buf, 0]),
                pltpu.make_async_copy(w3_hbm.at[layer, e], w3s.at[buf], wsem.at[buf, 1]),
                pltpu.make_async_copy(w2_hbm.at[layer, e], w2s.at[buf], wsem.at[buf, 2]))

    def weight_start(e, buf):
        for cp in weight_copies(e, buf):
            cp.start(priority=1)

    def next_used_tile(e):
        return plan_ref[PLAN_TILE_END, e]

    def gather_start(tile, slot):
        base = tile * tm

        def body(j, _):
            for u in range(DMA_UNROLL):
                r = j * DMA_UNROLL + u
                pltpu.make_async_copy(h2_hbm.at[inv[base + r]], gbuf.at[slot, r], gsem.at[slot]).start()
            return 0
        lax.fori_loop(0, tm // DMA_UNROLL, body, 0)

    def gather_wait(slot):
        pltpu.make_async_copy(h2_hbm.at[pl.ds(0, tm)], gbuf.at[slot], gsem.at[slot]).wait()

    @pl.when(i == 0)
    def _():
        first_expert = plan_ref[PLAN_TILE_EXPERT, 0]
        weight_start(first_expert, 0)
        t1 = next_used_tile(first_expert)

        @pl.when(t1 < nt)
        def _():
            weight_start(plan_ref[PLAN_TILE_EXPERT, t1], 1)
        seq[0] = 0

        def fill(p, _):
            inv[p] = 0
            return 0
        for e in range(N_EXPERTS):
            lax.fori_loop(plan_ref[PLAN_ROW_START, e] + plan_ref[PLAN_COUNT, e],
                          plan_ref[PLAN_TILE_END, e] * tm, fill, 0)

        def scatter(j, _):
            for u in range(DMA_UNROLL):
                tok = j * DMA_UNROLL + u
                inv[pa_ref[tok]] = tok
                inv[pb_ref[tok]] = tok
            return 0
        lax.fori_loop(0, n_tok // DMA_UNROLL, scatter, 0)
        gather_start(0, 0)

    valid = i < nt
    first = (i == 0) | (expert != plan_ref[PLAN_TILE_EXPERT, jnp.maximum(i - 1, 0)])

    @pl.when(valid & first)
    def _():
        buf = seq[0] % 2
        for cp in weight_copies(expert, buf):
            cp.wait()
        w1b[...] = w1s[buf].astype(BF16)
        w3b[...] = w3s[buf].astype(BF16)
        w2b[...] = w2s[buf].astype(BF16)
        seq[0] = seq[0] + 1
        t1 = next_used_tile(expert)
        t2 = next_used_tile(plan_ref[PLAN_TILE_EXPERT, jnp.minimum(t1, LANES - 1)])

        @pl.when((t1 < nt) & (t2 < nt))
        def _():
            weight_start(plan_ref[PLAN_TILE_EXPERT, t2], buf)

    @pl.when(valid)
    def _():
        slot = i % 2

        @pl.when(i + 1 < nt)
        def _():
            gather_start(i + 1, 1 - slot)

        gather_wait(slot)
        xt = gbuf[slot].reshape(tm, D_MODEL)
        a = jnp.dot(xt, w1b[...], preferred_element_type=F32)
        g = jnp.dot(xt, w3b[...], preferred_element_type=F32)
        act = (a * _sigmoid(a) * g).astype(BF16)
        y = jnp.dot(act, w2b[...], preferred_element_type=F32)
        ys_ref[...] = y.astype(BF16).reshape(tm, ROW_TILES, LANES)

    @pl.when(jnp.logical_not(valid))
    def _():
        ys_ref[...] = jnp.zeros_like(ys_ref)


def _expert_gemm(h2t, pos_a, pos_b, plan, w1, w3, w2, layer):
    t = h2t.shape[0]
    n_tiles = (t * 2) // EXP_TM + N_EXPERTS
    n_rows = n_tiles * EXP_TM
    d, de = w1.shape[2], w1.shape[3]
    any_spec = pl.BlockSpec(memory_space=pl.ANY)
    return pl.pallas_call(
        functools.partial(_expert_kernel, layer=layer),
        grid_spec=pltpu.PrefetchScalarGridSpec(
            num_scalar_prefetch=3,
            grid=(n_tiles,),
            in_specs=[any_spec, any_spec, any_spec, any_spec],
            out_specs=pl.BlockSpec((EXP_TM, ROW_TILES, LANES), lambda i, pa, pb, plan: (i, 0, 0)),
            scratch_shapes=[pltpu.SMEM((n_rows,), I32),
                            pltpu.SMEM((1,), I32),
                            pltpu.VMEM((2, EXP_TM, ROW_TILES, LANES), BF16),
                            pltpu.SemaphoreType.DMA((2,)),
                            pltpu.VMEM((2, d, de), F32),
                            pltpu.VMEM((2, d, de), F32),
                            pltpu.VMEM((2, de, d), F32),
                            pltpu.SemaphoreType.DMA((2, 3)),
                            pltpu.VMEM((d, de), BF16),
                            pltpu.VMEM((d, de), BF16),
                            pltpu.VMEM((de, d), BF16)]),
        out_shape=jax.ShapeDtypeStruct((n_rows, ROW_TILES, LANES), BF16),
        compiler_params=_cparams(("arbitrary",)),
        name="moe_expert_gemm",
    )(pos_a, pos_b, plan, h2t, w1, w3, w2)


def _combine_kernel(pa_ref, pb_ref, ys_hbm, x1_ref, wts_ref, mod_ref, npost_ref, o_ref, buf_a, buf_b, sems):
    i = pl.program_id(0)
    n = pl.num_programs(0)
    tm = CMB_TM

    def start_tile(tile, slot):
        def body(j, _):
            for u in range(DMA_UNROLL):
                r = j * DMA_UNROLL + u
                tok = tile * tm + r
                pltpu.make_async_copy(ys_hbm.at[pa_ref[tok]], buf_a.at[slot, r], sems.at[0, slot]).start()
                pltpu.make_async_copy(ys_hbm.at[pb_ref[tok]], buf_b.at[slot, r], sems.at[1, slot]).start()
            return 0
        lax.fori_loop(0, tm // DMA_UNROLL, body, 0)

    def wait_tile(slot):
        pltpu.make_async_copy(ys_hbm.at[pl.ds(0, tm)], buf_a.at[slot], sems.at[0, slot]).wait()
        pltpu.make_async_copy(ys_hbm.at[pl.ds(0, tm)], buf_b.at[slot], sems.at[1, slot]).wait()

    slot = i % 2

    @pl.when(i == 0)
    def _():
        start_tile(0, 0)

    @pl.when(i + 1 < n)
    def _():
        start_tile(i + 1, 1 - slot)

    wait_tile(slot)

    wa = wts_ref[:, 0:1]
    wb = wts_ref[:, 1:2]
    ya = buf_a[slot].reshape(tm, D_MODEL).astype(F32)
    yb = buf_b[slot].reshape(tm, D_MODEL).astype(F32)
    y = wa * ya + wb * yb
    gate2 = mod_ref[0, 5:6, :]
    o_ref[...] = x1_ref[...] + gate2 * (_rms_rows(y) * npost_ref[...])


def _combine(ys, pos_a, pos_b, x1, wts, mod3, npost, s):
    t, d = x1.shape
    per_b = s // CMB_TM
    row = lambda i, pa, pb: (i, 0)
    return pl.pallas_call(
        _combine_kernel,
        grid_spec=pltpu.PrefetchScalarGridSpec(
            num_scalar_prefetch=2,
            grid=(t // CMB_TM,),
            in_specs=[pl.BlockSpec(memory_space=pl.ANY),
                      pl.BlockSpec((CMB_TM, d), row),
                      pl.BlockSpec((CMB_TM, LANES), row),
                      pl.BlockSpec((1, N_MOD, d), lambda i, pa, pb: (i // per_b, 0, 0)),
                      pl.BlockSpec((1, d), lambda i, pa, pb: (0, 0))],
            out_specs=pl.BlockSpec((CMB_TM, d), row),
            scratch_shapes=[pltpu.VMEM((2, CMB_TM, ROW_TILES, LANES), BF16),
                            pltpu.VMEM((2, CMB_TM, ROW_TILES, LANES), BF16),
                            pltpu.SemaphoreType.DMA((2, 2))]),
        out_shape=jax.ShapeDtypeStruct((t, d), F32),
        compiler_params=_cparams(("arbitrary",)),
        name="moe_combine",
    )(pos_a, pos_b, ys, x1, wts, mod3, npost.reshape(1, d))


def _rope_tables(s):
    inv_freq = ROPE_BASE ** (-jnp.arange(0, HEAD_DIM, 2, dtype=F32) / HEAD_DIM)
    angle = jnp.arange(s, dtype=F32)[:, None] * inv_freq[None, :]
    cos = jnp.cos(angle)
    sin = jnp.sin(angle)
    return jnp.concatenate([cos, cos], axis=-1), jnp.concatenate([-sin, sin], axis=-1)


def _router_weights(router_coarse_w, router_fine_w, router_coarse_b, router_fine_b):
    d = router_fine_w.shape[0]
    w = jnp.concatenate([router_fine_w, router_coarse_w], axis=1)
    n = w.shape[1]
    w_hi = w.astype(BF16)
    w_lo = (w - w_hi.astype(F32)).astype(BF16)
    pad = jnp.zeros((d, LANES // 2 - n), BF16)
    rw = jnp.concatenate([w_hi, pad, w_lo, pad], axis=1)
    rb = jnp.concatenate([router_fine_b, router_coarse_b, jnp.zeros((LANES - n,), F32)]).reshape(1, LANES)
    return rw, rb


def _layer(x2, c, nb, s, cosf, sinf, lg_tab, layer, ada_w, ada_b, norm_pre_mix, norm_post_mix, w_in, fox_f_bias,
           ret_gn_w, fox_qn_w, fox_kn_w, fox_on_w, w_out, norm_pre_ffn, norm_post_ffn,
           router_coarse_w, router_coarse_b, router_fine_w, router_fine_b, w1, w3, w2):
    t, d = x2.shape
    mod3 = _modulation(c, ada_w, ada_b, layer).reshape(nb, N_MOD, d)

    proj, lf = _in_projection(x2, mod3, norm_pre_mix[layer], w_in, fox_f_bias[layer], fox_qn_w[layer],
                              fox_kn_w[layer], cosf, sinf, nb, s, layer)
    cum = _cum_forget(lf)
    proj3 = proj.reshape(nb, s, MAIN_COLS)
    ret = _retention(proj3, ret_gn_w[layer], lg_tab)
    fox = _fox_attention(proj3, cum, fox_on_w[layer])

    rw, rb = _router_weights(router_coarse_w[layer], router_fine_w[layer], router_coarse_b[layer],
                             router_fine_b[layer])
    x1, h2t, idx, wts, counts = _out_projection(
        ret.reshape(t, RET_WIDTH), fox.reshape(t, FOX_WIDTH), _cast_bf16(w_out, layer), x2, mod3,
        norm_post_mix[layer], norm_pre_ffn[layer], rw, rb, s)

    pos, plan = _route_plan(idx, counts)
    pos_a, pos_b = pos[0], pos[1]
    ys = _expert_gemm(h2t, pos_a, pos_b, plan, w1, w3, w2, layer)
    return _combine(ys, pos_a, pos_b, x1, wts, mod3, norm_post_ffn[layer], s)


def kernel(x, c, ada_w, ada_b, norm_pre_mix, norm_post_mix, w_in, fox_f_bias, ret_gn_w, fox_qn_w, fox_kn_w,
           fox_on_w, w_out, norm_pre_ffn, norm_post_ffn, router_coarse_w, router_coarse_b, router_fine_w,
           router_fine_b, w1, w3, w2):
    nb, s, d = x.shape
    depth = ada_w.shape[0]
    cosf, sinf = _rope_tables(s)
    log_gamma = jnp.log1p(-jnp.exp2(-5.0 - jnp.arange(RET_HEADS, dtype=F32)))
    lg_tab = jnp.broadcast_to(log_gamma[:, None, None], (RET_HEADS, 1, LANES))
    x2 = x.reshape(nb * s, d)
    for layer in range(depth):
        x2 = _layer(x2, c, nb, s, cosf, sinf, lg_tab, layer, ada_w, ada_b, norm_pre_mix, norm_post_mix,
                    w_in, fox_f_bias, ret_gn_w, fox_qn_w, fox_kn_w, fox_on_w, w_out, norm_pre_ffn,
                    norm_post_ffn, router_coarse_w, router_coarse_b, router_fine_w, router_fine_b, w1, w3, w2)
    return x2.reshape(nb, s, d)
```

```python
import functools

import jax
import jax.numpy as jnp
from jax import lax
from jax.experimental import pallas as pl
from jax.experimental.pallas import tpu as pltpu

F32 = jnp.float32
BF16 = jnp.bfloat16
I32 = jnp.int32

D_MODEL = 2048
HEAD_DIM = 128
RET_WIDTH = D_MODEL // 2
FOX_WIDTH = D_MODEL - RET_WIDTH
RET_HEADS = RET_WIDTH // HEAD_DIM
FOX_HEADS = FOX_WIDTH // HEAD_DIM
ROPE_BASE = 10000.0
N_GROUPS = 4
EXPERTS_PER_GROUP = 8
N_EXPERTS = N_GROUPS * EXPERTS_PER_GROUP
D_EXPERT = D_MODEL // 4
N_MOD = 6
EPS = 1e-6
MAIN_COLS = 4 * RET_WIDTH + 3 * FOX_WIDTH

LANES = 128
SUBLANES = 8
ROW_TILES = D_MODEL // LANES
NEG = -1e30
LOG2E = 1.4426950408889634

VMEM_LIMIT = 56 * 1024 * 1024

MOD_TN = 1024
IN_TM, IN_TN = 1024, 512
IN_RC = 256
RET_CHUNK = 256
RET_HPS = 2
FOX_T = 512
FOX_HPS = 2
OUT_TM = 512
OUT_RC = 256
EXP_TM = 256
DSP_TM = 512
DSP_UNROLL = 8
CMB_TM = 256
DMA_UNROLL = 8


def _cparams(sem):
    return pltpu.CompilerParams(dimension_semantics=sem, vmem_limit_bytes=VMEM_LIMIT)


def _nt_dot(a, b):
    return lax.dot_general(a, b, (((1,), (1,)), ((), ())), preferred_element_type=F32)


def _tn_dot(a, b):
    return lax.dot_general(a, b, (((0,), (0,)), ((), ())), preferred_element_type=F32)


def _sigmoid(x):
    return 1.0 / (1.0 + jnp.exp(-x))


def _col_to_row(col):
    n = col.shape[0]
    ri = lax.broadcasted_iota(I32, (n, n), 0)
    ci = lax.broadcasted_iota(I32, (n, n), 1)
    return jnp.sum(jnp.where(ri == ci, col, 0.0), axis=0, keepdims=True)


def _row_to_col(row):
    n = row.shape[1]
    ri = lax.broadcasted_iota(I32, (n, n), 0)
    ci = lax.broadcasted_iota(I32, (n, n), 1)
    return jnp.sum(jnp.where(ri == ci, row, 0.0), axis=1, keepdims=True)


def _mod_kernel(cb_ref, w_ref, b_ref, o_ref):
    nb = cb_ref.shape[0]
    for b in range(nb):
        cv = cb_ref[b]
        ca = cv * _sigmoid(cv)
        for j in range(MOD_TN // LANES):
            sl = slice(j * LANES, (j + 1) * LANES)
            col = jnp.sum(w_ref[:, sl] * ca, axis=0, keepdims=True)
            o_ref[b:b + 1, sl] = col + b_ref[:, sl]


def _modulation(c, ada_w, ada_b, layer):
    nb, d = c.shape
    n = ada_w.shape[2]
    cb = jnp.broadcast_to(c[:, :, None], (nb, d, LANES))
    return pl.pallas_call(
        _mod_kernel,
        grid=(n // MOD_TN,),
        in_specs=[pl.BlockSpec((nb, d, LANES), lambda j: (0, 0, 0)),
                  pl.BlockSpec((None, d, MOD_TN), lambda j: (layer, 0, j)),
                  pl.BlockSpec((None, 1, MOD_TN), lambda j: (layer, 0, j))],
        out_specs=pl.BlockSpec((nb, MOD_TN), lambda j: (0, j)),
        out_shape=jax.ShapeDtypeStruct((nb, n), F32),
        compiler_params=_cparams(("arbitrary",)),
        name="adaln_mod",
    )(cb, ada_w, ada_b.reshape(ada_b.shape[0], 1, n))


def _rms_rows(x):
    return x * lax.rsqrt(jnp.mean(x * x, axis=-1, keepdims=True) + EPS)


def _inproj_kernel(x_ref, mod_ref, nw_ref, w_ref, wf_ref, fb_ref, cos_ref, sin_ref, qn_ref, kn_ref,
                   o_ref, lf_ref, h_scr, wb_scr):
    j = pl.program_id(1)
    tiles_per_sec = RET_WIDTH // IN_TN
    heads_per_tile = IN_TN // HEAD_DIM

    @pl.when(j == 0)
    def _():
        shift = mod_ref[0, 0:1, :]
        scale = 1.0 + mod_ref[0, 1:2, :]
        nw = nw_ref[...]
        rows = 128
        for r in range(IN_TM // rows):
            rs = slice(r * rows, (r + 1) * rows)
            h = _rms_rows(x_ref[rs, :]) * nw * scale + shift
            h_scr[rs, :] = h.astype(BF16)
        z = _nt_dot(wf_ref[...], h_scr[...]) + fb_ref[...]
        lf_ref[0] = jnp.minimum(z, 0.0) - jnp.log1p(jnp.exp(-jnp.abs(z)))

    wb_scr[...] = w_ref[...].astype(BF16)
    sec = j // tiles_per_sec

    def section(epilogue):
        for r in range(IN_TM // IN_RC):
            rs = slice(r * IN_RC, (r + 1) * IN_RC)
            acc = _nt_dot(h_scr[rs, :], wb_scr[...])
            epilogue(acc, rs)

    def per_head(fn):
        def epilogue(acc, rs):
            for hh in range(heads_per_tile):
                sl = slice(hh * HEAD_DIM, (hh + 1) * HEAD_DIM)
                o_ref[rs, sl] = fn(acc[:, sl], rs).astype(BF16)
        return epilogue

    def whole(fn):
        def epilogue(acc, rs):
            o_ref[rs, :] = fn(acc).astype(BF16)
        return epilogue

    def rope(v, rs):
        return v * cos_ref[rs, :] + pltpu.roll(v, HEAD_DIM // 2, 1) * sin_ref[rs, :]

    @pl.when(sec == 0)
    def _():
        section(per_head(rope))

    @pl.when(sec == 1)
    def _():
        section(per_head(lambda v, rs: rope(v, rs) * (HEAD_DIM ** -0.5)))

    @pl.when((sec == 2) | (sec == 6))
    def _():
        section(whole(lambda a: a))

    @pl.when(sec == 3)
    def _():
        section(whole(lambda a: a * _sigmoid(a)))

    @pl.when(sec == 4)
    def _():
        section(per_head(lambda v, rs: _rms_rows(v) * qn_ref[...] * (LOG2E * HEAD_DIM ** -0.5)))

    @pl.when(sec == 5)
    def _():
        section(per_head(lambda v, rs: _rms_rows(v) * kn_ref[...]))


def _in_projection(x2, mod3, norm_w, w_in, fox_f_bias, fox_qn_w, fox_kn_w, cosf, sinf, nb, s, layer):
    t, d = x2.shape
    w_t = jnp.swapaxes(w_in, 1, 2)
    wf_t = w_t[layer, MAIN_COLS:, :].astype(BF16)
    per_b = s // IN_TM
    grid = (t // IN_TM, MAIN_COLS // IN_TN)
    return pl.pallas_call(
        _inproj_kernel,
        grid=grid,
        in_specs=[
            pl.BlockSpec((IN_TM, d), lambda i, j: (i, 0)),
            pl.BlockSpec((1, N_MOD, d), lambda i, j: (i // per_b, 0, 0)),
            pl.BlockSpec((1, d), lambda i, j: (0, 0)),
            pl.BlockSpec((None, IN_TN, d), lambda i, j: (layer, j, 0)),
            pl.BlockSpec((FOX_HEADS, d), lambda i, j: (0, 0)),
            pl.BlockSpec((FOX_HEADS, 1), lambda i, j: (0, 0)),
            pl.BlockSpec((IN_TM, HEAD_DIM), lambda i, j: (i % per_b, 0)),
            pl.BlockSpec((IN_TM, HEAD_DIM), lambda i, j: (i % per_b, 0)),
            pl.BlockSpec((1, HEAD_DIM), lambda i, j: (0, 0)),
            pl.BlockSpec((1, HEAD_DIM), lambda i, j: (0, 0)),
        ],
        out_specs=[
            pl.BlockSpec((IN_TM, IN_TN), lambda i, j: (i, j)),
            pl.BlockSpec((1, FOX_HEADS, IN_TM), lambda i, j: (i // per_b, 0, i % per_b)),
        ],
        out_shape=[jax.ShapeDtypeStruct((t, MAIN_COLS), BF16),
                   jax.ShapeDtypeStruct((nb, FOX_HEADS, s), F32)],
        scratch_shapes=[pltpu.VMEM((IN_TM, d), BF16), pltpu.VMEM((IN_TN, d), BF16)],
        compiler_params=_cparams(("arbitrary", "arbitrary")),
        name="in_projection",
    )(x2, mod3, norm_w.reshape(1, d), w_t, wf_t, fox_f_bias.reshape(FOX_HEADS, 1), cosf, sinf,
      fox_qn_w.reshape(1, HEAD_DIM), fox_kn_w.reshape(1, HEAD_DIM))


def _split3(x):
    hi = x.astype(BF16)
    r1 = x - hi.astype(F32)
    mid = r1.astype(BF16)
    lo = (r1 - mid.astype(F32)).astype(BF16)
    return hi, mid, lo


def _cumsum_kernel(x_ref, o_ref, *, rows_per_seq):
    x = x_ref[...]
    r = x.shape[0]
    ri = lax.broadcasted_iota(I32, (LANES, LANES), 0)
    ci = lax.broadcasted_iota(I32, (LANES, LANES), 1)
    upper = jnp.where(ri <= ci, 1.0, 0.0).astype(BF16)
    rowcum = sum(jnp.dot(p, upper, preferred_element_type=F32) for p in _split3(x))
    tot = jnp.broadcast_to(rowcum[:, LANES - 1:LANES], (r, LANES))
    gi = lax.broadcasted_iota(I32, (r, r), 0)
    gj = lax.broadcasted_iota(I32, (r, r), 1)
    same_seq = (gi // rows_per_seq) == (gj // rows_per_seq)
    lower = jnp.where(same_seq & (gj < gi), 1.0, 0.0).astype(BF16)
    prefix = sum(jnp.dot(lower, p, preferred_element_type=F32) for p in _split3(tot))
    o_ref[...] = rowcum + prefix


def _cum_forget(lf):
    nb, nh, s = lf.shape
    rows = nb * nh * s // LANES
    out = pl.pallas_call(
        functools.partial(_cumsum_kernel, rows_per_seq=s // LANES),
        out_shape=jax.ShapeDtypeStruct((rows, LANES), F32),
        compiler_params=pltpu.CompilerParams(vmem_limit_bytes=VMEM_LIMIT),
        name="forget_cumsum",
    )(lf.reshape(rows, LANES))
    return out.reshape(nb * nh, 1, s)


def _ret_kernel(lg_ref, q_ref, k_ref, v_ref, g_ref, gnw_ref, o_ref):
    c = RET_CHUNK
    s = q_ref.shape[1]
    ri = lax.broadcasted_iota(I32, (c, c), 0)
    ci = lax.broadcasted_iota(I32, (c, c), 1)
    diff = (ri - ci).astype(F32)
    pos = lax.broadcasted_iota(I32, (c, 1), 0).astype(F32)
    consts = []
    for hh in range(RET_HPS):
        lg = lg_ref[hh][:, 0:1]
        decay = jnp.where(diff >= 0.0, jnp.exp(lg * jnp.maximum(diff, 0.0)), 0.0)
        zeta = jnp.exp(lg * (float(c - 1) - pos))
        xi = jnp.exp(lg * (pos + 1.0))
        chunk_decay = jnp.exp(lg * float(c))
        consts.append((decay, zeta, xi, chunk_decay))

    def body(n, states):
        r0 = pl.multiple_of(n * c, c)
        new_states = []
        for hh in range(RET_HPS):
            decay, zeta, xi, chunk_decay = consts[hh]
            sl = slice(hh * HEAD_DIM, (hh + 1) * HEAD_DIM)
            q = q_ref[0, pl.ds(r0, c), sl]
            k = k_ref[0, pl.ds(r0, c), sl]
            v = v_ref[0, pl.ds(r0, c), sl]
            scores = _nt_dot(q, k) * decay
            intra = jnp.dot(scores.astype(BF16), v, preferred_element_type=F32)
            cross = jnp.dot(q, states[hh].astype(BF16), preferred_element_type=F32) * xi
            kz = (k.astype(F32) * zeta).astype(BF16)
            kv = _tn_dot(kz, v)
            o = intra + cross
            mu = jnp.mean(o, axis=-1, keepdims=True)
            oc = o - mu
            var = jnp.mean(oc * oc, axis=-1, keepdims=True)
            y = oc * lax.rsqrt(var + EPS) * gnw_ref[:, sl] * g_ref[0, pl.ds(r0, c), sl].astype(F32)
            o_ref[0, pl.ds(r0, c), sl] = y.astype(BF16)
            new_states.append(states[hh] * chunk_decay + kv)
        return tuple(new_states)

    init = tuple(jnp.zeros((HEAD_DIM, HEAD_DIM), F32) for _ in range(RET_HPS))
    lax.fori_loop(0, s // c, body, init)


def _retention(proj3, ret_gn_w, lg_tab):
    nb, s, _ = proj3.shape
    w = RET_HPS * HEAD_DIM
    gpg = RET_WIDTH // w

    def col(sec):
        return pl.BlockSpec((1, s, w), lambda b, h: (b, 0, sec * gpg + h))

    return pl.pallas_call(
        _ret_kernel,
        grid=(nb, gpg),
        in_specs=[pl.BlockSpec((RET_HPS, 1, LANES), lambda b, h: (h, 0, 0)),
                  col(0), col(1), col(2), col(3),
                  pl.BlockSpec((1, w), lambda b, h: (0, h))],
        out_specs=pl.BlockSpec((1, s, w), lambda b, h: (b, 0, h)),
        out_shape=jax.ShapeDtypeStruct((nb, s, RET_WIDTH), BF16),
        compiler_params=_cparams(("arbitrary", "arbitrary")),
        name="retention",
    )(lg_tab, proj3, proj3, proj3, proj3, ret_gn_w.reshape(1, RET_WIDTH))


def _fox_kernel(q_ref, k_ref, v_ref, cum_ref, onw_ref, o_ref):
    t = FOX_T
    qi = pl.program_id(2)
    ri = lax.broadcasted_iota(I32, (t, t), 0)
    ci = lax.broadcasted_iota(I32, (t, t), 1)

    def step(ki, carry, masked):
        k0 = pl.multiple_of(ki * t, t)
        out = []
        for hh in range(FOX_HPS):
            m, l, acc = carry[hh]
            sl = slice(hh * HEAD_DIM, (hh + 1) * HEAD_DIM)
            k = k_ref[0, pl.ds(k0, t), sl]
            v = v_ref[0, pl.ds(k0, t), sl]
            sc = _nt_dot(q_ref[0, :, sl], k) - cum_ref[hh, :, pl.ds(k0, t)] * LOG2E
            if masked:
                sc = jnp.where(ci <= ri, sc, NEG)
            m_new = jnp.maximum(m, jnp.max(sc, axis=1, keepdims=True))
            p = jnp.exp2(sc - m_new)
            alpha = jnp.exp2(m - m_new)
            l = alpha * l + jnp.sum(p, axis=1, keepdims=True)
            acc = alpha * acc + jnp.dot(p.astype(BF16), v, preferred_element_type=F32)
            out.append((m_new, l, acc))
        return tuple(out)

    init = tuple((jnp.full((t, 1), NEG, F32), jnp.zeros((t, 1), F32), jnp.zeros((t, HEAD_DIM), F32))
                 for _ in range(FOX_HPS))
    carry = lax.fori_loop(0, qi, lambda ki, cr: step(ki, cr, False), init)
    carry = step(qi, carry, True)
    for hh in range(FOX_HPS):
        _, l, acc = carry[hh]
        sl = slice(hh * HEAD_DIM, (hh + 1) * HEAD_DIM)
        o = acc / l
        o_ref[0, :, sl] = (_rms_rows(o) * onw_ref[:, sl]).astype(BF16)


def _fox_attention(proj3, cum, fox_on_w):
    nb, s, _ = proj3.shape
    w = FOX_HPS * HEAD_DIM
    base = 4 * RET_WIDTH // w
    gpg = FOX_WIDTH // w
    return pl.pallas_call(
        _fox_kernel,
        grid=(nb, gpg, s // FOX_T),
        in_specs=[pl.BlockSpec((1, FOX_T, w), lambda b, h, i: (b, i, base + h)),
                  pl.BlockSpec((1, s, w), lambda b, h, i: (b, 0, base + gpg + h)),
                  pl.BlockSpec((1, s, w), lambda b, h, i: (b, 0, base + 2 * gpg + h)),
                  pl.BlockSpec((FOX_HPS, 1, s), lambda b, h, i: (b * gpg + h, 0, 0)),
                  pl.BlockSpec((1, w), lambda b, h, i: (0, h))],
        out_specs=pl.BlockSpec((1, FOX_T, w), lambda b, h, i: (b, i, h)),
        out_shape=jax.ShapeDtypeStruct((nb, s, FOX_WIDTH), BF16),
        compiler_params=_cparams(("arbitrary", "arbitrary", "arbitrary")),
        name="fox_attention",
    )(proj3, proj3, proj3, cum, fox_on_w.reshape(1, FOX_WIDTH))


def _outproj_kernel(ret_ref, fox_ref, wo_ref, x_ref, mod_ref, npost_ref, npre_ref, rw_ref, rb_ref,
                    x1_ref, h2_ref, idx_ref, wts_ref, cnt_ref, carry_scr):
    i = pl.program_id(0)

    @pl.when(i == 0)
    def _():
        carry_scr[...] = jnp.zeros_like(carry_scr)

    idx_ref[...] = jnp.zeros_like(idx_ref)
    subtiles = [slice(r * OUT_RC, (r + 1) * OUT_RC) for r in range(OUT_TM // OUT_RC)]
    ys = [jnp.dot(ret_ref[rs, :], wo_ref[:RET_WIDTH, :], preferred_element_type=F32)
          + jnp.dot(fox_ref[rs, :], wo_ref[RET_WIDTH:, :], preferred_element_type=F32) for rs in subtiles]
    for rs, y in zip(subtiles, ys):
        _outproj_rows(rs, y, x_ref, mod_ref, npost_ref, npre_ref, rw_ref, rb_ref, x1_ref, h2_ref, idx_ref,
                      wts_ref, cnt_ref, carry_scr)


def _outproj_rows(rs, y, x_ref, mod_ref, npost_ref, npre_ref, rw_ref, rb_ref,
                  x1_ref, h2_ref, idx_ref, wts_ref, cnt_ref, carry_scr):
    tm = OUT_RC
    gate1 = mod_ref[0, 2:3, :]
    shift2 = mod_ref[0, 3:4, :]
    scale2 = 1.0 + mod_ref[0, 4:5, :]
    x1 = x_ref[rs, :] + gate1 * (_rms_rows(y) * npost_ref[...])
    x1_ref[rs, :] = x1
    h2 = _rms_rows(x1) * npre_ref[...] * scale2 + shift2
    h2_ref[rs, :, :] = h2.astype(BF16).reshape(tm, ROW_TILES, LANES)

    hi = h2.astype(BF16)
    lo = (h2 - hi.astype(F32)).astype(BF16)
    rw = rw_ref[...]
    lg2 = jnp.dot(hi, rw, preferred_element_type=F32) + jnp.dot(lo, rw, preferred_element_type=F32)
    logits = lg2 + pltpu.roll(lg2, LANES // 2, 1) + rb_ref[...]

    lane = lax.broadcasted_iota(I32, (tm, LANES), 1)
    lanef = lane.astype(F32)
    big = float(LANES)

    def first_argmax(vals):
        vmax = jnp.max(vals, axis=1, keepdims=True)
        idx = jnp.min(jnp.where(vals == vmax, lanef, big), axis=1, keepdims=True)
        return vmax, idx

    is_coarse = (lane >= N_EXPERTS) & (lane < N_EXPERTS + N_GROUPS)
    cvals = jnp.where(is_coarse, logits, NEG)
    cmax, cidx = first_argmax(cvals)
    gidx = cidx - float(N_EXPERTS)
    gprob = 1.0 / jnp.sum(jnp.where(is_coarse, jnp.exp(cvals - cmax), 0.0), axis=1, keepdims=True)
    glo = gidx * float(EXPERTS_PER_GROUP)
    in_group = (lanef >= glo) & (lanef < glo + float(EXPERTS_PER_GROUP))
    fvals = jnp.where(in_group, logits, NEG)
    v1, e1 = first_argmax(fvals)
    fvals2 = jnp.where(lanef == e1, NEG, fvals)
    v2, e2 = first_argmax(fvals2)
    ex = jnp.exp(v2 - v1)
    w1 = gprob * (1.0 / (1.0 + ex))
    w2 = gprob * (ex / (1.0 + ex))

    sel1 = lanef == e1
    sel2 = lanef == e2
    onehot = jnp.where(sel1 | sel2, 1.0, 0.0)
    ri = lax.broadcasted_iota(I32, (tm, tm), 0)
    ci = lax.broadcasted_iota(I32, (tm, tm), 1)
    lower = jnp.where(ci < ri, 1.0, 0.0).astype(BF16)
    before = jnp.dot(lower, onehot.astype(BF16), preferred_element_type=F32) + carry_scr[...]
    r1 = jnp.sum(jnp.where(sel1, before, 0.0), axis=1, keepdims=True)
    r2 = jnp.sum(jnp.where(sel2, before, 0.0), axis=1, keepdims=True)
    carry_new = carry_scr[...] + jnp.sum(onehot, axis=0, keepdims=True)
    carry_scr[...] = carry_new
    cnt_ref[...] = carry_new.astype(I32)

    for row, col in enumerate((e1, e2, r1, r2)):
        idx_ref[row:row + 1, rs] = _col_to_row(col).astype(I32)
    wts_ref[rs, :] = jnp.where(lane == 0, w1, jnp.where(lane == 1, w2, 0.0))


def _out_projection(ret2, fox2, w_out_bf, x2, mod3, npost, npre, rw, rb, s):
    t, d = x2.shape
    per_b = s // OUT_TM
    row = lambda i: (i, 0)
    const = lambda i: (0, 0)
    return pl.pallas_call(
        _outproj_kernel,
        grid=(t // OUT_TM,),
        in_specs=[pl.BlockSpec((OUT_TM, RET_WIDTH), row),
                  pl.BlockSpec((OUT_TM, FOX_WIDTH), row),
                  pl.BlockSpec((d, d), const),
                  pl.BlockSpec((OUT_TM, d), row),
                  pl.BlockSpec((1, N_MOD, d), lambda i: (i // per_b, 0, 0)),
                  pl.BlockSpec((1, d), const),
                  pl.BlockSpec((1, d), const),
                  pl.BlockSpec((d, LANES), const),
                  pl.BlockSpec((1, LANES), const)],
        out_specs=[pl.BlockSpec((OUT_TM, d), row),
                   pl.BlockSpec((OUT_TM, ROW_TILES, LANES), lambda i: (i, 0, 0)),
                   pl.BlockSpec((SUBLANES, OUT_TM), lambda i: (0, i)),
                   pl.BlockSpec((OUT_TM, LANES), row),
                   pl.BlockSpec((1, LANES), const)],
        out_shape=[jax.ShapeDtypeStruct((t, d), F32),
                   jax.ShapeDtypeStruct((t, ROW_TILES, LANES), BF16),
                   jax.ShapeDtypeStruct((SUBLANES, t), I32),
                   jax.ShapeDtypeStruct((t, LANES), F32),
                   jax.ShapeDtypeStruct((1, LANES), I32)],
        scratch_shapes=[pltpu.VMEM((1, LANES), F32)],
        compiler_params=_cparams(("arbitrary",)),
        name="out_projection_router",
    )(ret2, fox2, w_out_bf, x2, mod3, npost.reshape(1, d), npre.reshape(1, d), rw, rb)


def _cast_kernel(w_ref, o_ref):
    o_ref[...] = w_ref[...].astype(BF16)


def _cast_bf16(w, layer, tr=256):
    _, r, c = w.shape
    return pl.pallas_call(
        _cast_kernel,
        grid=(r // tr,),
        in_specs=[pl.BlockSpec((None, tr, c), lambda i: (layer, i, 0))],
        out_specs=pl.BlockSpec((tr, c), lambda i: (i, 0)),
        out_shape=jax.ShapeDtypeStruct((r, c), BF16),
        compiler_params=_cparams(("arbitrary",)),
        name="cast_bf16",
    )(w)


PLAN_ROW_START = 0
PLAN_TILE_EXPERT = 1
PLAN_NUM_TILES = 2
PLAN_TILE_END = 3
PLAN_COUNT = 4


def _plan_kernel(idx_ref, cnt_ref, pos_ref, plan_ref):
    lane = lax.broadcasted_iota(I32, (1, LANES), 1)
    cnt = jnp.where(lane < N_EXPERTS, cnt_ref[...], 0).astype(F32)
    tiles = jnp.floor((cnt + float(EXP_TM - 1)) * (1.0 / EXP_TM))
    ri = lax.broadcasted_iota(I32, (LANES, LANES), 0)
    ci = lax.broadcasted_iota(I32, (LANES, LANES), 1)
    upper = jnp.where(ri <= ci, 1.0, 0.0).astype(BF16)
    tiles8 = jnp.broadcast_to(tiles, (SUBLANES, LANES)).astype(BF16)
    tile_end = jnp.dot(tiles8, upper, preferred_element_type=F32)[0:1, :]
    row_start = (tile_end - tiles) * float(EXP_TM)
    num_tiles = jnp.max(tile_end, axis=1, keepdims=True)
    end_col = _row_to_col(tile_end)
    owned = jnp.where((end_col <= ci.astype(F32)) & (ri < N_EXPERTS), 1.0, 0.0)
    tile_expert = jnp.sum(owned, axis=0, keepdims=True)
    last_expert = jnp.max(jnp.where(lane.astype(F32) < num_tiles, tile_expert, 0.0), axis=1, keepdims=True)
    tile_expert = jnp.minimum(tile_expert, last_expert)
    plan_ref[...] = jnp.zeros_like(plan_ref)
    plan_ref[PLAN_ROW_START:PLAN_ROW_START + 1, :] = row_start.astype(I32)
    plan_ref[PLAN_TILE_EXPERT:PLAN_TILE_EXPERT + 1, :] = tile_expert.astype(I32)
    plan_ref[PLAN_NUM_TILES:PLAN_NUM_TILES + 1, :] = jnp.broadcast_to(num_tiles, (1, LANES)).astype(I32)
    plan_ref[PLAN_TILE_END:PLAN_TILE_END + 1, :] = tile_end.astype(I32)
    plan_ref[PLAN_COUNT:PLAN_COUNT + 1, :] = cnt.astype(I32)

    experts = idx_ref[0:2, :]
    start_i = row_start.astype(I32)
    base = jnp.zeros(experts.shape, I32)
    for e in range(N_EXPERTS):
        base = jnp.where(experts == e, start_i[:, e:e + 1], base)
    pos_ref[...] = jnp.zeros_like(pos_ref)
    pos_ref[0:2, :] = idx_ref[2:4, :] + base


def _route_plan(idx, counts):
    t = idx.shape[1]
    return pl.pallas_call(
        _plan_kernel,
        out_shape=[jax.ShapeDtypeStruct((SUBLANES, t), I32),
                   jax.ShapeDtypeStruct((SUBLANES, LANES), I32)],
        compiler_params=pltpu.CompilerParams(vmem_limit_bytes=VMEM_LIMIT),
        name="route_plan",
    )(idx, counts)


def _dispatch_kernel(pa_ref, pb_ref, plan_ref, h2_hbm, xs_ref, inv, h2v, sem):
    i = pl.program_id(0)
    n_tok = h2_hbm.shape[0]
    nt = plan_ref[PLAN_NUM_TILES, 0]
    tiles_per_step = DSP_TM // EXP_TM
    table_copy = pltpu.make_async_copy(h2_hbm, h2v, sem)

    @pl.when(i == 0)
    def _():
        table_copy.start()

        def fill(p, _):
            inv[p] = 0
            return 0
        for e in range(N_EXPERTS):
            lax.fori_loop(plan_ref[PLAN_ROW_START, e] + plan_ref[PLAN_COUNT, e],
                          plan_ref[PLAN_TILE_END, e] * EXP_TM, fill, 0)

        def scatter(j, _):
            for u in range(DMA_UNROLL):
                tok = j * DMA_UNROLL + u
                inv[pa_ref[tok]] = tok
                inv[pb_ref[tok]] = tok
            return 0
        lax.fori_loop(0, n_tok // DMA_UNROLL, scatter, 0)
        table_copy.wait()

    for sub in range(tiles_per_step):
        tile = i * tiles_per_step + sub
        rows = pl.ds(sub * EXP_TM, EXP_TM)

        @pl.when(tile < nt)
        def _():
            base = tile * EXP_TM

            def body(j, _):
                for u in range(DSP_UNROLL):
                    r = j * DSP_UNROLL + u
                    xs_ref[sub * EXP_TM + r] = h2v[inv[base + r]]
                return 0
            lax.fori_loop(0, EXP_TM // DSP_UNROLL, body, 0)

        @pl.when(tile >= nt)
        def _():
            xs_ref[rows] = jnp.zeros((EXP_TM, ROW_TILES, LANES), BF16)


def _dispatch(h2t, pos_a, pos_b, plan):
    t = h2t.shape[0]
    n_rows = ((t * 2) // EXP_TM + N_EXPERTS) * EXP_TM
    return pl.pallas_call(
        _dispatch_kernel,
        grid_spec=pltpu.PrefetchScalarGridSpec(
            num_scalar_prefetch=3,
            grid=(n_rows // DSP_TM,),
            in_specs=[pl.BlockSpec(memory_space=pl.ANY)],
            out_specs=pl.BlockSpec((DSP_TM, ROW_TILES, LANES), lambda i, pa, pb, plan: (i, 0, 0)),
            scratch_shapes=[pltpu.SMEM((n_rows,), I32),
                            pltpu.VMEM((t, ROW_TILES, LANES), BF16),
                            pltpu.SemaphoreType.DMA(())]),
        out_shape=jax.ShapeDtypeStruct((n_rows, ROW_TILES, LANES), BF16),
        compiler_params=_cparams(("arbitrary",)),
        name="moe_dispatch",
    )(pos_a, pos_b, plan, h2t)


def _expert_kernel(plan_ref, xs_ref, w1_hbm, w3_hbm, w2_hbm, ys_ref,
                   seq, w1s, w3s, w2s, wsem, w1b, w3b, w2b, *, layer):
    i = pl.program_id(0)
    tm = EXP_TM
    nt = plan_ref[PLAN_NUM_TILES, 0]
    expert = plan_ref[PLAN_TILE_EXPERT, i]

    def weight_copies(e, buf):
        return (pltpu.make_async_copy(w1_hbm.at[layer, e], w1s.at[buf], wsem.at[buf, 0]),
                pltpu.make_async_copy(w3_hbm.at[layer, e], w3s.at[buf], wsem.at[buf, 1]),
                pltpu.make_async_copy(w2_hbm.at[layer, e], w2s.at[buf], wsem.at[buf, 2]))

    def weight_start(e, buf):
        for cp in weight_copies(e, buf):
            cp.start(priority=1)

    def next_used_tile(e):
        return plan_ref[PLAN_TILE_END, e]

    @pl.when(i == 0)
    def _():
        first_expert = plan_ref[PLAN_TILE_EXPERT, 0]
        weight_start(first_expert, 0)
        t1 = next_used_tile(first_expert)

        @pl.when(t1 < nt)
        def _():
            weight_start(plan_ref[PLAN_TILE_EXPERT, t1], 1)
        seq[0] = 0

    valid = i < nt
    first = (i == 0) | (expert != plan_ref[PLAN_TILE_EXPERT, jnp.maximum(i - 1, 0)])

    @pl.when(valid & first)
    def _():
        buf = seq[0] % 2
        for cp in weight_copies(expert, buf):
            cp.wait()
        w1b[...] = w1s[buf].astype(BF16)
        w3b[...] = w3s[buf].astype(BF16)
        w2b[...] = w2s[buf].astype(BF16)
        seq[0] = seq[0] + 1
        t1 = next_used_tile(expert)
        t2 = next_used_tile(plan_ref[PLAN_TILE_EXPERT, jnp.minimum(t1, LANES - 1)])

        @pl.when((t1 < nt) & (t2 < nt))
        def _():
            weight_start(plan_ref[PLAN_TILE_EXPERT, t2], buf)

    @pl.when(valid)
    def _():
        xt = xs_ref[...].reshape(tm, D_MODEL)
        a = jnp.dot(xt, w1b[...], preferred_element_type=F32)
        g = jnp.dot(xt, w3b[...], preferred_element_type=F32)
        act = (a * _sigmoid(a) * g).astype(BF16)
        y = jnp.dot(act, w2b[...], preferred_element_type=F32)
        ys_ref[...] = y.astype(BF16).reshape(tm, ROW_TILES, LANES)

    @pl.when(jnp.logical_not(valid))
    def _():
        ys_ref[...] = jnp.zeros_like(ys_ref)


def _expert_gemm(xs, plan, w1, w3, w2, layer):
    n_rows = xs.shape[0]
    n_tiles = n_rows // EXP_TM
    d, de = w1.shape[2], w1.shape[3]
    any_spec = pl.BlockSpec(memory_space=pl.ANY)

    def used_tile(i, plan):
        return (jnp.minimum(i, plan[PLAN_NUM_TILES, 0] - 1), 0, 0)

    return pl.pallas_call(
        functools.partial(_expert_kernel, layer=layer),
        grid_spec=pltpu.PrefetchScalarGridSpec(
            num_scalar_prefetch=1,
            grid=(n_tiles,),
            in_specs=[pl.BlockSpec((EXP_TM, ROW_TILES, LANES), used_tile), any_spec, any_spec, any_spec],
            out_specs=pl.BlockSpec((EXP_TM, ROW_TILES, LANES), lambda i, plan: (i, 0, 0)),
            scratch_shapes=[pltpu.SMEM((1,), I32),
                            pltpu.VMEM((2, d, de), F32),
                            pltpu.VMEM((2, d, de), F32),
                            pltpu.VMEM((2, de, d), F32),
                            pltpu.SemaphoreType.DMA((2, 3)),
                            pltpu.VMEM((d, de), BF16),
                            pltpu.VMEM((d, de), BF16),
                            pltpu.VMEM((de, d), BF16)]),
        out_shape=jax.ShapeDtypeStruct((n_rows, ROW_TILES, LANES), BF16),
        compiler_params=_cparams(("arbitrary",)),
        name="moe_expert_gemm",
    )(plan, xs, w1, w3, w2)


def _combine_kernel(pa_ref, pb_ref, ys_hbm, x1_ref, wts_ref, mod_ref, npost_ref, o_ref, buf_a, buf_b, sems):
    i = pl.program_id(0)
    n = pl.num_programs(0)
    tm = CMB_TM

    def start_tile(tile, slot):
        def body(j, _):
            for u in range(DMA_UNROLL):
                r = j * DMA_UNROLL + u
                tok = tile * tm + r
                pltpu.make_async_copy(ys_hbm.at[pa_ref[tok]], buf_a.at[slot, r], sems.at[0, slot]).start()
                pltpu.make_async_copy(ys_hbm.at[pb_ref[tok]], buf_b.at[slot, r], sems.at[1, slot]).start()
            return 0
        lax.fori_loop(0, tm // DMA_UNROLL, body, 0)

    def wait_tile(slot):
        pltpu.make_async_copy(ys_hbm.at[pl.ds(0, tm)], buf_a.at[slot], sems.at[0, slot]).wait()
        pltpu.make_async_copy(ys_hbm.at[pl.ds(0, tm)], buf_b.at[slot], sems.at[1, slot]).wait()

    slot = i % 2

    @pl.when(i == 0)
    def _():
        start_tile(0, 0)

    @pl.when(i + 1 < n)
    def _():
        start_tile(i + 1, 1 - slot)

    wait_tile(slot)

    wa = wts_ref[:, 0:1]
    wb = wts_ref[:, 1:2]
    ya = buf_a[slot].reshape(tm, D_MODEL).astype(F32)
    yb = buf_b[slot].reshape(tm, D_MODEL).astype(F32)
    y = wa * ya + wb * yb
    gate2 = mod_ref[0, 5:6, :]
    o_ref[...] = x1_ref[...] + gate2 * (_rms_rows(y) * npost_ref[...])


def _combine(ys, pos_a, pos_b, x1, wts, mod3, npost, s):
    t, d = x1.shape
    per_b = s // CMB_TM
    row = lambda i, pa, pb: (i, 0)
    return pl.pallas_call(
        _combine_kernel,
        grid_spec=pltpu.PrefetchScalarGridSpec(
            num_scalar_prefetch=2,
            grid=(t // CMB_TM,),
            in_specs=[pl.BlockSpec(memory_space=pl.ANY),
                      pl.BlockSpec((CMB_TM, d), row),
                      pl.BlockSpec((CMB_TM, LANES), row),
                      pl.BlockSpec((1, N_MOD, d), lambda i, pa, pb: (i // per_b, 0, 0)),
                      pl.BlockSpec((1, d), lambda i, pa, pb: (0, 0))],
            out_specs=pl.BlockSpec((CMB_TM, d), row),
            scratch_shapes=[pltpu.VMEM((2, CMB_TM, ROW_TILES, LANES), BF16),
                            pltpu.VMEM((2, CMB_TM, ROW_TILES, LANES), BF16),
                            pltpu.SemaphoreType.DMA((2, 2))]),
        out_shape=jax.ShapeDtypeStruct((t, d), F32),
        compiler_params=_cparams(("arbitrary",)),
        name="moe_combine",
    )(pos_a, pos_b, ys, x1, wts, mod3, npost.reshape(1, d))


def _rope_tables(s):
    inv_freq = ROPE_BASE ** (-jnp.arange(0, HEAD_DIM, 2, dtype=F32) / HEAD_DIM)
    angle = jnp.arange(s, dtype=F32)[:, None] * inv_freq[None, :]
    cos = jnp.cos(angle)
    sin = jnp.sin(angle)
    return jnp.concatenate([cos, cos], axis=-1), jnp.concatenate([-sin, sin], axis=-1)


def _router_weights(router_coarse_w, router_fine_w, router_coarse_b, router_fine_b):
    d = router_fine_w.shape[0]
    w = jnp.concatenate([router_fine_w, router_coarse_w], axis=1)
    n = w.shape[1]
    w_hi = w.astype(BF16)
    w_lo = (w - w_hi.astype(F32)).astype(BF16)
    pad = jnp.zeros((d, LANES // 2 - n), BF16)
    rw = jnp.concatenate([w_hi, pad, w_lo, pad], axis=1)
    rb = jnp.concatenate([router_fine_b, router_coarse_b, jnp.zeros((LANES - n,), F32)]).reshape(1, LANES)
    return rw, rb


def _layer(x2, c, nb, s, cosf, sinf, lg_tab, layer, ada_w, ada_b, norm_pre_mix, norm_post_mix, w_in, fox_f_bias,
           ret_gn_w, fox_qn_w, fox_kn_w, fox_on_w, w_out, norm_pre_ffn, norm_post_ffn,
           router_coarse_w, router_coarse_b, router_fine_w, router_fine_b, w1, w3, w2):
    t, d = x2.shape
    mod3 = _modulation(c, ada_w, ada_b, layer).reshape(nb, N_MOD, d)

    proj, lf = _in_projection(x2, mod3, norm_pre_mix[layer], w_in, fox_f_bias[layer], fox_qn_w[layer],
                              fox_kn_w[layer], cosf, sinf, nb, s, layer)
    cum = _cum_forget(lf)
    proj3 = proj.reshape(nb, s, MAIN_COLS)
    ret = _retention(proj3, ret_gn_w[layer], lg_tab)
    fox = _fox_attention(proj3, cum, fox_on_w[layer])

    rw, rb = _router_weights(router_coarse_w[layer], router_fine_w[layer], router_coarse_b[layer],
                             router_fine_b[layer])
    x1, h2t, idx, wts, counts = _out_projection(
        ret.reshape(t, RET_WIDTH), fox.reshape(t, FOX_WIDTH), _cast_bf16(w_out, layer), x2, mod3,
        norm_post_mix[layer], norm_pre_ffn[layer], rw, rb, s)

    pos, plan = _route_plan(idx, counts)
    pos_a, pos_b = pos[0], pos[1]
    xs = _dispatch(h2t, pos_a, pos_b, plan)
    ys = _expert_gemm(xs, plan, w1, w3, w2, layer)
    return _combine(ys, pos_a, pos_b, x1, wts, mod3, norm_post_ffn[layer], s)


def kernel(x, c, ada_w, ada_b, norm_pre_mix, norm_post_mix, w_in, fox_f_bias, ret_gn_w, fox_qn_w, fox_kn_w,
           fox_on_w, w_out, norm_pre_ffn, norm_post_ffn, router_coarse_w, router_coarse_b, router_fine_w,
           router_fine_b, w1, w3, w2):
    nb, s, d = x.shape
    depth = ada_w.shape[0]
    cosf, sinf = _rope_tables(s)
    log_gamma = jnp.log1p(-jnp.exp2(-5.0 - jnp.arange(RET_HEADS, dtype=F32)))
    lg_tab = jnp.broadcast_to(log_gamma[:, None, None], (RET_HEADS, 1, LANES))
    x2 = x.reshape(nb * s, d)
    for layer in range(depth):
        x2 = _layer(x2, c, nb, s, cosf, sinf, lg_tab, layer, ada_w, ada_b, norm_pre_mix, norm_post_mix,
                    w_in, fox_f_bias, ret_gn_w, fox_qn_w, fox_kn_w, fox_on_w, w_out, norm_pre_ffn,
                    norm_post_ffn, router_coarse_w, router_coarse_b, router_fine_w, router_fine_b, w1, w3, w2)
    return x2.reshape(nb, s, d)
```

```python
import functools

import jax
import jax.numpy as jnp
from jax import lax
from jax.experimental import pallas as pl
from jax.experimental.pallas import tpu as pltpu

F32 = jnp.float32
BF16 = jnp.bfloat16
I32 = jnp.int32

D_MODEL = 2048
HEAD_DIM = 128
RET_WIDTH = D_MODEL // 2
FOX_WIDTH = D_MODEL - RET_WIDTH
RET_HEADS = RET_WIDTH // HEAD_DIM
FOX_HEADS = FOX_WIDTH // HEAD_DIM
ROPE_BASE = 10000.0
N_GROUPS = 4
EXPERTS_PER_GROUP = 8
N_EXPERTS = N_GROUPS * EXPERTS_PER_GROUP
D_EXPERT = D_MODEL // 4
N_MOD = 6
EPS = 1e-6
MAIN_COLS = 4 * RET_WIDTH + 3 * FOX_WIDTH

LANES = 128
SUBLANES = 8
ROW_TILES = D_MODEL // LANES
NEG = -1e30
LOG2E = 1.4426950408889634

VMEM_LIMIT = 56 * 1024 * 1024

MOD_TN = 1024
IN_TM, IN_TN = 2048, 512
IN_RC = 256
IN_XC = 512
RET_CHUNK = 256
RET_HPS = 2
FOX_T = 512
FOX_HPS = 2
OUT_TM = 512
OUT_RC = 256
EXP_TM = 256
DSP_TM = 512
DSP_UNROLL = 8
CMB_TM = 256
DMA_UNROLL = 8


def _cparams(sem):
    return pltpu.CompilerParams(dimension_semantics=sem, vmem_limit_bytes=VMEM_LIMIT)


def _nt_dot(a, b):
    return lax.dot_general(a, b, (((1,), (1,)), ((), ())), preferred_element_type=F32)


def _tn_dot(a, b):
    return lax.dot_general(a, b, (((0,), (0,)), ((), ())), preferred_element_type=F32)


def _sigmoid(x):
    return 1.0 / (1.0 + jnp.exp(-x))


def _col_to_row(col):
    n = col.shape[0]
    ri = lax.broadcasted_iota(I32, (n, n), 0)
    ci = lax.broadcasted_iota(I32, (n, n), 1)
    return jnp.sum(jnp.where(ri == ci, col, 0.0), axis=0, keepdims=True)


def _row_to_col(row):
    n = row.shape[1]
    ri = lax.broadcasted_iota(I32, (n, n), 0)
    ci = lax.broadcasted_iota(I32, (n, n), 1)
    return jnp.sum(jnp.where(ri == ci, row, 0.0), axis=1, keepdims=True)


def _mod_kernel(cb_ref, w_ref, b_ref, o_ref):
    nb = cb_ref.shape[0]
    for b in range(nb):
        cv = cb_ref[b]
        ca = cv * _sigmoid(cv)
        for j in range(MOD_TN // LANES):
            sl = slice(j * LANES, (j + 1) * LANES)
            col = jnp.sum(w_ref[:, sl] * ca, axis=0, keepdims=True)
            o_ref[b:b + 1, sl] = col + b_ref[:, sl]


def _modulation(c, ada_w, ada_b, layer):
    nb, d = c.shape
    n = ada_w.shape[2]
    cb = jnp.broadcast_to(c[:, :, None], (nb, d, LANES))
    return pl.pallas_call(
        _mod_kernel,
        grid=(n // MOD_TN,),
        in_specs=[pl.BlockSpec((nb, d, LANES), lambda j: (0, 0, 0)),
                  pl.BlockSpec((None, d, MOD_TN), lambda j: (layer, 0, j)),
                  pl.BlockSpec((None, 1, MOD_TN), lambda j: (layer, 0, j))],
        out_specs=pl.BlockSpec((nb, MOD_TN), lambda j: (0, j)),
        out_shape=jax.ShapeDtypeStruct((nb, n), F32),
        compiler_params=_cparams(("arbitrary",)),
        name="adaln_mod",
    )(cb, ada_w, ada_b.reshape(ada_b.shape[0], 1, n))


def _rms_rows(x):
    return x * lax.rsqrt(jnp.mean(x * x, axis=-1, keepdims=True) + EPS)


def _inproj_kernel(x_hbm, mod_ref, nw_ref, w_ref, wf_ref, fb_ref, cos_ref, sin_ref, qn_ref, kn_ref,
                   o_ref, lf_ref, h_scr, wb_scr, xbuf, xsem):
    i = pl.program_id(0)
    j = pl.program_id(1)
    tiles_per_sec = RET_WIDTH // IN_TN
    heads_per_tile = IN_TN // HEAD_DIM

    @pl.when(j == 0)
    def _():
        shift = mod_ref[0, 0:1, :]
        scale = 1.0 + mod_ref[0, 1:2, :]
        nw = nw_ref[...]
        n_chunks = IN_TM // IN_XC

        def x_copy(c):
            rows = pl.ds(i * IN_TM + c * IN_XC, IN_XC)
            return pltpu.make_async_copy(x_hbm.at[rows, :], xbuf.at[c % 2], xsem.at[c % 2])

        x_copy(0).start()
        for c in range(n_chunks):
            if c + 1 < n_chunks:
                x_copy(c + 1).start()
            x_copy(c).wait()
            rows = 128
            for r in range(IN_XC // rows):
                h = _rms_rows(xbuf[c % 2, r * rows:(r + 1) * rows, :]) * nw * scale + shift
                h_scr[c * IN_XC + r * rows:c * IN_XC + (r + 1) * rows, :] = h.astype(BF16)
        z = _nt_dot(wf_ref[...], h_scr[...]) + fb_ref[...]
        lf_ref[0] = jnp.minimum(z, 0.0) - jnp.log1p(jnp.exp(-jnp.abs(z)))

    wb_scr[...] = w_ref[...].astype(BF16)
    sec = j // tiles_per_sec

    def section(epilogue):
        for r in range(IN_TM // IN_RC):
            rs = slice(r * IN_RC, (r + 1) * IN_RC)
            acc = _nt_dot(h_scr[rs, :], wb_scr[...])
            epilogue(acc, rs)

    def per_head(fn):
        def epilogue(acc, rs):
            for hh in range(heads_per_tile):
                sl = slice(hh * HEAD_DIM, (hh + 1) * HEAD_DIM)
                o_ref[rs, sl] = fn(acc[:, sl], rs).astype(BF16)
        return epilogue

    def whole(fn):
        def epilogue(acc, rs):
            o_ref[rs, :] = fn(acc).astype(BF16)
        return epilogue

    def rope(v, rs):
        return v * cos_ref[rs, :] + pltpu.roll(v, HEAD_DIM // 2, 1) * sin_ref[rs, :]

    @pl.when(sec == 0)
    def _():
        section(per_head(rope))

    @pl.when(sec == 1)
    def _():
        section(per_head(lambda v, rs: rope(v, rs) * (HEAD_DIM ** -0.5)))

    @pl.when((sec == 2) | (sec == 6))
    def _():
        section(whole(lambda a: a))

    @pl.when(sec == 3)
    def _():
        section(whole(lambda a: a * _sigmoid(a)))

    @pl.when(sec == 4)
    def _():
        section(per_head(lambda v, rs: _rms_rows(v) * qn_ref[...] * (LOG2E * HEAD_DIM ** -0.5)))

    @pl.when(sec == 5)
    def _():
        section(per_head(lambda v, rs: _rms_rows(v) * kn_ref[...]))


def _in_projection(x2, mod3, norm_w, w_in, fox_f_bias, fox_qn_w, fox_kn_w, cosf, sinf, nb, s, layer):
    t, d = x2.shape
    w_t = jnp.swapaxes(w_in, 1, 2)
    wf_t = w_t[layer, MAIN_COLS:, :].astype(BF16)
    per_b = s // IN_TM
    grid = (t // IN_TM, MAIN_COLS // IN_TN)
    return pl.pallas_call(
        _inproj_kernel,
        grid=grid,
        in_specs=[
            pl.BlockSpec(memory_space=pl.ANY),
            pl.BlockSpec((1, N_MOD, d), lambda i, j: (i // per_b, 0, 0)),
            pl.BlockSpec((1, d), lambda i, j: (0, 0)),
            pl.BlockSpec((None, IN_TN, d), lambda i, j: (layer, j, 0)),
            pl.BlockSpec((FOX_HEADS, d), lambda i, j: (0, 0)),
            pl.BlockSpec((FOX_HEADS, 1), lambda i, j: (0, 0)),
            pl.BlockSpec((IN_TM, HEAD_DIM), lambda i, j: (i % per_b, 0)),
            pl.BlockSpec((IN_TM, HEAD_DIM), lambda i, j: (i % per_b, 0)),
            pl.BlockSpec((1, HEAD_DIM), lambda i, j: (0, 0)),
            pl.BlockSpec((1, HEAD_DIM), lambda i, j: (0, 0)),
        ],
        out_specs=[
            pl.BlockSpec((IN_TM, IN_TN), lambda i, j: (i, j)),
            pl.BlockSpec((1, FOX_HEADS, IN_TM), lambda i, j: (i // per_b, 0, i % per_b)),
        ],
        out_shape=[jax.ShapeDtypeStruct((t, MAIN_COLS), BF16),
                   jax.ShapeDtypeStruct((nb, FOX_HEADS, s), F32)],
        scratch_shapes=[pltpu.VMEM((IN_TM, d), BF16), pltpu.VMEM((IN_TN, d), BF16),
                        pltpu.VMEM((2, IN_XC, d), F32), pltpu.SemaphoreType.DMA((2,))],
        compiler_params=_cparams(("arbitrary", "arbitrary")),
        name="in_projection",
    )(x2, mod3, norm_w.reshape(1, d), w_t, wf_t, fox_f_bias.reshape(FOX_HEADS, 1), cosf, sinf,
      fox_qn_w.reshape(1, HEAD_DIM), fox_kn_w.reshape(1, HEAD_DIM))


def _split3(x):
    hi = x.astype(BF16)
    r1 = x - hi.astype(F32)
    mid = r1.astype(BF16)
    lo = (r1 - mid.astype(F32)).astype(BF16)
    return hi, mid, lo


def _cumsum_kernel(x_ref, o_ref, *, rows_per_seq):
    x = x_ref[...]
    r = x.shape[0]
    ri = lax.broadcasted_iota(I32, (LANES, LANES), 0)
    ci = lax.broadcasted_iota(I32, (LANES, LANES), 1)
    upper = jnp.where(ri <= ci, 1.0, 0.0).astype(BF16)
    rowcum = sum(jnp.dot(p, upper, preferred_element_type=F32) for p in _split3(x))
    tot = jnp.broadcast_to(rowcum[:, LANES - 1:LANES], (r, LANES))
    gi = lax.broadcasted_iota(I32, (r, r), 0)
    gj = lax.broadcasted_iota(I32, (r, r), 1)
    same_seq = (gi // rows_per_seq) == (gj // rows_per_seq)
    lower = jnp.where(same_seq & (gj < gi), 1.0, 0.0).astype(BF16)
    prefix = sum(jnp.dot(lower, p, preferred_element_type=F32) for p in _split3(tot))
    o_ref[...] = rowcum + prefix


def _cum_forget(lf):
    nb, nh, s = lf.shape
    rows = nb * nh * s // LANES
    out = pl.pallas_call(
        functools.partial(_cumsum_kernel, rows_per_seq=s // LANES),
        out_shape=jax.ShapeDtypeStruct((rows, LANES), F32),
        compiler_params=pltpu.CompilerParams(vmem_limit_bytes=VMEM_LIMIT),
        name="forget_cumsum",
    )(lf.reshape(rows, LANES))
    return out.reshape(nb * nh, 1, s)


def _ret_kernel(lg_ref, q_ref, k_ref, v_ref, g_ref, gnw_ref, o_ref):
    c = RET_CHUNK
    s = q_ref.shape[1]
    ri = lax.broadcasted_iota(I32, (c, c), 0)
    ci = lax.broadcasted_iota(I32, (c, c), 1)
    diff = (ri - ci).astype(F32)
    pos = lax.broadcasted_iota(I32, (c, 1), 0).astype(F32)
    consts = []
    for hh in range(RET_HPS):
        lg = lg_ref[hh][:, 0:1]
        decay = jnp.where(diff >= 0.0, jnp.exp(lg * jnp.maximum(diff, 0.0)), 0.0)
        zeta = jnp.exp(lg * (float(c - 1) - pos))
        xi = jnp.exp(lg * (pos + 1.0))
        chunk_decay = jnp.exp(lg * float(c))
        consts.append((decay, zeta, xi, chunk_decay))

    def body(n, states):
        r0 = pl.multiple_of(n * c, c)
        new_states = []
        for hh in range(RET_HPS):
            decay, zeta, xi, chunk_decay = consts[hh]
            sl = slice(hh * HEAD_DIM, (hh + 1) * HEAD_DIM)
            q = q_ref[0, pl.ds(r0, c), sl]
            k = k_ref[0, pl.ds(r0, c), sl]
            v = v_ref[0, pl.ds(r0, c), sl]
            scores = _nt_dot(q, k) * decay
            intra = jnp.dot(scores.astype(BF16), v, preferred_element_type=F32)
            cross = jnp.dot(q, states[hh].astype(BF16), preferred_element_type=F32) * xi
            kz = (k.astype(F32) * zeta).astype(BF16)
            kv = _tn_dot(kz, v)
            o = intra + cross
            mu = jnp.mean(o, axis=-1, keepdims=True)
            oc = o - mu
            var = jnp.mean(oc * oc, axis=-1, keepdims=True)
            y = oc * lax.rsqrt(var + EPS) * gnw_ref[:, sl] * g_ref[0, pl.ds(r0, c), sl].astype(F32)
            o_ref[0, pl.ds(r0, c), sl] = y.astype(BF16)
            new_states.append(states[hh] * chunk_decay + kv)
        return tuple(new_states)

    init = tuple(jnp.zeros((HEAD_DIM, HEAD_DIM), F32) for _ in range(RET_HPS))
    lax.fori_loop(0, s // c, body, init)


def _retention(proj3, ret_gn_w, lg_tab):
    nb, s, _ = proj3.shape
    w = RET_HPS * HEAD_DIM
    gpg = RET_WIDTH // w

    def col(sec):
        return pl.BlockSpec((1, s, w), lambda b, h: (b, 0, sec * gpg + h))

    return pl.pallas_call(
        _ret_kernel,
        grid=(nb, gpg),
        in_specs=[pl.BlockSpec((RET_HPS, 1, LANES), lambda b, h: (h, 0, 0)),
                  col(0), col(1), col(2), col(3),
                  pl.BlockSpec((1, w), lambda b, h: (0, h))],
        out_specs=pl.BlockSpec((1, s, w), lambda b, h: (b, 0, h)),
        out_shape=jax.ShapeDtypeStruct((nb, s, RET_WIDTH), BF16),
        compiler_params=_cparams(("arbitrary", "arbitrary")),
        name="retention",
    )(lg_tab, proj3, proj3, proj3, proj3, ret_gn_w.reshape(1, RET_WIDTH))


def _fox_kernel(q_ref, k_ref, v_ref, cum_ref, onw_ref, o_ref):
    t = FOX_T
    qi = pl.program_id(2)
    ri = lax.broadcasted_iota(I32, (t, t), 0)
    ci = lax.broadcasted_iota(I32, (t, t), 1)

    def step(ki, carry, masked):
        k0 = pl.multiple_of(ki * t, t)
        out = []
        for hh in range(FOX_HPS):
            m, l, acc = carry[hh]
            sl = slice(hh * HEAD_DIM, (hh + 1) * HEAD_DIM)
            k = k_ref[0, pl.ds(k0, t), sl]
            v = v_ref[0, pl.ds(k0, t), sl]
            sc = _nt_dot(q_ref[0, :, sl], k) - cum_ref[hh, :, pl.ds(k0, t)] * LOG2E
            if masked:
                sc = jnp.where(ci <= ri, sc, NEG)
            m_new = jnp.maximum(m, jnp.max(sc, axis=1, keepdims=True))
            p = jnp.exp2(sc - m_new)
            alpha = jnp.exp2(m - m_new)
            l = alpha * l + jnp.sum(p, axis=1, keepdims=True)
            acc = alpha * acc + jnp.dot(p.astype(BF16), v, preferred_element_type=F32)
            out.append((m_new, l, acc))
        return tuple(out)

    init = tuple((jnp.full((t, 1), NEG, F32), jnp.zeros((t, 1), F32), jnp.zeros((t, HEAD_DIM), F32))
                 for _ in range(FOX_HPS))
    carry = lax.fori_loop(0, qi, lambda ki, cr: step(ki, cr, False), init)
    carry = step(qi, carry, True)
    for hh in range(FOX_HPS):
        _, l, acc = carry[hh]
        sl = slice(hh * HEAD_DIM, (hh + 1) * HEAD_DIM)
        o = acc / l
        o_ref[0, :, sl] = (_rms_rows(o) * onw_ref[:, sl]).astype(BF16)


def _fox_attention(proj3, cum, fox_on_w):
    nb, s, _ = proj3.shape
    w = FOX_HPS * HEAD_DIM
    base = 4 * RET_WIDTH // w
    gpg = FOX_WIDTH // w
    return pl.pallas_call(
        _fox_kernel,
        grid=(nb, gpg, s // FOX_T),
        in_specs=[pl.BlockSpec((1, FOX_T, w), lambda b, h, i: (b, i, base + h)),
                  pl.BlockSpec((1, s, w), lambda b, h, i: (b, 0, base + gpg + h)),
                  pl.BlockSpec((1, s, w), lambda b, h, i: (b, 0, base + 2 * gpg + h)),
                  pl.BlockSpec((FOX_HPS, 1, s), lambda b, h, i: (b * gpg + h, 0, 0)),
                  pl.BlockSpec((1, w), lambda b, h, i: (0, h))],
        out_specs=pl.BlockSpec((1, FOX_T, w), lambda b, h, i: (b, i, h)),
        out_shape=jax.ShapeDtypeStruct((nb, s, FOX_WIDTH), BF16),
        compiler_params=_cparams(("arbitrary", "arbitrary", "arbitrary")),
        name="fox_attention",
    )(proj3, proj3, proj3, cum, fox_on_w.reshape(1, FOX_WIDTH))


def _outproj_kernel(ret_ref, fox_ref, wo_ref, x_ref, mod_ref, npost_ref, npre_ref, rw_ref, rb_ref,
                    x1_ref, h2_ref, idx_ref, wts_ref, cnt_ref, carry_scr):
    i = pl.program_id(0)

    @pl.when(i == 0)
    def _():
        carry_scr[...] = jnp.zeros_like(carry_scr)

    idx_ref[...] = jnp.zeros_like(idx_ref)
    subtiles = [slice(r * OUT_RC, (r + 1) * OUT_RC) for r in range(OUT_TM // OUT_RC)]
    ys = [jnp.dot(ret_ref[rs, :], wo_ref[:RET_WIDTH, :], preferred_element_type=F32)
          + jnp.dot(fox_ref[rs, :], wo_ref[RET_WIDTH:, :], preferred_element_type=F32) for rs in subtiles]
    for rs, y in zip(subtiles, ys):
        _outproj_rows(rs, y, x_ref, mod_ref, npost_ref, npre_ref, rw_ref, rb_ref, x1_ref, h2_ref, idx_ref,
                      wts_ref, cnt_ref, carry_scr)


def _outproj_rows(rs, y, x_ref, mod_ref, npost_ref, npre_ref, rw_ref, rb_ref,
                  x1_ref, h2_ref, idx_ref, wts_ref, cnt_ref, carry_scr):
    tm = OUT_RC
    gate1 = mod_ref[0, 2:3, :]
    shift2 = mod_ref[0, 3:4, :]
    scale2 = 1.0 + mod_ref[0, 4:5, :]
    x1 = x_ref[rs, :] + gate1 * (_rms_rows(y) * npost_ref[...])
    x1_ref[rs, :] = x1
    h2 = _rms_rows(x1) * npre_ref[...] * scale2 + shift2
    h2_ref[rs, :, :] = h2.astype(BF16).reshape(tm, ROW_TILES, LANES)

    hi = h2.astype(BF16)
    lo = (h2 - hi.astype(F32)).astype(BF16)
    rw = rw_ref[...]
    lg2 = jnp.dot(hi, rw, preferred_element_type=F32) + jnp.dot(lo, rw, preferred_element_type=F32)
    logits = lg2 + pltpu.roll(lg2, LANES // 2, 1) + rb_ref[...]

    lane = lax.broadcasted_iota(I32, (tm, LANES), 1)
    lanef = lane.astype(F32)
    big = float(LANES)

    def first_argmax(vals):
        vmax = jnp.max(vals, axis=1, keepdims=True)
        idx = jnp.min(jnp.where(vals == vmax, lanef, big), axis=1, keepdims=True)
        return vmax, idx

    is_coarse = (lane >= N_EXPERTS) & (lane < N_EXPERTS + N_GROUPS)
    cvals = jnp.where(is_coarse, logits, NEG)
    cmax, cidx = first_argmax(cvals)
    gidx = cidx - float(N_EXPERTS)
    gprob = 1.0 / jnp.sum(jnp.where(is_coarse, jnp.exp(cvals - cmax), 0.0), axis=1, keepdims=True)
    glo = gidx * float(EXPERTS_PER_GROUP)
    in_group = (lanef >= glo) & (lanef < glo + float(EXPERTS_PER_GROUP))
    fvals = jnp.where(in_group, logits, NEG)
    v1, e1 = first_argmax(fvals)
    fvals2 = jnp.where(lanef == e1, NEG, fvals)
    v2, e2 = first_argmax(fvals2)
    ex = jnp.exp(v2 - v1)
    w1 = gprob * (1.0 / (1.0 + ex))
    w2 = gprob * (ex / (1.0 + ex))

    sel1 = lanef == e1
    sel2 = lanef == e2
    onehot = jnp.where(sel1 | sel2, 1.0, 0.0)
    ri = lax.broadcasted_iota(I32, (tm, tm), 0)
    ci = lax.broadcasted_iota(I32, (tm, tm), 1)
    lower = jnp.where(ci < ri, 1.0, 0.0).astype(BF16)
    before = jnp.dot(lower, onehot.astype(BF16), preferred_element_type=F32) + carry_scr[...]
    r1 = jnp.sum(jnp.where(sel1, before, 0.0), axis=1, keepdims=True)
    r2 = jnp.sum(jnp.where(sel2, before, 0.0), axis=1, keepdims=True)
    carry_new = carry_scr[...] + jnp.sum(onehot, axis=0, keepdims=True)
    carry_scr[...] = carry_new
    cnt_ref[...] = carry_new.astype(I32)

    for row, col in enumerate((e1, e2, r1, r2)):
        idx_ref[row:row + 1, rs] = _col_to_row(col).astype(I32)
    wts_ref[rs, :] = jnp.where(lane == 0, w1, jnp.where(lane == 1, w2, 0.0))


def _out_projection(ret2, fox2, w_out_bf, x2, mod3, npost, npre, rw, rb, s):
    t, d = x2.shape
    per_b = s // OUT_TM
    row = lambda i: (i, 0)
    const = lambda i: (0, 0)
    return pl.pallas_call(
        _outproj_kernel,
        grid=(t // OUT_TM,),
        in_specs=[pl.BlockSpec((OUT_TM, RET_WIDTH), row),
                  pl.BlockSpec((OUT_TM, FOX_WIDTH), row),
                  pl.BlockSpec((d, d), const),
                  pl.BlockSpec((OUT_TM, d), row),
                  pl.BlockSpec((1, N_MOD, d), lambda i: (i // per_b, 0, 0)),
                  pl.BlockSpec((1, d), const),
                  pl.BlockSpec((1, d), const),
                  pl.BlockSpec((d, LANES), const),
                  pl.BlockSpec((1, LANES), const)],
        out_specs=[pl.BlockSpec((OUT_TM, d), row),
                   pl.BlockSpec((OUT_TM, ROW_TILES, LANES), lambda i: (i, 0, 0)),
                   pl.BlockSpec((SUBLANES, OUT_TM), lambda i: (0, i)),
                   pl.BlockSpec((OUT_TM, LANES), row),
                   pl.BlockSpec((1, LANES), const)],
        out_shape=[jax.ShapeDtypeStruct((t, d), F32),
                   jax.ShapeDtypeStruct((t, ROW_TILES, LANES), BF16),
                   jax.ShapeDtypeStruct((SUBLANES, t), I32),
                   jax.ShapeDtypeStruct((t, LANES), F32),
                   jax.ShapeDtypeStruct((1, LANES), I32)],
        scratch_shapes=[pltpu.VMEM((1, LANES), F32)],
        compiler_params=_cparams(("arbitrary",)),
        name="out_projection_router",
    )(ret2, fox2, w_out_bf, x2, mod3, npost.reshape(1, d), npre.reshape(1, d), rw, rb)


def _cast_kernel(w_ref, o_ref):
    o_ref[...] = w_ref[...].astype(BF16)


def _cast_bf16(w, layer, tr=256):
    _, r, c = w.shape
    return pl.pallas_call(
        _cast_kernel,
        grid=(r // tr,),
        in_specs=[pl.BlockSpec((None, tr, c), lambda i: (layer, i, 0))],
        out_specs=pl.BlockSpec((tr, c), lambda i: (i, 0)),
        out_shape=jax.ShapeDtypeStruct((r, c), BF16),
        compiler_params=_cparams(("arbitrary",)),
        name="cast_bf16",
    )(w)


PLAN_ROW_START = 0
PLAN_TILE_EXPERT = 1
PLAN_NUM_TILES = 2
PLAN_TILE_END = 3
PLAN_COUNT = 4


def _plan_kernel(idx_ref, cnt_ref, pos_ref, plan_ref):
    lane = lax.broadcasted_iota(I32, (1, LANES), 1)
    cnt = jnp.where(lane < N_EXPERTS, cnt_ref[...], 0).astype(F32)
    tiles = jnp.floor((cnt + float(EXP_TM - 1)) * (1.0 / EXP_TM))
    ri = lax.broadcasted_iota(I32, (LANES, LANES), 0)
    ci = lax.broadcasted_iota(I32, (LANES, LANES), 1)
    upper = jnp.where(ri <= ci, 1.0, 0.0).astype(BF16)
    tiles8 = jnp.broadcast_to(tiles, (SUBLANES, LANES)).astype(BF16)
    tile_end = jnp.dot(tiles8, upper, preferred_element_type=F32)[0:1, :]
    row_start = (tile_end - tiles) * float(EXP_TM)
    num_tiles = jnp.max(tile_end, axis=1, keepdims=True)
    end_col = _row_to_col(tile_end)
    owned = jnp.where((end_col <= ci.astype(F32)) & (ri < N_EXPERTS), 1.0, 0.0)
    tile_expert = jnp.sum(owned, axis=0, keepdims=True)
    last_expert = jnp.max(jnp.where(lane.astype(F32) < num_tiles, tile_expert, 0.0), axis=1, keepdims=True)
    tile_expert = jnp.minimum(tile_expert, last_expert)
    plan_ref[...] = jnp.zeros_like(plan_ref)
    plan_ref[PLAN_ROW_START:PLAN_ROW_START + 1, :] = row_start.astype(I32)
    plan_ref[PLAN_TILE_EXPERT:PLAN_TILE_EXPERT + 1, :] = tile_expert.astype(I32)
    plan_ref[PLAN_NUM_TILES:PLAN_NUM_TILES + 1, :] = jnp.broadcast_to(num_tiles, (1, LANES)).astype(I32)
    plan_ref[PLAN_TILE_END:PLAN_TILE_END + 1, :] = tile_end.astype(I32)
    plan_ref[PLAN_COUNT:PLAN_COUNT + 1, :] = cnt.astype(I32)

    experts = idx_ref[0:2, :]
    start_i = row_start.astype(I32)
    base = jnp.zeros(experts.shape, I32)
    for e in range(N_EXPERTS):
        base = jnp.where(experts == e, start_i[:, e:e + 1], base)
    pos_ref[...] = jnp.zeros_like(pos_ref)
    pos_ref[0:2, :] = idx_ref[2:4, :] + base


def _route_plan(idx, counts):
    t = idx.shape[1]
    return pl.pallas_call(
        _plan_kernel,
        out_shape=[jax.ShapeDtypeStruct((SUBLANES, t), I32),
                   jax.ShapeDtypeStruct((SUBLANES, LANES), I32)],
        compiler_params=pltpu.CompilerParams(vmem_limit_bytes=VMEM_LIMIT),
        name="route_plan",
    )(idx, counts)


def _dispatch_kernel(pa_ref, pb_ref, plan_ref, h2_hbm, xs_ref, inv, h2v, sem):
    i = pl.program_id(0)
    n_tok = h2_hbm.shape[0]
    nt = plan_ref[PLAN_NUM_TILES, 0]
    tiles_per_step = DSP_TM // EXP_TM
    table_copy = pltpu.make_async_copy(h2_hbm, h2v, sem)

    @pl.when(i == 0)
    def _():
        table_copy.start()

        def fill(p, _):
            inv[p] = 0
            return 0
        for e in range(N_EXPERTS):
            lax.fori_loop(plan_ref[PLAN_ROW_START, e] + plan_ref[PLAN_COUNT, e],
                          plan_ref[PLAN_TILE_END, e] * EXP_TM, fill, 0)

        def scatter(j, _):
            for u in range(DMA_UNROLL):
                tok = j * DMA_UNROLL + u
                inv[pa_ref[tok]] = tok
                inv[pb_ref[tok]] = tok
            return 0
        lax.fori_loop(0, n_tok // DMA_UNROLL, scatter, 0)
        table_copy.wait()

    for sub in range(tiles_per_step):
        tile = i * tiles_per_step + sub
        rows = pl.ds(sub * EXP_TM, EXP_TM)

        @pl.when(tile < nt)
        def _():
            base = tile * EXP_TM

            def body(j, _):
                for u in range(DSP_UNROLL):
                    r = j * DSP_UNROLL + u
                    xs_ref[sub * EXP_TM + r] = h2v[inv[base + r]]
                return 0
            lax.fori_loop(0, EXP_TM // DSP_UNROLL, body, 0)

        @pl.when(tile >= nt)
        def _():
            xs_ref[rows] = jnp.zeros((EXP_TM, ROW_TILES, LANES), BF16)


def _dispatch(h2t, pos_a, pos_b, plan):
    t = h2t.shape[0]
    n_rows = ((t * 2) // EXP_TM + N_EXPERTS) * EXP_TM
    return pl.pallas_call(
        _dispatch_kernel,
        grid_spec=pltpu.PrefetchScalarGridSpec(
            num_scalar_prefetch=3,
            grid=(n_rows // DSP_TM,),
            in_specs=[pl.BlockSpec(memory_space=pl.ANY)],
            out_specs=pl.BlockSpec((DSP_TM, ROW_TILES, LANES), lambda i, pa, pb, plan: (i, 0, 0)),
            scratch_shapes=[pltpu.SMEM((n_rows,), I32),
                            pltpu.VMEM((t, ROW_TILES, LANES), BF16),
                            pltpu.SemaphoreType.DMA(())]),
        out_shape=jax.ShapeDtypeStruct((n_rows, ROW_TILES, LANES), BF16),
        compiler_params=_cparams(("arbitrary",)),
        name="moe_dispatch",
    )(pos_a, pos_b, plan, h2t)


def _expert_kernel(plan_ref, xs_ref, w1_hbm, w3_hbm, w2_hbm, ys_ref,
                   seq, w1s, w3s, w2s, wsem, w1b, w3b, w2b, *, layer):
    i = pl.program_id(0)
    tm = EXP_TM
    nt = plan_ref[PLAN_NUM_TILES, 0]
    expert = plan_ref[PLAN_TILE_EXPERT, i]

    def weight_copies(e, buf):
        return (pltpu.make_async_copy(w1_hbm.at[layer, e], w1s.at[buf], wsem.at[buf, 0]),
                pltpu.make_async_copy(w3_hbm.at[layer, e], w3s.at[buf], wsem.at[buf, 1]),
                pltpu.make_async_copy(w2_hbm.at[layer, e], w2s.at[buf], wsem.at[buf, 2]))

    def weight_start(e, buf):
        for cp in weight_copies(e, buf):
            cp.start(priority=1)

    def next_used_tile(e):
        return plan_ref[PLAN_TILE_END, e]

    @pl.when(i == 0)
    def _():
        first_expert = plan_ref[PLAN_TILE_EXPERT, 0]
        weight_start(first_expert, 0)
        t1 = next_used_tile(first_expert)

        @pl.when(t1 < nt)
        def _():
            weight_start(plan_ref[PLAN_TILE_EXPERT, t1], 1)
        seq[0] = 0

    valid = i < nt
    first = (i == 0) | (expert != plan_ref[PLAN_TILE_EXPERT, jnp.maximum(i - 1, 0)])

    @pl.when(valid & first)
    def _():
        buf = seq[0] % 2
        for cp in weight_copies(expert, buf):
            cp.wait()
        w1b[...] = w1s[buf].astype(BF16)
        w3b[...] = w3s[buf].astype(BF16)
        w2b[...] = w2s[buf].astype(BF16)
        seq[0] = seq[0] + 1
        t1 = next_used_tile(expert)
        t2 = next_used_tile(plan_ref[PLAN_TILE_EXPERT, jnp.minimum(t1, LANES - 1)])

        @pl.when((t1 < nt) & (t2 < nt))
        def _():
            weight_start(plan_ref[PLAN_TILE_EXPERT, t2], buf)

    @pl.when(valid)
    def _():
        xt = xs_ref[...].reshape(tm, D_MODEL)
        a = jnp.dot(xt, w1b[...], preferred_element_type=F32)
        g = jnp.dot(xt, w3b[...], preferred_element_type=F32)
        act = (a * _sigmoid(a) * g).astype(BF16)
        y = jnp.dot(act, w2b[...], preferred_element_type=F32)
        ys_ref[...] = y.astype(BF16).reshape(tm, ROW_TILES, LANES)

    @pl.when(jnp.logical_not(valid))
    def _():
        ys_ref[...] = jnp.zeros_like(ys_ref)


def _expert_gemm(xs, plan, w1, w3, w2, layer):
    n_rows = xs.shape[0]
    n_tiles = n_rows // EXP_TM
    d, de = w1.shape[2], w1.shape[3]
    any_spec = pl.BlockSpec(memory_space=pl.ANY)

    def used_tile(i, plan):
        return (jnp.minimum(i, plan[PLAN_NUM_TILES, 0] - 1), 0, 0)

    return pl.pallas_call(
        functools.partial(_expert_kernel, layer=layer),
        grid_spec=pltpu.PrefetchScalarGridSpec(
            num_scalar_prefetch=1,
            grid=(n_tiles,),
            in_specs=[pl.BlockSpec((EXP_TM, ROW_TILES, LANES), used_tile), any_spec, any_spec, any_spec],
            out_specs=pl.BlockSpec((EXP_TM, ROW_TILES, LANES), lambda i, plan: (i, 0, 0)),
            scratch_shapes=[pltpu.SMEM((1,), I32),
                            pltpu.VMEM((2, d, de), F32),
                            pltpu.VMEM((2, d, de), F32),
                            pltpu.VMEM((2, de, d), F32),
                            pltpu.SemaphoreType.DMA((2, 3)),
                            pltpu.VMEM((d, de), BF16),
                            pltpu.VMEM((d, de), BF16),
                            pltpu.VMEM((de, d), BF16)]),
        out_shape=jax.ShapeDtypeStruct((n_rows, ROW_TILES, LANES), BF16),
        compiler_params=_cparams(("arbitrary",)),
        name="moe_expert_gemm",
    )(plan, xs, w1, w3, w2)


def _combine_kernel(pa_ref, pb_ref, ys_hbm, x1_ref, wts_ref, mod_ref, npost_ref, o_ref, buf_a, buf_b, sems):
    i = pl.program_id(0)
    n = pl.num_programs(0)
    tm = CMB_TM

    def start_tile(tile, slot):
        def body(j, _):
            for u in range(DMA_UNROLL):
                r = j * DMA_UNROLL + u
                tok = tile * tm + r
                pltpu.make_async_copy(ys_hbm.at[pa_ref[tok]], buf_a.at[slot, r], sems.at[0, slot]).start()
                pltpu.make_async_copy(ys_hbm.at[pb_ref[tok]], buf_b.at[slot, r], sems.at[1, slot]).start()
            return 0
        lax.fori_loop(0, tm // DMA_UNROLL, body, 0)

    def wait_tile(slot):
        pltpu.make_async_copy(ys_hbm.at[pl.ds(0, tm)], buf_a.at[slot], sems.at[0, slot]).wait()
        pltpu.make_async_copy(ys_hbm.at[pl.ds(0, tm)], buf_b.at[slot], sems.at[1, slot]).wait()

    slot = i % 2

    @pl.when(i == 0)
    def _():
        start_tile(0, 0)

    @pl.when(i + 1 < n)
    def _():
        start_tile(i + 1, 1 - slot)

    wait_tile(slot)

    wa = wts_ref[:, 0:1]
    wb = wts_ref[:, 1:2]
    ya = buf_a[slot].reshape(tm, D_MODEL).astype(F32)
    yb = buf_b[slot].reshape(tm, D_MODEL).astype(F32)
    y = wa * ya + wb * yb
    gate2 = mod_ref[0, 5:6, :]
    o_ref[...] = x1_ref[...] + gate2 * (_rms_rows(y) * npost_ref[...])


def _combine(ys, pos_a, pos_b, x1, wts, mod3, npost, s):
    t, d = x1.shape
    per_b = s // CMB_TM
    row = lambda i, pa, pb: (i, 0)
    return pl.pallas_call(
        _combine_kernel,
        grid_spec=pltpu.PrefetchScalarGridSpec(
            num_scalar_prefetch=2,
            grid=(t // CMB_TM,),
            in_specs=[pl.BlockSpec(memory_space=pl.ANY),
                      pl.BlockSpec((CMB_TM, d), row),
                      pl.BlockSpec((CMB_TM, LANES), row),
                      pl.BlockSpec((1, N_MOD, d), lambda i, pa, pb: (i // per_b, 0, 0)),
                      pl.BlockSpec((1, d), lambda i, pa, pb: (0, 0))],
            out_specs=pl.BlockSpec((CMB_TM, d), row),
            scratch_shapes=[pltpu.VMEM((2, CMB_TM, ROW_TILES, LANES), BF16),
                            pltpu.VMEM((2, CMB_TM, ROW_TILES, LANES), BF16),
                            pltpu.SemaphoreType.DMA((2, 2))]),
        out_shape=jax.ShapeDtypeStruct((t, d), F32),
        compiler_params=_cparams(("arbitrary",)),
        name="moe_combine",
    )(pos_a, pos_b, ys, x1, wts, mod3, npost.reshape(1, d))


def _rope_tables(s):
    inv_freq = ROPE_BASE ** (-jnp.arange(0, HEAD_DIM, 2, dtype=F32) / HEAD_DIM)
    angle = jnp.arange(s, dtype=F32)[:, None] * inv_freq[None, :]
    cos = jnp.cos(angle)
    sin = jnp.sin(angle)
    return jnp.concatenate([cos, cos], axis=-1), jnp.concatenate([-sin, sin], axis=-1)


def _router_weights(router_coarse_w, router_fine_w, router_coarse_b, router_fine_b):
    d = router_fine_w.shape[0]
    w = jnp.concatenate([router_fine_w, router_coarse_w], axis=1)
    n = w.shape[1]
    w_hi = w.astype(BF16)
    w_lo = (w - w_hi.astype(F32)).astype(BF16)
    pad = jnp.zeros((d, LANES // 2 - n), BF16)
    rw = jnp.concatenate([w_hi, pad, w_lo, pad], axis=1)
    rb = jnp.concatenate([router_fine_b, router_coarse_b, jnp.zeros((LANES - n,), F32)]).reshape(1, LANES)
    return rw, rb


def _layer(x2, c, nb, s, cosf, sinf, lg_tab, layer, ada_w, ada_b, norm_pre_mix, norm_post_mix, w_in, fox_f_bias,
           ret_gn_w, fox_qn_w, fox_kn_w, fox_on_w, w_out, norm_pre_ffn, norm_post_ffn,
           router_coarse_w, router_coarse_b, router_fine_w, router_fine_b, w1, w3, w2):
    t, d = x2.shape
    mod3 = _modulation(c, ada_w, ada_b, layer).reshape(nb, N_MOD, d)

    proj, lf = _in_projection(x2, mod3, norm_pre_mix[layer], w_in, fox_f_bias[layer], fox_qn_w[layer],
                              fox_kn_w[layer], cosf, sinf, nb, s, layer)
    cum = _cum_forget(lf)
    proj3 = proj.reshape(nb, s, MAIN_COLS)
    ret = _retention(proj3, ret_gn_w[layer], lg_tab)
    fox = _fox_attention(proj3, cum, fox_on_w[layer])

    rw, rb = _router_weights(router_coarse_w[layer], router_fine_w[layer], router_coarse_b[layer],
                             router_fine_b[layer])
    x1, h2t, idx, wts, counts = _out_projection(
        ret.reshape(t, RET_WIDTH), fox.reshape(t, FOX_WIDTH), _cast_bf16(w_out, layer), x2, mod3,
        norm_post_mix[layer], norm_pre_ffn[layer], rw, rb, s)

    pos, plan = _route_plan(idx, counts)
    pos_a, pos_b = pos[0], pos[1]
    xs = _dispatch(h2t, pos_a, pos_b, plan)
    ys = _expert_gemm(xs, plan, w1, w3, w2, layer)
    return _combine(ys, pos_a, pos_b, x1, wts, mod3, norm_post_ffn[layer], s)


def kernel(x, c, ada_w, ada_b, norm_pre_mix, norm_post_mix, w_in, fox_f_bias, ret_gn_w, fox_qn_w, fox_kn_w,
           fox_on_w, w_out, norm_pre_ffn, norm_post_ffn, router_coarse_w, router_coarse_b, router_fine_w,
           router_fine_b, w1, w3, w2):
    nb, s, d = x.shape
    depth = ada_w.shape[0]
    cosf, sinf = _rope_tables(s)
    log_gamma = jnp.log1p(-jnp.exp2(-5.0 - jnp.arange(RET_HEADS, dtype=F32)))
    lg_tab = jnp.broadcast_to(log_gamma[:, None, None], (RET_HEADS, 1, LANES))
    x2 = x.reshape(nb * s, d)
    for layer in range(depth):
        x2 = _layer(x2, c, nb, s, cosf, sinf, lg_tab, layer, ada_w, ada_b, norm_pre_mix, norm_post_mix,
                    w_in, fox_f_bias, ret_gn_w, fox_qn_w, fox_kn_w, fox_on_w, w_out, norm_pre_ffn,
                    norm_post_ffn, router_coarse_w, router_coarse_b, router_fine_w, router_fine_b, w1, w3, w2)
    return x2.reshape(nb, s, d)
```

```python
import functools

import jax
import jax.numpy as jnp
from jax import lax
from jax.experimental import pallas as pl
from jax.experimental.pallas import tpu as pltpu

F32 = jnp.float32
BF16 = jnp.bfloat16
I32 = jnp.int32

D_MODEL = 2048
HEAD_DIM = 128
RET_WIDTH = D_MODEL // 2
FOX_WIDTH = D_MODEL - RET_WIDTH
RET_HEADS = RET_WIDTH // HEAD_DIM
FOX_HEADS = FOX_WIDTH // HEAD_DIM
ROPE_BASE = 10000.0
N_GROUPS = 4
EXPERTS_PER_GROUP = 8
N_EXPERTS = N_GROUPS * EXPERTS_PER_GROUP
D_EXPERT = D_MODEL // 4
N_MOD = 6
EPS = 1e-6
MAIN_COLS = 4 * RET_WIDTH + 3 * FOX_WIDTH

LANES = 128
SUBLANES = 8
ROW_TILES = D_MODEL // LANES
NEG = -1e30
LOG2E = 1.4426950408889634

VMEM_LIMIT = 56 * 1024 * 1024

MOD_TN = 1024
IN_TM, IN_TN = 2048, 512
IN_RC = 256
IN_XC = 512
RET_CHUNK = 512
RET_HPS = 2
FOX_TQ, FOX_TK = 512, 512
FOX_HPS = 2
OUT_TM = 512
OUT_RC = 256
EXP_TM = 256
DSP_TM = 512
DSP_UNROLL = 8
CMB_TM = 256
DMA_UNROLL = 8


def _cparams(sem):
    return pltpu.CompilerParams(dimension_semantics=sem, vmem_limit_bytes=VMEM_LIMIT)


def _nt_dot(a, b):
    return lax.dot_general(a, b, (((1,), (1,)), ((), ())), preferred_element_type=F32)


def _tn_dot(a, b):
    return lax.dot_general(a, b, (((0,), (0,)), ((), ())), preferred_element_type=F32)


def _sigmoid(x):
    return 1.0 / (1.0 + jnp.exp(-x))


def _col_to_row(col):
    n = col.shape[0]
    ri = lax.broadcasted_iota(I32, (n, n), 0)
    ci = lax.broadcasted_iota(I32, (n, n), 1)
    return jnp.sum(jnp.where(ri == ci, col, 0.0), axis=0, keepdims=True)


def _row_to_col(row):
    n = row.shape[1]
    ri = lax.broadcasted_iota(I32, (n, n), 0)
    ci = lax.broadcasted_iota(I32, (n, n), 1)
    return jnp.sum(jnp.where(ri == ci, row, 0.0), axis=1, keepdims=True)


def _mod_kernel(cb_ref, w_ref, b_ref, o_ref):
    nb = cb_ref.shape[0]
    for b in range(nb):
        cv = cb_ref[b]
        ca = cv * _sigmoid(cv)
        for j in range(MOD_TN // LANES):
            sl = slice(j * LANES, (j + 1) * LANES)
            col = jnp.sum(w_ref[:, sl] * ca, axis=0, keepdims=True)
            o_ref[b:b + 1, sl] = col + b_ref[:, sl]


def _modulation(c, ada_w, ada_b, layer):
    nb, d = c.shape
    n = ada_w.shape[2]
    cb = jnp.broadcast_to(c[:, :, None], (nb, d, LANES))
    return pl.pallas_call(
        _mod_kernel,
        grid=(n // MOD_TN,),
        in_specs=[pl.BlockSpec((nb, d, LANES), lambda j: (0, 0, 0)),
                  pl.BlockSpec((None, d, MOD_TN), lambda j: (layer, 0, j)),
                  pl.BlockSpec((None, 1, MOD_TN), lambda j: (layer, 0, j))],
        out_specs=pl.BlockSpec((nb, MOD_TN), lambda j: (0, j)),
        out_shape=jax.ShapeDtypeStruct((nb, n), F32),
        compiler_params=_cparams(("arbitrary",)),
        name="adaln_mod",
    )(cb, ada_w, ada_b.reshape(ada_b.shape[0], 1, n))


def _rms_rows(x):
    return x * lax.rsqrt(jnp.mean(x * x, axis=-1, keepdims=True) + EPS)


def _inproj_kernel(x_hbm, mod_ref, nw_ref, w_ref, wf_ref, fb_ref, cos_ref, sin_ref, qn_ref, kn_ref,
                   o_ref, lf_ref, h_scr, wb_scr, xbuf, xsem):
    i = pl.program_id(0)
    j = pl.program_id(1)
    tiles_per_sec = RET_WIDTH // IN_TN
    heads_per_tile = IN_TN // HEAD_DIM

    @pl.when(j == 0)
    def _():
        shift = mod_ref[0, 0:1, :]
        scale = 1.0 + mod_ref[0, 1:2, :]
        nw = nw_ref[...]
        n_chunks = IN_TM // IN_XC

        def x_copy(c):
            rows = pl.ds(i * IN_TM + c * IN_XC, IN_XC)
            return pltpu.make_async_copy(x_hbm.at[rows, :], xbuf.at[c % 2], xsem.at[c % 2])

        x_copy(0).start()
        for c in range(n_chunks):
            if c + 1 < n_chunks:
                x_copy(c + 1).start()
            x_copy(c).wait()
            rows = 128
            for r in range(IN_XC // rows):
                h = _rms_rows(xbuf[c % 2, r * rows:(r + 1) * rows, :]) * nw * scale + shift
                h_scr[c * IN_XC + r * rows:c * IN_XC + (r + 1) * rows, :] = h.astype(BF16)
        z = _nt_dot(wf_ref[...], h_scr[...]) + fb_ref[...]
        lf_ref[0] = jnp.minimum(z, 0.0) - jnp.log1p(jnp.exp(-jnp.abs(z)))

    wb_scr[...] = w_ref[...].astype(BF16)
    sec = j // tiles_per_sec

    def section(epilogue):
        for r in range(IN_TM // IN_RC):
            rs = slice(r * IN_RC, (r + 1) * IN_RC)
            acc = _nt_dot(h_scr[rs, :], wb_scr[...])
            epilogue(acc, rs)

    def per_head(fn):
        def epilogue(acc, rs):
            for hh in range(heads_per_tile):
                sl = slice(hh * HEAD_DIM, (hh + 1) * HEAD_DIM)
                o_ref[rs, sl] = fn(acc[:, sl], rs).astype(BF16)
        return epilogue

    def whole(fn):
        def epilogue(acc, rs):
            o_ref[rs, :] = fn(acc).astype(BF16)
        return epilogue

    def rope(v, rs):
        return v * cos_ref[rs, :] + pltpu.roll(v, HEAD_DIM // 2, 1) * sin_ref[rs, :]

    @pl.when(sec == 0)
    def _():
        section(per_head(rope))

    @pl.when(sec == 1)
    def _():
        section(per_head(lambda v, rs: rope(v, rs) * (HEAD_DIM ** -0.5)))

    @pl.when((sec == 2) | (sec == 6))
    def _():
        section(whole(lambda a: a))

    @pl.when(sec == 3)
    def _():
        section(whole(lambda a: a * _sigmoid(a)))

    @pl.when(sec == 4)
    def _():
        section(per_head(lambda v, rs: _rms_rows(v) * qn_ref[...] * (LOG2E * HEAD_DIM ** -0.5)))

    @pl.when(sec == 5)
    def _():
        section(per_head(lambda v, rs: _rms_rows(v) * kn_ref[...]))


def _in_projection(x2, mod3, norm_w, w_in, fox_f_bias, fox_qn_w, fox_kn_w, cosf, sinf, nb, s, layer):
    t, d = x2.shape
    w_t = jnp.swapaxes(w_in, 1, 2)
    wf_t = w_t[layer, MAIN_COLS:, :].astype(BF16)
    per_b = s // IN_TM
    grid = (t // IN_TM, MAIN_COLS // IN_TN)
    return pl.pallas_call(
        _inproj_kernel,
        grid=grid,
        in_specs=[
            pl.BlockSpec(memory_space=pl.ANY),
            pl.BlockSpec((1, N_MOD, d), lambda i, j: (i // per_b, 0, 0)),
            pl.BlockSpec((1, d), lambda i, j: (0, 0)),
            pl.BlockSpec((None, IN_TN, d), lambda i, j: (layer, j, 0)),
            pl.BlockSpec((FOX_HEADS, d), lambda i, j: (0, 0)),
            pl.BlockSpec((FOX_HEADS, 1), lambda i, j: (0, 0)),
            pl.BlockSpec((IN_TM, HEAD_DIM), lambda i, j: (i % per_b, 0)),
            pl.BlockSpec((IN_TM, HEAD_DIM), lambda i, j: (i % per_b, 0)),
            pl.BlockSpec((1, HEAD_DIM), lambda i, j: (0, 0)),
            pl.BlockSpec((1, HEAD_DIM), lambda i, j: (0, 0)),
        ],
        out_specs=[
            pl.BlockSpec((IN_TM, IN_TN), lambda i, j: (i, j)),
            pl.BlockSpec((1, FOX_HEADS, IN_TM), lambda i, j: (i // per_b, 0, i % per_b)),
        ],
        out_shape=[jax.ShapeDtypeStruct((t, MAIN_COLS), BF16),
                   jax.ShapeDtypeStruct((nb, FOX_HEADS, s), F32)],
        scratch_shapes=[pltpu.VMEM((IN_TM, d), BF16), pltpu.VMEM((IN_TN, d), BF16),
                        pltpu.VMEM((2, IN_XC, d), F32), pltpu.SemaphoreType.DMA((2,))],
        compiler_params=_cparams(("arbitrary", "arbitrary")),
        name="in_projection",
    )(x2, mod3, norm_w.reshape(1, d), w_t, wf_t, fox_f_bias.reshape(FOX_HEADS, 1), cosf, sinf,
      fox_qn_w.reshape(1, HEAD_DIM), fox_kn_w.reshape(1, HEAD_DIM))


def _split3(x):
    hi = x.astype(BF16)
    r1 = x - hi.astype(F32)
    mid = r1.astype(BF16)
    lo = (r1 - mid.astype(F32)).astype(BF16)
    return hi, mid, lo


def _cumsum_kernel(x_ref, o_ref, *, rows_per_seq):
    x = x_ref[...]
    r = x.shape[0]
    ri = lax.broadcasted_iota(I32, (LANES, LANES), 0)
    ci = lax.broadcasted_iota(I32, (LANES, LANES), 1)
    upper = jnp.where(ri <= ci, 1.0, 0.0).astype(BF16)
    rowcum = sum(jnp.dot(p, upper, preferred_element_type=F32) for p in _split3(x))
    tot = jnp.broadcast_to(rowcum[:, LANES - 1:LANES], (r, LANES))
    gi = lax.broadcasted_iota(I32, (r, r), 0)
    gj = lax.broadcasted_iota(I32, (r, r), 1)
    same_seq = (gi // rows_per_seq) == (gj // rows_per_seq)
    lower = jnp.where(same_seq & (gj < gi), 1.0, 0.0).astype(BF16)
    prefix = sum(jnp.dot(lower, p, preferred_element_type=F32) for p in _split3(tot))
    o_ref[...] = rowcum + prefix


def _cum_forget(lf):
    nb, nh, s = lf.shape
    rows = nb * nh * s // LANES
    out = pl.pallas_call(
        functools.partial(_cumsum_kernel, rows_per_seq=s // LANES),
        out_shape=jax.ShapeDtypeStruct((rows, LANES), F32),
        compiler_params=pltpu.CompilerParams(vmem_limit_bytes=VMEM_LIMIT),
        name="forget_cumsum",
    )(lf.reshape(rows, LANES))
    return out.reshape(nb * nh, 1, s)


def _ret_kernel(lg_ref, q_ref, k_ref, v_ref, g_ref, gnw_ref, o_ref):
    c = RET_CHUNK
    s = q_ref.shape[1]
    ri = lax.broadcasted_iota(I32, (c, c), 0)
    ci = lax.broadcasted_iota(I32, (c, c), 1)
    diff = (ri - ci).astype(F32)
    pos = lax.broadcasted_iota(I32, (c, 1), 0).astype(F32)
    consts = []
    for hh in range(RET_HPS):
        lg = lg_ref[hh][:, 0:1]
        decay = jnp.where(diff >= 0.0, jnp.exp(lg * jnp.maximum(diff, 0.0)), 0.0)
        zeta = jnp.exp(lg * (float(c - 1) - pos))
        xi = jnp.exp(lg * (pos + 1.0))
        chunk_decay = jnp.exp(lg * float(c))
        consts.append((decay, zeta, xi, chunk_decay))

    def body(n, states):
        r0 = pl.multiple_of(n * c, c)
        new_states = []
        for hh in range(RET_HPS):
            decay, zeta, xi, chunk_decay = consts[hh]
            sl = slice(hh * HEAD_DIM, (hh + 1) * HEAD_DIM)
            q = q_ref[0, pl.ds(r0, c), sl]
            k = k_ref[0, pl.ds(r0, c), sl]
            v = v_ref[0, pl.ds(r0, c), sl]
            scores = _nt_dot(q, k) * decay
            intra = jnp.dot(scores.astype(BF16), v, preferred_element_type=F32)
            cross = jnp.dot(q, states[hh].astype(BF16), preferred_element_type=F32) * xi
            kz = (k.astype(F32) * zeta).astype(BF16)
            kv = _tn_dot(kz, v)
            o = intra + cross
            mu = jnp.mean(o, axis=-1, keepdims=True)
            oc = o - mu
            var = jnp.mean(oc * oc, axis=-1, keepdims=True)
            y = oc * lax.rsqrt(var + EPS) * gnw_ref[:, sl] * g_ref[0, pl.ds(r0, c), sl].astype(F32)
            o_ref[0, pl.ds(r0, c), sl] = y.astype(BF16)
            new_states.append(states[hh] * chunk_decay + kv)
        return tuple(new_states)

    init = tuple(jnp.zeros((HEAD_DIM, HEAD_DIM), F32) for _ in range(RET_HPS))
    lax.fori_loop(0, s // c, body, init)


def _retention(proj3, ret_gn_w, lg_tab):
    nb, s, _ = proj3.shape
    w = RET_HPS * HEAD_DIM
    gpg = RET_WIDTH // w

    def col(sec):
        return pl.BlockSpec((1, s, w), lambda b, h: (b, 0, sec * gpg + h))

    return pl.pallas_call(
        _ret_kernel,
        grid=(nb, gpg),
        in_specs=[pl.BlockSpec((RET_HPS, 1, LANES), lambda b, h: (h, 0, 0)),
                  col(0), col(1), col(2), col(3),
                  pl.BlockSpec((1, w), lambda b, h: (0, h))],
        out_specs=pl.BlockSpec((1, s, w), lambda b, h: (b, 0, h)),
        out_shape=jax.ShapeDtypeStruct((nb, s, RET_WIDTH), BF16),
        compiler_params=_cparams(("arbitrary", "arbitrary")),
        name="retention",
    )(lg_tab, proj3, proj3, proj3, proj3, ret_gn_w.reshape(1, RET_WIDTH))


def _fox_kernel(q_ref, k_ref, v_ref, cum_ref, onw_ref, o_ref):
    tq, t = FOX_TQ, FOX_TK
    qi = pl.program_id(2)
    q0 = qi * tq
    last = q0 // t
    ri = lax.broadcasted_iota(I32, (tq, t), 0)
    ci = lax.broadcasted_iota(I32, (tq, t), 1)

    def step(ki, carry, masked):
        k0 = pl.multiple_of(ki * t, t)
        out = []
        for hh in range(FOX_HPS):
            m, l, acc = carry[hh]
            sl = slice(hh * HEAD_DIM, (hh + 1) * HEAD_DIM)
            k = k_ref[0, pl.ds(k0, t), sl]
            v = v_ref[0, pl.ds(k0, t), sl]
            sc = _nt_dot(q_ref[0, :, sl], k) - cum_ref[hh, :, pl.ds(k0, t)] * LOG2E
            if masked:
                sc = jnp.where(ci + k0 <= ri + q0, sc, NEG)
            m_new = jnp.maximum(m, jnp.max(sc, axis=1, keepdims=True))
            p = jnp.exp2(sc - m_new)
            alpha = jnp.exp2(m - m_new)
            l = alpha * l + jnp.sum(p, axis=1, keepdims=True)
            acc = alpha * acc + jnp.dot(p.astype(BF16), v, preferred_element_type=F32)
            out.append((m_new, l, acc))
        return tuple(out)

    init = tuple((jnp.full((tq, 1), NEG, F32), jnp.zeros((tq, 1), F32), jnp.zeros((tq, HEAD_DIM), F32))
                 for _ in range(FOX_HPS))
    carry = lax.fori_loop(0, last, lambda ki, cr: step(ki, cr, False), init)
    carry = step(last, carry, True)
    for hh in range(FOX_HPS):
        _, l, acc = carry[hh]
        sl = slice(hh * HEAD_DIM, (hh + 1) * HEAD_DIM)
        o = acc / l
        o_ref[0, :, sl] = (_rms_rows(o) * onw_ref[:, sl]).astype(BF16)


def _fox_attention(proj3, cum, fox_on_w):
    nb, s, _ = proj3.shape
    w = FOX_HPS * HEAD_DIM
    base = 4 * RET_WIDTH // w
    gpg = FOX_WIDTH // w
    return pl.pallas_call(
        _fox_kernel,
        grid=(nb, gpg, s // FOX_TQ),
        in_specs=[pl.BlockSpec((1, FOX_TQ, w), lambda b, h, i: (b, i, base + h)),
                  pl.BlockSpec((1, s, w), lambda b, h, i: (b, 0, base + gpg + h)),
                  pl.BlockSpec((1, s, w), lambda b, h, i: (b, 0, base + 2 * gpg + h)),
                  pl.BlockSpec((FOX_HPS, 1, s), lambda b, h, i: (b * gpg + h, 0, 0)),
                  pl.BlockSpec((1, w), lambda b, h, i: (0, h))],
        out_specs=pl.BlockSpec((1, FOX_TQ, w), lambda b, h, i: (b, i, h)),
        out_shape=jax.ShapeDtypeStruct((nb, s, FOX_WIDTH), BF16),
        compiler_params=_cparams(("arbitrary", "arbitrary", "arbitrary")),
        name="fox_attention",
    )(proj3, proj3, proj3, cum, fox_on_w.reshape(1, FOX_WIDTH))


def _outproj_kernel(ret_ref, fox_ref, wo_ref, x_ref, mod_ref, npost_ref, npre_ref, rw_ref, rb_ref,
                    x1_ref, h2_ref, idx_ref, wts_ref, cnt_ref, carry_scr):
    i = pl.program_id(0)

    @pl.when(i == 0)
    def _():
        carry_scr[...] = jnp.zeros_like(carry_scr)

    idx_ref[...] = jnp.zeros_like(idx_ref)
    subtiles = [slice(r * OUT_RC, (r + 1) * OUT_RC) for r in range(OUT_TM // OUT_RC)]
    ys = [jnp.dot(ret_ref[rs, :], wo_ref[:RET_WIDTH, :], preferred_element_type=F32)
          + jnp.dot(fox_ref[rs, :], wo_ref[RET_WIDTH:, :], preferred_element_type=F32) for rs in subtiles]
    for rs, y in zip(subtiles, ys):
        _outproj_rows(rs, y, x_ref, mod_ref, npost_ref, npre_ref, rw_ref, rb_ref, x1_ref, h2_ref, idx_ref,
                      wts_ref, cnt_ref, carry_scr)


def _outproj_rows(rs, y, x_ref, mod_ref, npost_ref, npre_ref, rw_ref, rb_ref,
                  x1_ref, h2_ref, idx_ref, wts_ref, cnt_ref, carry_scr):
    tm = OUT_RC
    gate1 = mod_ref[0, 2:3, :]
    shift2 = mod_ref[0, 3:4, :]
    scale2 = 1.0 + mod_ref[0, 4:5, :]
    x1 = x_ref[rs, :] + gate1 * (_rms_rows(y) * npost_ref[...])
    x1_ref[rs, :] = x1
    h2 = _rms_rows(x1) * npre_ref[...] * scale2 + shift2
    h2_ref[rs, :, :] = h2.astype(BF16).reshape(tm, ROW_TILES, LANES)

    hi = h2.astype(BF16)
    lo = (h2 - hi.astype(F32)).astype(BF16)
    rw = rw_ref[...]
    lg2 = jnp.dot(hi, rw, preferred_element_type=F32) + jnp.dot(lo, rw, preferred_element_type=F32)
    logits = lg2 + pltpu.roll(lg2, LANES // 2, 1) + rb_ref[...]

    lane = lax.broadcasted_iota(I32, (tm, LANES), 1)
    lanef = lane.astype(F32)
    big = float(LANES)

    def first_argmax(vals):
        vmax = jnp.max(vals, axis=1, keepdims=True)
        idx = jnp.min(jnp.where(vals == vmax, lanef, big), axis=1, keepdims=True)
        return vmax, idx

    is_coarse = (lane >= N_EXPERTS) & (lane < N_EXPERTS + N_GROUPS)
    cvals = jnp.where(is_coarse, logits, NEG)
    cmax, cidx = first_argmax(cvals)
    gidx = cidx - float(N_EXPERTS)
    gprob = 1.0 / jnp.sum(jnp.where(is_coarse, jnp.exp(cvals - cmax), 0.0), axis=1, keepdims=True)
    glo = gidx * float(EXPERTS_PER_GROUP)
    in_group = (lanef >= glo) & (lanef < glo + float(EXPERTS_PER_GROUP))
    fvals = jnp.where(in_group, logits, NEG)
    v1, e1 = first_argmax(fvals)
    fvals2 = jnp.where(lanef == e1, NEG, fvals)
    v2, e2 = first_argmax(fvals2)
    ex = jnp.exp(v2 - v1)
    w1 = gprob * (1.0 / (1.0 + ex))
    w2 = gprob * (ex / (1.0 + ex))

    sel1 = lanef == e1
    sel2 = lanef == e2
    onehot = jnp.where(sel1 | sel2, 1.0, 0.0)
    ri = lax.broadcasted_iota(I32, (tm, tm), 0)
    ci = lax.broadcasted_iota(I32, (tm, tm), 1)
    lower = jnp.where(ci < ri, 1.0, 0.0).astype(BF16)
    before = jnp.dot(lower, onehot.astype(BF16), preferred_element_type=F32) + carry_scr[...]
    r1 = jnp.sum(jnp.where(sel1, before, 0.0), axis=1, keepdims=True)
    r2 = jnp.sum(jnp.where(sel2, before, 0.0), axis=1, keepdims=True)
    carry_new = carry_scr[...] + jnp.sum(onehot, axis=0, keepdims=True)
    carry_scr[...] = carry_new
    cnt_ref[...] = carry_new.astype(I32)

    for row, col in enumerate((e1, e2, r1, r2)):
        idx_ref[row:row + 1, rs] = _col_to_row(col).astype(I32)
    wts_ref[rs, :] = jnp.where(lane == 0, w1, jnp.where(lane == 1, w2, 0.0))


def _out_projection(ret2, fox2, w_out_bf, x2, mod3, npost, npre, rw, rb, s):
    t, d = x2.shape
    per_b = s // OUT_TM
    row = lambda i: (i, 0)
    const = lambda i: (0, 0)
    return pl.pallas_call(
        _outproj_kernel,
        grid=(t // OUT_TM,),
        in_specs=[pl.BlockSpec((OUT_TM, RET_WIDTH), row),
                  pl.BlockSpec((OUT_TM, FOX_WIDTH), row),
                  pl.BlockSpec((d, d), const),
                  pl.BlockSpec((OUT_TM, d), row),
                  pl.BlockSpec((1, N_MOD, d), lambda i: (i // per_b, 0, 0)),
                  pl.BlockSpec((1, d), const),
                  pl.BlockSpec((1, d), const),
                  pl.BlockSpec((d, LANES), const),
                  pl.BlockSpec((1, LANES), const)],
        out_specs=[pl.BlockSpec((OUT_TM, d), row),
                   pl.BlockSpec((OUT_TM, ROW_TILES, LANES), lambda i: (i, 0, 0)),
                   pl.BlockSpec((SUBLANES, OUT_TM), lambda i: (0, i)),
                   pl.BlockSpec((OUT_TM, LANES), row),
                   pl.BlockSpec((1, LANES), const)],
        out_shape=[jax.ShapeDtypeStruct((t, d), F32),
                   jax.ShapeDtypeStruct((t, ROW_TILES, LANES), BF16),
                   jax.ShapeDtypeStruct((SUBLANES, t), I32),
                   jax.ShapeDtypeStruct((t, LANES), F32),
                   jax.ShapeDtypeStruct((1, LANES), I32)],
        scratch_shapes=[pltpu.VMEM((1, LANES), F32)],
        compiler_params=_cparams(("arbitrary",)),
        name="out_projection_router",
    )(ret2, fox2, w_out_bf, x2, mod3, npost.reshape(1, d), npre.reshape(1, d), rw, rb)


def _cast_kernel(w_ref, o_ref):
    o_ref[...] = w_ref[...].astype(BF16)


def _cast_bf16(w, layer, tr=256):
    _, r, c = w.shape
    return pl.pallas_call(
        _cast_kernel,
        grid=(r // tr,),
        in_specs=[pl.BlockSpec((None, tr, c), lambda i: (layer, i, 0))],
        out_specs=pl.BlockSpec((tr, c), lambda i: (i, 0)),
        out_shape=jax.ShapeDtypeStruct((r, c), BF16),
        compiler_params=_cparams(("arbitrary",)),
        name="cast_bf16",
    )(w)


PLAN_ROW_START = 0
PLAN_TILE_EXPERT = 1
PLAN_NUM_TILES = 2
PLAN_TILE_END = 3
PLAN_COUNT = 4


def _plan_kernel(idx_ref, cnt_ref, pos_ref, plan_ref):
    lane = lax.broadcasted_iota(I32, (1, LANES), 1)
    cnt = jnp.where(lane < N_EXPERTS, cnt_ref[...], 0).astype(F32)
    tiles = jnp.floor((cnt + float(EXP_TM - 1)) * (1.0 / EXP_TM))
    ri = lax.broadcasted_iota(I32, (LANES, LANES), 0)
    ci = lax.broadcasted_iota(I32, (LANES, LANES), 1)
    upper = jnp.where(ri <= ci, 1.0, 0.0).astype(BF16)
    tiles8 = jnp.broadcast_to(tiles, (SUBLANES, LANES)).astype(BF16)
    tile_end = jnp.dot(tiles8, upper, preferred_element_type=F32)[0:1, :]
    row_start = (tile_end - tiles) * float(EXP_TM)
    num_tiles = jnp.max(tile_end, axis=1, keepdims=True)
    end_col = _row_to_col(tile_end)
    owned = jnp.where((end_col <= ci.astype(F32)) & (ri < N_EXPERTS), 1.0, 0.0)
    tile_expert = jnp.sum(owned, axis=0, keepdims=True)
    last_expert = jnp.max(jnp.where(lane.astype(F32) < num_tiles, tile_expert, 0.0), axis=1, keepdims=True)
    tile_expert = jnp.minimum(tile_expert, last_expert)
    plan_ref[...] = jnp.zeros_like(plan_ref)
    plan_ref[PLAN_ROW_START:PLAN_ROW_START + 1, :] = row_start.astype(I32)
    plan_ref[PLAN_TILE_EXPERT:PLAN_TILE_EXPERT + 1, :] = tile_expert.astype(I32)
    plan_ref[PLAN_NUM_TILES:PLAN_NUM_TILES + 1, :] = jnp.broadcast_to(num_tiles, (1, LANES)).astype(I32)
    plan_ref[PLAN_TILE_END:PLAN_TILE_END + 1, :] = tile_end.astype(I32)
    plan_ref[PLAN_COUNT:PLAN_COUNT + 1, :] = cnt.astype(I32)

    experts = idx_ref[0:2, :]
    start_i = row_start.astype(I32)
    base = jnp.zeros(experts.shape, I32)
    for e in range(N_EXPERTS):
        base = jnp.where(experts == e, start_i[:, e:e + 1], base)
    pos_ref[...] = jnp.zeros_like(pos_ref)
    pos_ref[0:2, :] = idx_ref[2:4, :] + base


def _route_plan(idx, counts):
    t = idx.shape[1]
    return pl.pallas_call(
        _plan_kernel,
        out_shape=[jax.ShapeDtypeStruct((SUBLANES, t), I32),
                   jax.ShapeDtypeStruct((SUBLANES, LANES), I32)],
        compiler_params=pltpu.CompilerParams(vmem_limit_bytes=VMEM_LIMIT),
        name="route_plan",
    )(idx, counts)


def _dispatch_kernel(pa_ref, pb_ref, plan_ref, h2_hbm, xs_ref, inv, h2v, sem):
    i = pl.program_id(0)
    n_tok = h2_hbm.shape[0]
    nt = plan_ref[PLAN_NUM_TILES, 0]
    tiles_per_step = DSP_TM // EXP_TM
    table_copy = pltpu.make_async_copy(h2_hbm, h2v, sem)

    @pl.when(i == 0)
    def _():
        table_copy.start()

        def fill(p, _):
            inv[p] = 0
            return 0
        for e in range(N_EXPERTS):
            lax.fori_loop(plan_ref[PLAN_ROW_START, e] + plan_ref[PLAN_COUNT, e],
                          plan_ref[PLAN_TILE_END, e] * EXP_TM, fill, 0)

        def scatter(j, _):
            for u in range(DMA_UNROLL):
                tok = j * DMA_UNROLL + u
                inv[pa_ref[tok]] = tok
                inv[pb_ref[tok]] = tok
            return 0
        lax.fori_loop(0, n_tok // DMA_UNROLL, scatter, 0)
        table_copy.wait()

    for sub in range(tiles_per_step):
        tile = i * tiles_per_step + sub
        rows = pl.ds(sub * EXP_TM, EXP_TM)

        @pl.when(tile < nt)
        def _():
            base = tile * EXP_TM

            def body(j, _):
                for u in range(DSP_UNROLL):
                    r = j * DSP_UNROLL + u
                    xs_ref[sub * EXP_TM + r] = h2v[inv[base + r]]
                return 0
            lax.fori_loop(0, EXP_TM // DSP_UNROLL, body, 0)

        @pl.when(tile >= nt)
        def _():
            xs_ref[rows] = jnp.zeros((EXP_TM, ROW_TILES, LANES), BF16)


def _dispatch(h2t, pos_a, pos_b, plan):
    t = h2t.shape[0]
    n_rows = ((t * 2) // EXP_TM + N_EXPERTS) * EXP_TM
    return pl.pallas_call(
        _dispatch_kernel,
        grid_spec=pltpu.PrefetchScalarGridSpec(
            num_scalar_prefetch=3,
            grid=(n_rows // DSP_TM,),
            in_specs=[pl.BlockSpec(memory_space=pl.ANY)],
            out_specs=pl.BlockSpec((DSP_TM, ROW_TILES, LANES), lambda i, pa, pb, plan: (i, 0, 0)),
            scratch_shapes=[pltpu.SMEM((n_rows,), I32),
                            pltpu.VMEM((t, ROW_TILES, LANES), BF16),
                            pltpu.SemaphoreType.DMA(())]),
        out_shape=jax.ShapeDtypeStruct((n_rows, ROW_TILES, LANES), BF16),
        compiler_params=_cparams(("arbitrary",)),
        name="moe_dispatch",
    )(pos_a, pos_b, plan, h2t)


def _expert_kernel(plan_ref, xs_ref, w1_hbm, w3_hbm, w2_hbm, ys_ref,
                   seq, w1s, w3s, w2s, wsem, w1b, w3b, w2b, *, layer):
    i = pl.program_id(0)
    tm = EXP_TM
    nt = plan_ref[PLAN_NUM_TILES, 0]
    expert = plan_ref[PLAN_TILE_EXPERT, i]

    def weight_copies(e, buf):
        return (pltpu.make_async_copy(w1_hbm.at[layer, e], w1s.at[buf], wsem.at[buf, 0]),
                pltpu.make_async_copy(w3_hbm.at[layer, e], w3s.at[buf], wsem.at[buf, 1]),
                pltpu.make_async_copy(w2_hbm.at[layer, e], w2s.at[buf], wsem.at[buf, 2]))

    def weight_start(e, buf):
        for cp in weight_copies(e, buf):
            cp.start(priority=1)

    def next_used_tile(e):
        return plan_ref[PLAN_TILE_END, e]

    @pl.when(i == 0)
    def _():
        first_expert = plan_ref[PLAN_TILE_EXPERT, 0]
        weight_start(first_expert, 0)
        t1 = next_used_tile(first_expert)

        @pl.when(t1 < nt)
        def _():
            weight_start(plan_ref[PLAN_TILE_EXPERT, t1], 1)
        seq[0] = 0

    valid = i < nt
    first = (i == 0) | (expert != plan_ref[PLAN_TILE_EXPERT, jnp.maximum(i - 1, 0)])

    @pl.when(valid & first)
    def _():
        buf = seq[0] % 2
        for cp in weight_copies(expert, buf):
            cp.wait()
        w1b[...] = w1s[buf].astype(BF16)
        w3b[...] = w3s[buf].astype(BF16)
        w2b[...] = w2s[buf].astype(BF16)
        seq[0] = seq[0] + 1
        t1 = next_used_tile(expert)
        t2 = next_used_tile(plan_ref[PLAN_TILE_EXPERT, jnp.minimum(t1, LANES - 1)])

        @pl.when((t1 < nt) & (t2 < nt))
        def _():
            weight_start(plan_ref[PLAN_TILE_EXPERT, t2], buf)

    @pl.when(valid)
    def _():
        xt = xs_ref[...].reshape(tm, D_MODEL)
        a = jnp.dot(xt, w1b[...], preferred_element_type=F32)
        g = jnp.dot(xt, w3b[...], preferred_element_type=F32)
        act = (a * _sigmoid(a) * g).astype(BF16)
        y = jnp.dot(act, w2b[...], preferred_element_type=F32)
        ys_ref[...] = y.astype(BF16).reshape(tm, ROW_TILES, LANES)

    @pl.when(jnp.logical_not(valid))
    def _():
        ys_ref[...] = jnp.zeros_like(ys_ref)


def _expert_gemm(xs, plan, w1, w3, w2, layer):
    n_rows = xs.shape[0]
    n_tiles = n_rows // EXP_TM
    d, de = w1.shape[2], w1.shape[3]
    any_spec = pl.BlockSpec(memory_space=pl.ANY)

    def used_tile(i, plan):
        return (jnp.minimum(i, plan[PLAN_NUM_TILES, 0] - 1), 0, 0)

    return pl.pallas_call(
        functools.partial(_expert_kernel, layer=layer),
        grid_spec=pltpu.PrefetchScalarGridSpec(
            num_scalar_prefetch=1,
            grid=(n_tiles,),
            in_specs=[pl.BlockSpec((EXP_TM, ROW_TILES, LANES), used_tile), any_spec, any_spec, any_spec],
            out_specs=pl.BlockSpec((EXP_TM, ROW_TILES, LANES), lambda i, plan: (i, 0, 0)),
            scratch_shapes=[pltpu.SMEM((1,), I32),
                            pltpu.VMEM((2, d, de), F32),
                            pltpu.VMEM((2, d, de), F32),
                            pltpu.VMEM((2, de, d), F32),
                            pltpu.SemaphoreType.DMA((2, 3)),
                            pltpu.VMEM((d, de), BF16),
                            pltpu.VMEM((d, de), BF16),
                            pltpu.VMEM((de, d), BF16)]),
        out_shape=jax.ShapeDtypeStruct((n_rows, ROW_TILES, LANES), BF16),
        compiler_params=_cparams(("arbitrary",)),
        name="moe_expert_gemm",
    )(plan, xs, w1, w3, w2)


def _combine_kernel(pa_ref, pb_ref, ys_hbm, x1_ref, wts_ref, mod_ref, npost_ref, o_ref, buf_a, buf_b, sems):
    i = pl.program_id(0)
    n = pl.num_programs(0)
    tm = CMB_TM

    def start_tile(tile, slot):
        def body(j, _):
            for u in range(DMA_UNROLL):
                r = j * DMA_UNROLL + u
                tok = tile * tm + r
                pltpu.make_async_copy(ys_hbm.at[pa_ref[tok]], buf_a.at[slot, r], sems.at[0, slot]).start()
                pltpu.make_async_copy(ys_hbm.at[pb_ref[tok]], buf_b.at[slot, r], sems.at[1, slot]).start()
            return 0
        lax.fori_loop(0, tm // DMA_UNROLL, body, 0)

    def wait_tile(slot):
        pltpu.make_async_copy(ys_hbm.at[pl.ds(0, tm)], buf_a.at[slot], sems.at[0, slot]).wait()
        pltpu.make_async_copy(ys_hbm.at[pl.ds(0, tm)], buf_b.at[slot], sems.at[1, slot]).wait()

    slot = i % 2

    @pl.when(i == 0)
    def _():
        start_tile(0, 0)

    @pl.when(i + 1 < n)
    def _():
        start_tile(i + 1, 1 - slot)

    wait_tile(slot)

    wa = wts_ref[:, 0:1]
    wb = wts_ref[:, 1:2]
    ya = buf_a[slot].reshape(tm, D_MODEL).astype(F32)
    yb = buf_b[slot].reshape(tm, D_MODEL).astype(F32)
    y = wa * ya + wb * yb
    gate2 = mod_ref[0, 5:6, :]
    o_ref[...] = x1_ref[...] + gate2 * (_rms_rows(y) * npost_ref[...])


def _combine(ys, pos_a, pos_b, x1, wts, mod3, npost, s):
    t, d = x1.shape
    per_b = s // CMB_TM
    row = lambda i, pa, pb: (i, 0)
    return pl.pallas_call(
        _combine_kernel,
        grid_spec=pltpu.PrefetchScalarGridSpec(
            num_scalar_prefetch=2,
            grid=(t // CMB_TM,),
            in_specs=[pl.BlockSpec(memory_space=pl.ANY),
                      pl.BlockSpec((CMB_TM, d), row),
                      pl.BlockSpec((CMB_TM, LANES), row),
                      pl.BlockSpec((1, N_MOD, d), lambda i, pa, pb: (i // per_b, 0, 0)),
                      pl.BlockSpec((1, d), lambda i, pa, pb: (0, 0))],
            out_specs=pl.BlockSpec((CMB_TM, d), row),
            scratch_shapes=[pltpu.VMEM((2, CMB_TM, ROW_TILES, LANES), BF16),
                            pltpu.VMEM((2, CMB_TM, ROW_TILES, LANES), BF16),
                            pltpu.SemaphoreType.DMA((2, 2))]),
        out_shape=jax.ShapeDtypeStruct((t, d), F32),
        compiler_params=_cparams(("arbitrary",)),
        name="moe_combine",
    )(pos_a, pos_b, ys, x1, wts, mod3, npost.reshape(1, d))


def _rope_tables(s):
    inv_freq = ROPE_BASE ** (-jnp.arange(0, HEAD_DIM, 2, dtype=F32) / HEAD_DIM)
    angle = jnp.arange(s, dtype=F32)[:, None] * inv_freq[None, :]
    cos = jnp.cos(angle)
    sin = jnp.sin(angle)
    return jnp.concatenate([cos, cos], axis=-1), jnp.concatenate([-sin, sin], axis=-1)


def _router_weights(router_coarse_w, router_fine_w, router_coarse_b, router_fine_b):
    d = router_fine_w.shape[0]
    w = jnp.concatenate([router_fine_w, router_coarse_w], axis=1)
    n = w.shape[1]
    w_hi = w.astype(BF16)
    w_lo = (w - w_hi.astype(F32)).astype(BF16)
    pad = jnp.zeros((d, LANES // 2 - n), BF16)
    rw = jnp.concatenate([w_hi, pad, w_lo, pad], axis=1)
    rb = jnp.concatenate([router_fine_b, router_coarse_b, jnp.zeros((LANES - n,), F32)]).reshape(1, LANES)
    return rw, rb


def _layer(x2, c, nb, s, cosf, sinf, lg_tab, layer, ada_w, ada_b, norm_pre_mix, norm_post_mix, w_in, fox_f_bias,
           ret_gn_w, fox_qn_w, fox_kn_w, fox_on_w, w_out, norm_pre_ffn, norm_post_ffn,
           router_coarse_w, router_coarse_b, router_fine_w, router_fine_b, w1, w3, w2):
    t, d = x2.shape
    mod3 = _modulation(c, ada_w, ada_b, layer).reshape(nb, N_MOD, d)

    proj, lf = _in_projection(x2, mod3, norm_pre_mix[layer], w_in, fox_f_bias[layer], fox_qn_w[layer],
                              fox_kn_w[layer], cosf, sinf, nb, s, layer)
    cum = _cum_forget(lf)
    proj3 = proj.reshape(nb, s, MAIN_COLS)
    ret = _retention(proj3, ret_gn_w[layer], lg_tab)
    fox = _fox_attention(proj3, cum, fox_on_w[layer])

    rw, rb = _router_weights(router_coarse_w[layer], router_fine_w[layer], router_coarse_b[layer],
                             router_fine_b[layer])
    x1, h2t, idx, wts, counts = _out_projection(
        ret.reshape(t, RET_WIDTH), fox.reshape(t, FOX_WIDTH), _cast_bf16(w_out, layer), x2, mod3,
        norm_post_mix[layer], norm_pre_ffn[layer], rw, rb, s)

    pos, plan = _route_plan(idx, counts)
    pos_a, pos_b = pos[0], pos[1]
    xs = _dispatch(h2t, pos_a, pos_b, plan)
    ys = _expert_gemm(xs, plan, w1, w3, w2, layer)
    return _combine(ys, pos_a, pos_b, x1, wts, mod3, norm_post_ffn[layer], s)


def kernel(x, c, ada_w, ada_b, norm_pre_mix, norm_post_mix, w_in, fox_f_bias, ret_gn_w, fox_qn_w, fox_kn_w,
           fox_on_w, w_out, norm_pre_ffn, norm_post_ffn, router_coarse_w, router_coarse_b, router_fine_w,
           router_fine_b, w1, w3, w2):
    nb, s, d = x.shape
    depth = ada_w.shape[0]
    cosf, sinf = _rope_tables(s)
    log_gamma = jnp.log1p(-jnp.exp2(-5.0 - jnp.arange(RET_HEADS, dtype=F32)))
    lg_tab = jnp.broadcast_to(log_gamma[:, None, None], (RET_HEADS, 1, LANES))
    x2 = x.reshape(nb * s, d)
    for layer in range(depth):
        x2 = _layer(x2, c, nb, s, cosf, sinf, lg_tab, layer, ada_w, ada_b, norm_pre_mix, norm_post_mix,
                    w_in, fox_f_bias, ret_gn_w, fox_qn_w, fox_kn_w, fox_on_w, w_out, norm_pre_ffn,
                    norm_post_ffn, router_coarse_w, router_coarse_b, router_fine_w, router_fine_b, w1, w3, w2)
    return x2.reshape(nb, s, d)
```

```python
import functools

import jax
import jax.numpy as jnp
from jax import lax
from jax.experimental import pallas as pl
from jax.experimental.pallas import tpu as pltpu

F32 = jnp.float32
BF16 = jnp.bfloat16
I32 = jnp.int32

D_MODEL = 2048
HEAD_DIM = 128
RET_WIDTH = D_MODEL // 2
FOX_WIDTH = D_MODEL - RET_WIDTH
RET_HEADS = RET_WIDTH // HEAD_DIM
FOX_HEADS = FOX_WIDTH // HEAD_DIM
ROPE_BASE = 10000.0
N_GROUPS = 4
EXPERTS_PER_GROUP = 8
N_EXPERTS = N_GROUPS * EXPERTS_PER_GROUP
D_EXPERT = D_MODEL // 4
N_MOD = 6
EPS = 1e-6
MAIN_COLS = 4 * RET_WIDTH + 3 * FOX_WIDTH

LANES = 128
SUBLANES = 8
ROW_TILES = D_MODEL // LANES
NEG = -1e30
LOG2E = 1.4426950408889634

VMEM_LIMIT = 56 * 1024 * 1024

MOD_TN = 1024
IN_TM, IN_TN = 2048, 512
IN_RC = 256
IN_XC = 512
RET_CHUNK = 512
RET_HPS = 2
FOX_TQ, FOX_TK = 512, 512
FOX_HPS = 2
FOX_UNROLL = 4
OUT_TM = 512
OUT_RC = 256
EXP_TM = 256
DSP_TM = 512
DSP_UNROLL = 8
CMB_TM = 256
DMA_UNROLL = 8


def _cparams(sem):
    return pltpu.CompilerParams(dimension_semantics=sem, vmem_limit_bytes=VMEM_LIMIT)


def _nt_dot(a, b):
    return lax.dot_general(a, b, (((1,), (1,)), ((), ())), preferred_element_type=F32)


def _tn_dot(a, b):
    return lax.dot_general(a, b, (((0,), (0,)), ((), ())), preferred_element_type=F32)


def _sigmoid(x):
    return 1.0 / (1.0 + jnp.exp(-x))


def _col_to_row(col):
    n = col.shape[0]
    ri = lax.broadcasted_iota(I32, (n, n), 0)
    ci = lax.broadcasted_iota(I32, (n, n), 1)
    return jnp.sum(jnp.where(ri == ci, col, 0.0), axis=0, keepdims=True)


def _row_to_col(row):
    n = row.shape[1]
    ri = lax.broadcasted_iota(I32, (n, n), 0)
    ci = lax.broadcasted_iota(I32, (n, n), 1)
    return jnp.sum(jnp.where(ri == ci, row, 0.0), axis=1, keepdims=True)


def _mod_kernel(cb_ref, w_ref, b_ref, o_ref):
    nb = cb_ref.shape[0]
    for b in range(nb):
        cv = cb_ref[b]
        ca = cv * _sigmoid(cv)
        for j in range(MOD_TN // LANES):
            sl = slice(j * LANES, (j + 1) * LANES)
            col = jnp.sum(w_ref[:, sl] * ca, axis=0, keepdims=True)
            o_ref[b:b + 1, sl] = col + b_ref[:, sl]


def _modulation(c, ada_w, ada_b, layer):
    nb, d = c.shape
    n = ada_w.shape[2]
    cb = jnp.broadcast_to(c[:, :, None], (nb, d, LANES))
    return pl.pallas_call(
        _mod_kernel,
        grid=(n // MOD_TN,),
        in_specs=[pl.BlockSpec((nb, d, LANES), lambda j: (0, 0, 0)),
                  pl.BlockSpec((None, d, MOD_TN), lambda j: (layer, 0, j)),
                  pl.BlockSpec((None, 1, MOD_TN), lambda j: (layer, 0, j))],
        out_specs=pl.BlockSpec((nb, MOD_TN), lambda j: (0, j)),
        out_shape=jax.ShapeDtypeStruct((nb, n), F32),
        compiler_params=_cparams(("arbitrary",)),
        name="adaln_mod",
    )(cb, ada_w, ada_b.reshape(ada_b.shape[0], 1, n))


def _rms_rows(x):
    return x * lax.rsqrt(jnp.mean(x * x, axis=-1, keepdims=True) + EPS)


def _inproj_kernel(x_hbm, mod_ref, nw_ref, w_ref, wf_ref, fb_ref, cos_ref, sin_ref, qn_ref, kn_ref,
                   o_ref, lf_ref, h_scr, wb_scr, xbuf, xsem):
    i = pl.program_id(0)
    j = pl.program_id(1)
    tiles_per_sec = RET_WIDTH // IN_TN
    heads_per_tile = IN_TN // HEAD_DIM

    @pl.when(j == 0)
    def _():
        shift = mod_ref[0, 0:1, :]
        scale = 1.0 + mod_ref[0, 1:2, :]
        nw = nw_ref[...]
        n_chunks = IN_TM // IN_XC

        def x_copy(c):
            rows = pl.ds(i * IN_TM + c * IN_XC, IN_XC)
            return pltpu.make_async_copy(x_hbm.at[rows, :], xbuf.at[c % 2], xsem.at[c % 2])

        x_copy(0).start()
        for c in range(n_chunks):
            if c + 1 < n_chunks:
                x_copy(c + 1).start()
            x_copy(c).wait()
            rows = 128
            for r in range(IN_XC // rows):
                h = _rms_rows(xbuf[c % 2, r * rows:(r + 1) * rows, :]) * nw * scale + shift
                h_scr[c * IN_XC + r * rows:c * IN_XC + (r + 1) * rows, :] = h.astype(BF16)
        z = _nt_dot(wf_ref[...], h_scr[...]) + fb_ref[...]
        lf_ref[0] = jnp.minimum(z, 0.0) - jnp.log1p(jnp.exp(-jnp.abs(z)))

    wb_scr[...] = w_ref[...].astype(BF16)
    sec = j // tiles_per_sec

    def section(epilogue):
        for r in range(IN_TM // IN_RC):
            rs = slice(r * IN_RC, (r + 1) * IN_RC)
            acc = _nt_dot(h_scr[rs, :], wb_scr[...])
            epilogue(acc, rs)

    def per_head(fn):
        def epilogue(acc, rs):
            for hh in range(heads_per_tile):
                sl = slice(hh * HEAD_DIM, (hh + 1) * HEAD_DIM)
                o_ref[rs, sl] = fn(acc[:, sl], rs).astype(BF16)
        return epilogue

    def whole(fn):
        def epilogue(acc, rs):
            o_ref[rs, :] = fn(acc).astype(BF16)
        return epilogue

    def rope(v, rs):
        return v * cos_ref[rs, :] + pltpu.roll(v, HEAD_DIM // 2, 1) * sin_ref[rs, :]

    @pl.when(sec == 0)
    def _():
        section(per_head(rope))

    @pl.when(sec == 1)
    def _():
        section(per_head(lambda v, rs: rope(v, rs) * (HEAD_DIM ** -0.5)))

    @pl.when((sec == 2) | (sec == 6))
    def _():
        section(whole(lambda a: a))

    @pl.when(sec == 3)
    def _():
        section(whole(lambda a: a * _sigmoid(a)))

    @pl.when(sec == 4)
    def _():
        section(per_head(lambda v, rs: _rms_rows(v) * qn_ref[...] * (LOG2E * HEAD_DIM ** -0.5)))

    @pl.when(sec == 5)
    def _():
        section(per_head(lambda v, rs: _rms_rows(v) * kn_ref[...]))


def _in_projection(x2, mod3, norm_w, w_in, fox_f_bias, fox_qn_w, fox_kn_w, cosf, sinf, nb, s, layer):
    t, d = x2.shape
    w_t = jnp.swapaxes(w_in, 1, 2)
    wf_t = w_t[layer, MAIN_COLS:, :].astype(BF16)
    per_b = s // IN_TM
    grid = (t // IN_TM, MAIN_COLS // IN_TN)
    return pl.pallas_call(
        _inproj_kernel,
        grid=grid,
        in_specs=[
            pl.BlockSpec(memory_space=pl.ANY),
            pl.BlockSpec((1, N_MOD, d), lambda i, j: (i // per_b, 0, 0)),
            pl.BlockSpec((1, d), lambda i, j: (0, 0)),
            pl.BlockSpec((None, IN_TN, d), lambda i, j: (layer, j, 0)),
            pl.BlockSpec((FOX_HEADS, d), lambda i, j: (0, 0)),
            pl.BlockSpec((FOX_HEADS, 1), lambda i, j: (0, 0)),
            pl.BlockSpec((IN_TM, HEAD_DIM), lambda i, j: (i % per_b, 0)),
            pl.BlockSpec((IN_TM, HEAD_DIM), lambda i, j: (i % per_b, 0)),
            pl.BlockSpec((1, HEAD_DIM), lambda i, j: (0, 0)),
            pl.BlockSpec((1, HEAD_DIM), lambda i, j: (0, 0)),
        ],
        out_specs=[
            pl.BlockSpec((IN_TM, IN_TN), lambda i, j: (i, j)),
            pl.BlockSpec((1, FOX_HEADS, IN_TM), lambda i, j: (i // per_b, 0, i % per_b)),
        ],
        out_shape=[jax.ShapeDtypeStruct((t, MAIN_COLS), BF16),
                   jax.ShapeDtypeStruct((nb, FOX_HEADS, s), F32)],
        scratch_shapes=[pltpu.VMEM((IN_TM, d), BF16), pltpu.VMEM((IN_TN, d), BF16),
                        pltpu.VMEM((2, IN_XC, d), F32), pltpu.SemaphoreType.DMA((2,))],
        compiler_params=_cparams(("arbitrary", "arbitrary")),
        name="in_projection",
    )(x2, mod3, norm_w.reshape(1, d), w_t, wf_t, fox_f_bias.reshape(FOX_HEADS, 1), cosf, sinf,
      fox_qn_w.reshape(1, HEAD_DIM), fox_kn_w.reshape(1, HEAD_DIM))


def _split3(x):
    hi = x.astype(BF16)
    r1 = x - hi.astype(F32)
    mid = r1.astype(BF16)
    lo = (r1 - mid.astype(F32)).astype(BF16)
    return hi, mid, lo


def _cumsum_kernel(x_ref, o_ref, *, rows_per_seq):
    x = x_ref[...]
    r = x.shape[0]
    ri = lax.broadcasted_iota(I32, (LANES, LANES), 0)
    ci = lax.broadcasted_iota(I32, (LANES, LANES), 1)
    upper = jnp.where(ri <= ci, 1.0, 0.0).astype(BF16)
    rowcum = sum(jnp.dot(p, upper, preferred_element_type=F32) for p in _split3(x))
    tot = jnp.broadcast_to(rowcum[:, LANES - 1:LANES], (r, LANES))
    gi = lax.broadcasted_iota(I32, (r, r), 0)
    gj = lax.broadcasted_iota(I32, (r, r), 1)
    same_seq = (gi // rows_per_seq) == (gj // rows_per_seq)
    lower = jnp.where(same_seq & (gj < gi), 1.0, 0.0).astype(BF16)
    prefix = sum(jnp.dot(lower, p, preferred_element_type=F32) for p in _split3(tot))
    o_ref[...] = rowcum + prefix


def _cum_forget(lf):
    nb, nh, s = lf.shape
    rows = nb * nh * s // LANES
    out = pl.pallas_call(
        functools.partial(_cumsum_kernel, rows_per_seq=s // LANES),
        out_shape=jax.ShapeDtypeStruct((rows, LANES), F32),
        compiler_params=pltpu.CompilerParams(vmem_limit_bytes=VMEM_LIMIT),
        name="forget_cumsum",
    )(lf.reshape(rows, LANES))
    return out.reshape(nb * nh, 1, s)


def _ret_kernel(lg_ref, q_ref, k_ref, v_ref, g_ref, gnw_ref, o_ref):
    c = RET_CHUNK
    s = q_ref.shape[1]
    ri = lax.broadcasted_iota(I32, (c, c), 0)
    ci = lax.broadcasted_iota(I32, (c, c), 1)
    diff = (ri - ci).astype(F32)
    pos = lax.broadcasted_iota(I32, (c, 1), 0).astype(F32)
    consts = []
    for hh in range(RET_HPS):
        lg = lg_ref[hh][:, 0:1]
        decay = jnp.where(diff >= 0.0, jnp.exp(lg * jnp.maximum(diff, 0.0)), 0.0)
        zeta = jnp.exp(lg * (float(c - 1) - pos))
        xi = jnp.exp(lg * (pos + 1.0))
        chunk_decay = jnp.exp(lg * float(c))
        consts.append((decay, zeta, xi, chunk_decay))

    def body(n, states):
        r0 = pl.multiple_of(n * c, c)
        new_states = []
        for hh in range(RET_HPS):
            decay, zeta, xi, chunk_decay = consts[hh]
            sl = slice(hh * HEAD_DIM, (hh + 1) * HEAD_DIM)
            q = q_ref[0, pl.ds(r0, c), sl]
            k = k_ref[0, pl.ds(r0, c), sl]
            v = v_ref[0, pl.ds(r0, c), sl]
            scores = _nt_dot(q, k) * decay
            intra = jnp.dot(scores.astype(BF16), v, preferred_element_type=F32)
            cross = jnp.dot(q, states[hh].astype(BF16), preferred_element_type=F32) * xi
            kz = (k.astype(F32) * zeta).astype(BF16)
            kv = _tn_dot(kz, v)
            o = intra + cross
            mu = jnp.mean(o, axis=-1, keepdims=True)
            oc = o - mu
            var = jnp.mean(oc * oc, axis=-1, keepdims=True)
            y = oc * lax.rsqrt(var + EPS) * gnw_ref[:, sl] * g_ref[0, pl.ds(r0, c), sl].astype(F32)
            o_ref[0, pl.ds(r0, c), sl] = y.astype(BF16)
            new_states.append(states[hh] * chunk_decay + kv)
        return tuple(new_states)

    init = tuple(jnp.zeros((HEAD_DIM, HEAD_DIM), F32) for _ in range(RET_HPS))
    lax.fori_loop(0, s // c, body, init)


def _retention(proj3, ret_gn_w, lg_tab):
    nb, s, _ = proj3.shape
    w = RET_HPS * HEAD_DIM
    gpg = RET_WIDTH // w

    def col(sec):
        return pl.BlockSpec((1, s, w), lambda b, h: (b, 0, sec * gpg + h))

    return pl.pallas_call(
        _ret_kernel,
        grid=(nb, gpg),
        in_specs=[pl.BlockSpec((RET_HPS, 1, LANES), lambda b, h: (h, 0, 0)),
                  col(0), col(1), col(2), col(3),
                  pl.BlockSpec((1, w), lambda b, h: (0, h))],
        out_specs=pl.BlockSpec((1, s, w), lambda b, h: (b, 0, h)),
        out_shape=jax.ShapeDtypeStruct((nb, s, RET_WIDTH), BF16),
        compiler_params=_cparams(("arbitrary", "arbitrary")),
        name="retention",
    )(lg_tab, proj3, proj3, proj3, proj3, ret_gn_w.reshape(1, RET_WIDTH))


def _fox_kernel(q_ref, k_ref, v_ref, cum_ref, onw_ref, o_ref):
    tq, t = FOX_TQ, FOX_TK
    qi = pl.program_id(2)
    q0 = qi * tq
    last = q0 // t
    ri = lax.broadcasted_iota(I32, (tq, t), 0)
    ci = lax.broadcasted_iota(I32, (tq, t), 1)

    def step(ki, carry, masked):
        k0 = pl.multiple_of(ki * t, t)
        out = []
        for hh in range(FOX_HPS):
            m, l, acc = carry[hh]
            sl = slice(hh * HEAD_DIM, (hh + 1) * HEAD_DIM)
            k = k_ref[0, pl.ds(k0, t), sl]
            v = v_ref[0, pl.ds(k0, t), sl]
            sc = _nt_dot(q_ref[0, :, sl], k) - cum_ref[hh, :, pl.ds(k0, t)] * LOG2E
            if masked:
                sc = jnp.where(ci + k0 <= ri + q0, sc, NEG)
            m_new = jnp.maximum(m, jnp.max(sc, axis=1, keepdims=True))
            p = jnp.exp2(sc - m_new)
            alpha = jnp.exp2(m - m_new)
            l = alpha * l + jnp.sum(p, axis=1, keepdims=True)
            acc = alpha * acc + jnp.dot(p.astype(BF16), v, preferred_element_type=F32)
            out.append((m_new, l, acc))
        return tuple(out)

    init = tuple((jnp.full((tq, 1), NEG, F32), jnp.zeros((tq, 1), F32), jnp.zeros((tq, HEAD_DIM), F32))
                 for _ in range(FOX_HPS))
    def run(first, count, cr):
        for u in range(count):
            cr = step(first + u, cr, False)
        return cr

    carry = lax.fori_loop(0, last // FOX_UNROLL, lambda j, cr: run(j * FOX_UNROLL, FOX_UNROLL, cr), init)
    done = (last // FOX_UNROLL) * FOX_UNROLL
    group = FOX_UNROLL // 2
    while group >= 1:
        take = (last - done) >= group
        carry = lax.cond(take, functools.partial(run, done, group), lambda cr: cr, carry)
        done = done + jnp.where(take, group, 0)
        group //= 2
    carry = step(last, carry, True)
    for hh in range(FOX_HPS):
        _, l, acc = carry[hh]
        sl = slice(hh * HEAD_DIM, (hh + 1) * HEAD_DIM)
        o = acc / l
        o_ref[0, :, sl] = (_rms_rows(o) * onw_ref[:, sl]).astype(BF16)


def _fox_attention(proj3, cum, fox_on_w):
    nb, s, _ = proj3.shape
    w = FOX_HPS * HEAD_DIM
    base = 4 * RET_WIDTH // w
    gpg = FOX_WIDTH // w
    return pl.pallas_call(
        _fox_kernel,
        grid=(nb, gpg, s // FOX_TQ),
        in_specs=[pl.BlockSpec((1, FOX_TQ, w), lambda b, h, i: (b, i, base + h)),
                  pl.BlockSpec((1, s, w), lambda b, h, i: (b, 0, base + gpg + h)),
                  pl.BlockSpec((1, s, w), lambda b, h, i: (b, 0, base + 2 * gpg + h)),
                  pl.BlockSpec((FOX_HPS, 1, s), lambda b, h, i: (b * gpg + h, 0, 0)),
                  pl.BlockSpec((1, w), lambda b, h, i: (0, h))],
        out_specs=pl.BlockSpec((1, FOX_TQ, w), lambda b, h, i: (b, i, h)),
        out_shape=jax.ShapeDtypeStruct((nb, s, FOX_WIDTH), BF16),
        compiler_params=_cparams(("arbitrary", "arbitrary", "arbitrary")),
        name="fox_attention",
    )(proj3, proj3, proj3, cum, fox_on_w.reshape(1, FOX_WIDTH))


def _outproj_kernel(ret_ref, fox_ref, wo_ref, x_ref, mod_ref, npost_ref, npre_ref, rw_ref, rb_ref,
                    x1_ref, h2_ref, idx_ref, wts_ref, cnt_ref, carry_scr):
    i = pl.program_id(0)

    @pl.when(i == 0)
    def _():
        carry_scr[...] = jnp.zeros_like(carry_scr)

    idx_ref[...] = jnp.zeros_like(idx_ref)
    subtiles = [slice(r * OUT_RC, (r + 1) * OUT_RC) for r in range(OUT_TM // OUT_RC)]
    ys = [jnp.dot(ret_ref[rs, :], wo_ref[:RET_WIDTH, :], preferred_element_type=F32)
          + jnp.dot(fox_ref[rs, :], wo_ref[RET_WIDTH:, :], preferred_element_type=F32) for rs in subtiles]
    for rs, y in zip(subtiles, ys):
        _outproj_rows(rs, y, x_ref, mod_ref, npost_ref, npre_ref, rw_ref, rb_ref, x1_ref, h2_ref, idx_ref,
                      wts_ref, cnt_ref, carry_scr)


def _outproj_rows(rs, y, x_ref, mod_ref, npost_ref, npre_ref, rw_ref, rb_ref,
                  x1_ref, h2_ref, idx_ref, wts_ref, cnt_ref, carry_scr):
    tm = OUT_RC
    gate1 = mod_ref[0, 2:3, :]
    shift2 = mod_ref[0, 3:4, :]
    scale2 = 1.0 + mod_ref[0, 4:5, :]
    x1 = x_ref[rs, :] + gate1 * (_rms_rows(y) * npost_ref[...])
    x1_ref[rs, :] = x1
    h2 = _rms_rows(x1) * npre_ref[...] * scale2 + shift2
    h2_ref[rs, :, :] = h2.astype(BF16).reshape(tm, ROW_TILES, LANES)

    hi = h2.astype(BF16)
    lo = (h2 - hi.astype(F32)).astype(BF16)
    rw = rw_ref[...]
    lg2 = jnp.dot(hi, rw, preferred_element_type=F32) + jnp.dot(lo, rw, preferred_element_type=F32)
    logits = lg2 + pltpu.roll(lg2, LANES // 2, 1) + rb_ref[...]

    lane = lax.broadcasted_iota(I32, (tm, LANES), 1)
    lanef = lane.astype(F32)
    big = float(LANES)

    def first_argmax(vals):
        vmax = jnp.max(vals, axis=1, keepdims=True)
        idx = jnp.min(jnp.where(vals == vmax, lanef, big), axis=1, keepdims=True)
        return vmax, idx

    is_coarse = (lane >= N_EXPERTS) & (lane < N_EXPERTS + N_GROUPS)
    cvals = jnp.where(is_coarse, logits, NEG)
    cmax, cidx = first_argmax(cvals)
    gidx = cidx - float(N_EXPERTS)
    gprob = 1.0 / jnp.sum(jnp.where(is_coarse, jnp.exp(cvals - cmax), 0.0), axis=1, keepdims=True)
    glo = gidx * float(EXPERTS_PER_GROUP)
    in_group = (lanef >= glo) & (lanef < glo + float(EXPERTS_PER_GROUP))
    fvals = jnp.where(in_group, logits, NEG)
    v1, e1 = first_argmax(fvals)
    fvals2 = jnp.where(lanef == e1, NEG, fvals)
    v2, e2 = first_argmax(fvals2)
    ex = jnp.exp(v2 - v1)
    w1 = gprob * (1.0 / (1.0 + ex))
    w2 = gprob * (ex / (1.0 + ex))

    sel1 = lanef == e1
    sel2 = lanef == e2
    onehot = jnp.where(sel1 | sel2, 1.0, 0.0)
    ri = lax.broadcasted_iota(I32, (tm, tm), 0)
    ci = lax.broadcasted_iota(I32, (tm, tm), 1)
    lower = jnp.where(ci < ri, 1.0, 0.0).astype(BF16)
    before = jnp.dot(lower, onehot.astype(BF16), preferred_element_type=F32) + carry_scr[...]
    r1 = jnp.sum(jnp.where(sel1, before, 0.0), axis=1, keepdims=True)
    r2 = jnp.sum(jnp.where(sel2, before, 0.0), axis=1, keepdims=True)
    carry_new = carry_scr[...] + jnp.sum(onehot, axis=0, keepdims=True)
    carry_scr[...] = carry_new
    cnt_ref[...] = carry_new.astype(I32)

    for row, col in enumerate((e1, e2, r1, r2)):
        idx_ref[row:row + 1, rs] = _col_to_row(col).astype(I32)
    wts_ref[rs, :] = jnp.where(lane == 0, w1, jnp.where(lane == 1, w2, 0.0))


def _out_projection(ret2, fox2, w_out_bf, x2, mod3, npost, npre, rw, rb, s):
    t, d = x2.shape
    per_b = s // OUT_TM
    row = lambda i: (i, 0)
    const = lambda i: (0, 0)
    return pl.pallas_call(
        _outproj_kernel,
        grid=(t // OUT_TM,),
        in_specs=[pl.BlockSpec((OUT_TM, RET_WIDTH), row),
                  pl.BlockSpec((OUT_TM, FOX_WIDTH), row),
                  pl.BlockSpec((d, d), const),
                  pl.BlockSpec((OUT_TM, d), row),
                  pl.BlockSpec((1, N_MOD, d), lambda i: (i // per_b, 0, 0)),
                  pl.BlockSpec((1, d), const),
                  pl.BlockSpec((1, d), const),
                  pl.BlockSpec((d, LANES), const),
                  pl.BlockSpec((1, LANES), const)],
        out_specs=[pl.BlockSpec((OUT_TM, d), row),
                   pl.BlockSpec((OUT_TM, ROW_TILES, LANES), lambda i: (i, 0, 0)),
                   pl.BlockSpec((SUBLANES, OUT_TM), lambda i: (0, i)),
                   pl.BlockSpec((OUT_TM, LANES), row),
                   pl.BlockSpec((1, LANES), const)],
        out_shape=[jax.ShapeDtypeStruct((t, d), F32),
                   jax.ShapeDtypeStruct((t, ROW_TILES, LANES), BF16),
                   jax.ShapeDtypeStruct((SUBLANES, t), I32),
                   jax.ShapeDtypeStruct((t, LANES), F32),
                   jax.ShapeDtypeStruct((1, LANES), I32)],
        scratch_shapes=[pltpu.VMEM((1, LANES), F32)],
        compiler_params=_cparams(("arbitrary",)),
        name="out_projection_router",
    )(ret2, fox2, w_out_bf, x2, mod3, npost.reshape(1, d), npre.reshape(1, d), rw, rb)


def _cast_kernel(w_ref, o_ref):
    o_ref[...] = w_ref[...].astype(BF16)


def _cast_bf16(w, layer, tr=256):
    _, r, c = w.shape
    return pl.pallas_call(
        _cast_kernel,
        grid=(r // tr,),
        in_specs=[pl.BlockSpec((None, tr, c), lambda i: (layer, i, 0))],
        out_specs=pl.BlockSpec((tr, c), lambda i: (i, 0)),
        out_shape=jax.ShapeDtypeStruct((r, c), BF16),
        compiler_params=_cparams(("arbitrary",)),
        name="cast_bf16",
    )(w)


PLAN_ROW_START = 0
PLAN_TILE_EXPERT = 1
PLAN_NUM_TILES = 2
PLAN_TILE_END = 3
PLAN_COUNT = 4


def _plan_kernel(idx_ref, cnt_ref, pos_ref, plan_ref):
    lane = lax.broadcasted_iota(I32, (1, LANES), 1)
    cnt = jnp.where(lane < N_EXPERTS, cnt_ref[...], 0).astype(F32)
    tiles = jnp.floor((cnt + float(EXP_TM - 1)) * (1.0 / EXP_TM))
    ri = lax.broadcasted_iota(I32, (LANES, LANES), 0)
    ci = lax.broadcasted_iota(I32, (LANES, LANES), 1)
    upper = jnp.where(ri <= ci, 1.0, 0.0).astype(BF16)
    tiles8 = jnp.broadcast_to(tiles, (SUBLANES, LANES)).astype(BF16)
    tile_end = jnp.dot(tiles8, upper, preferred_element_type=F32)[0:1, :]
    row_start = (tile_end - tiles) * float(EXP_TM)
    num_tiles = jnp.max(tile_end, axis=1, keepdims=True)
    end_col = _row_to_col(tile_end)
    owned = jnp.where((end_col <= ci.astype(F32)) & (ri < N_EXPERTS), 1.0, 0.0)
    tile_expert = jnp.sum(owned, axis=0, keepdims=True)
    last_expert = jnp.max(jnp.where(lane.astype(F32) < num_tiles, tile_expert, 0.0), axis=1, keepdims=True)
    tile_expert = jnp.minimum(tile_expert, last_expert)
    plan_ref[...] = jnp.zeros_like(plan_ref)
    plan_ref[PLAN_ROW_START:PLAN_ROW_START + 1, :] = row_start.astype(I32)
    plan_ref[PLAN_TILE_EXPERT:PLAN_TILE_EXPERT + 1, :] = tile_expert.astype(I32)
    plan_ref[PLAN_NUM_TILES:PLAN_NUM_TILES + 1, :] = jnp.broadcast_to(num_tiles, (1, LANES)).astype(I32)
    plan_ref[PLAN_TILE_END:PLAN_TILE_END + 1, :] = tile_end.astype(I32)
    plan_ref[PLAN_COUNT:PLAN_COUNT + 1, :] = cnt.astype(I32)

    experts = idx_ref[0:2, :]
    start_i = row_start.astype(I32)
    base = jnp.zeros(experts.shape, I32)
    for e in range(N_EXPERTS):
        base = jnp.where(experts == e, start_i[:, e:e + 1], base)
    pos_ref[...] = jnp.zeros_like(pos_ref)
    pos_ref[0:2, :] = idx_ref[2:4, :] + base


def _route_plan(idx, counts):
    t = idx.shape[1]
    return pl.pallas_call(
        _plan_kernel,
        out_shape=[jax.ShapeDtypeStruct((SUBLANES, t), I32),
                   jax.ShapeDtypeStruct((SUBLANES, LANES), I32)],
        compiler_params=pltpu.CompilerParams(vmem_limit_bytes=VMEM_LIMIT),
        name="route_plan",
    )(idx, counts)


def _dispatch_kernel(pa_ref, pb_ref, plan_ref, h2_hbm, xs_ref, inv, h2v, sem):
    i = pl.program_id(0)
    n_tok = h2_hbm.shape[0]
    nt = plan_ref[PLAN_NUM_TILES, 0]
    tiles_per_step = DSP_TM // EXP_TM
    table_copy = pltpu.make_async_copy(h2_hbm, h2v, sem)

    @pl.when(i == 0)
    def _():
        table_copy.start()

        def fill(p, _):
            inv[p] = 0
            return 0
        for e in range(N_EXPERTS):
            lax.fori_loop(plan_ref[PLAN_ROW_START, e] + plan_ref[PLAN_COUNT, e],
                          plan_ref[PLAN_TILE_END, e] * EXP_TM, fill, 0)

        def scatter(j, _):
            for u in range(DMA_UNROLL):
                tok = j * DMA_UNROLL + u
                inv[pa_ref[tok]] = tok
                inv[pb_ref[tok]] = tok
            return 0
        lax.fori_loop(0, n_tok // DMA_UNROLL, scatter, 0)
        table_copy.wait()

    for sub in range(tiles_per_step):
        tile = i * tiles_per_step + sub
        rows = pl.ds(sub * EXP_TM, EXP_TM)

        @pl.when(tile < nt)
        def _():
            base = tile * EXP_TM

            def body(j, _):
                for u in range(DSP_UNROLL):
                    r = j * DSP_UNROLL + u
                    xs_ref[sub * EXP_TM + r] = h2v[inv[base + r]]
                return 0
            lax.fori_loop(0, EXP_TM // DSP_UNROLL, body, 0)

        @pl.when(tile >= nt)
        def _():
            xs_ref[rows] = jnp.zeros((EXP_TM, ROW_TILES, LANES), BF16)


def _dispatch(h2t, pos_a, pos_b, plan):
    t = h2t.shape[0]
    n_rows = ((t * 2) // EXP_TM + N_EXPERTS) * EXP_TM
    return pl.pallas_call(
        _dispatch_kernel,
        grid_spec=pltpu.PrefetchScalarGridSpec(
            num_scalar_prefetch=3,
            grid=(n_rows // DSP_TM,),
            in_specs=[pl.BlockSpec(memory_space=pl.ANY)],
            out_specs=pl.BlockSpec((DSP_TM, ROW_TILES, LANES), lambda i, pa, pb, plan: (i, 0, 0)),
            scratch_shapes=[pltpu.SMEM((n_rows,), I32),
                            pltpu.VMEM((t, ROW_TILES, LANES), BF16),
                            pltpu.SemaphoreType.DMA(())]),
        out_shape=jax.ShapeDtypeStruct((n_rows, ROW_TILES, LANES), BF16),
        compiler_params=_cparams(("arbitrary",)),
        name="moe_dispatch",
    )(pos_a, pos_b, plan, h2t)


def _expert_kernel(plan_ref, xs_ref, w1_hbm, w3_hbm, w2_hbm, ys_ref,
                   seq, w1s, w3s, w2s, wsem, w1b, w3b, w2b, *, layer):
    i = pl.program_id(0)
    tm = EXP_TM
    nt = plan_ref[PLAN_NUM_TILES, 0]
    expert = plan_ref[PLAN_TILE_EXPERT, i]

    def weight_copies(e, buf):
        return (pltpu.make_async_copy(w1_hbm.at[layer, e], w1s.at[buf], wsem.at[buf, 0]),
                pltpu.make_async_copy(w3_hbm.at[layer, e], w3s.at[buf], wsem.at[buf, 1]),
                pltpu.make_async_copy(w2_hbm.at[layer, e], w2s.at[buf], wsem.at[buf, 2]))

    def weight_start(e, buf):
        for cp in weight_copies(e, buf):
            cp.start(priority=1)

    def next_used_tile(e):
        return plan_ref[PLAN_TILE_END, e]

    @pl.when(i == 0)
    def _():
        first_expert = plan_ref[PLAN_TILE_EXPERT, 0]
        weight_start(first_expert, 0)
        t1 = next_used_tile(first_expert)

        @pl.when(t1 < nt)
        def _():
            weight_start(plan_ref[PLAN_TILE_EXPERT, t1], 1)
        seq[0] = 0

    valid = i < nt
    first = (i == 0) | (expert != plan_ref[PLAN_TILE_EXPERT, jnp.maximum(i - 1, 0)])

    @pl.when(valid & first)
    def _():
        buf = seq[0] % 2
        for cp in weight_copies(expert, buf):
            cp.wait()
        w1b[...] = w1s[buf].astype(BF16)
        w3b[...] = w3s[buf].astype(BF16)
        w2b[...] = w2s[buf].astype(BF16)
        seq[0] = seq[0] + 1
        t1 = next_used_tile(expert)
        t2 = next_used_tile(plan_ref[PLAN_TILE_EXPERT, jnp.minimum(t1, LANES - 1)])

        @pl.when((t1 < nt) & (t2 < nt))
        def _():
            weight_start(plan_ref[PLAN_TILE_EXPERT, t2], buf)

    @pl.when(valid)
    def _():
        xt = xs_ref[...].reshape(tm, D_MODEL)
        a = jnp.dot(xt, w1b[...], preferred_element_type=F32)
        g = jnp.dot(xt, w3b[...], preferred_element_type=F32)
        act = (a * _sigmoid(a) * g).astype(BF16)
        y = jnp.dot(act, w2b[...], preferred_element_type=F32)
        ys_ref[...] = y.astype(BF16).reshape(tm, ROW_TILES, LANES)

    @pl.when(jnp.logical_not(valid))
    def _():
        ys_ref[...] = jnp.zeros_like(ys_ref)


def _expert_gemm(xs, plan, w1, w3, w2, layer):
    n_rows = xs.shape[0]
    n_tiles = n_rows // EXP_TM
    d, de = w1.shape[2], w1.shape[3]
    any_spec = pl.BlockSpec(memory_space=pl.ANY)

    def used_tile(i, plan):
        return (jnp.minimum(i, plan[PLAN_NUM_TILES, 0] - 1), 0, 0)

    return pl.pallas_call(
        functools.partial(_expert_kernel, layer=layer),
        grid_spec=pltpu.PrefetchScalarGridSpec(
            num_scalar_prefetch=1,
            grid=(n_tiles,),
            in_specs=[pl.BlockSpec((EXP_TM, ROW_TILES, LANES), used_tile), any_spec, any_spec, any_spec],
            out_specs=pl.BlockSpec((EXP_TM, ROW_TILES, LANES), lambda i, plan: (i, 0, 0)),
            scratch_shapes=[pltpu.SMEM((1,), I32),
                            pltpu.VMEM((2, d, de), F32),
                            pltpu.VMEM((2, d, de), F32),
                            pltpu.VMEM((2, de, d), F32),
                            pltpu.SemaphoreType.DMA((2, 3)),
                            pltpu.VMEM((d, de), BF16),
                            pltpu.VMEM((d, de), BF16),
                            pltpu.VMEM((de, d), BF16)]),
        out_shape=jax.ShapeDtypeStruct((n_rows, ROW_TILES, LANES), BF16),
        compiler_params=_cparams(("arbitrary",)),
        name="moe_expert_gemm",
    )(plan, xs, w1, w3, w2)


def _combine_kernel(pa_ref, pb_ref, ys_hbm, x1_ref, wts_ref, mod_ref, npost_ref, o_ref, buf_a, buf_b, sems):
    i = pl.program_id(0)
    n = pl.num_programs(0)
    tm = CMB_TM

    def start_tile(tile, slot):
        def body(j, _):
            for u in range(DMA_UNROLL):
                r = j * DMA_UNROLL + u
                tok = tile * tm + r
                pltpu.make_async_copy(ys_hbm.at[pa_ref[tok]], buf_a.at[slot, r], sems.at[0, slot]).start()
                pltpu.make_async_copy(ys_hbm.at[pb_ref[tok]], buf_b.at[slot, r], sems.at[1, slot]).start()
            return 0
        lax.fori_loop(0, tm // DMA_UNROLL, body, 0)

    def wait_tile(slot):
        pltpu.make_async_copy(ys_hbm.at[pl.ds(0, tm)], buf_a.at[slot], sems.at[0, slot]).wait()
        pltpu.make_async_copy(ys_hbm.at[pl.ds(0, tm)], buf_b.at[slot], sems.at[1, slot]).wait()

    slot = i % 2

    @pl.when(i == 0)
    def _():
        start_tile(0, 0)

    @pl.when(i + 1 < n)
    def _():
        start_tile(i + 1, 1 - slot)

    wait_tile(slot)

    wa = wts_ref[:, 0:1]
    wb = wts_ref[:, 1:2]
    ya = buf_a[slot].reshape(tm, D_MODEL).astype(F32)
    yb = buf_b[slot].reshape(tm, D_MODEL).astype(F32)
    y = wa * ya + wb * yb
    gate2 = mod_ref[0, 5:6, :]
    o_ref[...] = x1_ref[...] + gate2 * (_rms_rows(y) * npost_ref[...])


def _combine(ys, pos_a, pos_b, x1, wts, mod3, npost, s):
    t, d = x1.shape
    per_b = s // CMB_TM
    row = lambda i, pa, pb: (i, 0)
    return pl.pallas_call(
        _combine_kernel,
        grid_spec=pltpu.PrefetchScalarGridSpec(
            num_scalar_prefetch=2,
            grid=(t // CMB_TM,),
            in_specs=[pl.BlockSpec(memory_space=pl.ANY),
                      pl.BlockSpec((CMB_TM, d), row),
                      pl.BlockSpec((CMB_TM, LANES), row),
                      pl.BlockSpec((1, N_MOD, d), lambda i, pa, pb: (i // per_b, 0, 0)),
                      pl.BlockSpec((1, d), lambda i, pa, pb: (0, 0))],
            out_specs=pl.BlockSpec((CMB_TM, d), row),
            scratch_shapes=[pltpu.VMEM((2, CMB_TM, ROW_TILES, LANES), BF16),
                            pltpu.VMEM((2, CMB_TM, ROW_TILES, LANES), BF16),
                            pltpu.SemaphoreType.DMA((2, 2))]),
        out_shape=jax.ShapeDtypeStruct((t, d), F32),
        compiler_params=_cparams(("arbitrary",)),
        name="moe_combine",
    )(pos_a, pos_b, ys, x1, wts, mod3, npost.reshape(1, d))


def _rope_tables(s):
    inv_freq = ROPE_BASE ** (-jnp.arange(0, HEAD_DIM, 2, dtype=F32) / HEAD_DIM)
    angle = jnp.arange(s, dtype=F32)[:, None] * inv_freq[None, :]
    cos = jnp.cos(angle)
    sin = jnp.sin(angle)
    return jnp.concatenate([cos, cos], axis=-1), jnp.concatenate([-sin, sin], axis=-1)


def _router_weights(router_coarse_w, router_fine_w, router_coarse_b, router_fine_b):
    d = router_fine_w.shape[0]
    w = jnp.concatenate([router_fine_w, router_coarse_w], axis=1)
    n = w.shape[1]
    w_hi = w.astype(BF16)
    w_lo = (w - w_hi.astype(F32)).astype(BF16)
    pad = jnp.zeros((d, LANES // 2 - n), BF16)
    rw = jnp.concatenate([w_hi, pad, w_lo, pad], axis=1)
    rb = jnp.concatenate([router_fine_b, router_coarse_b, jnp.zeros((LANES - n,), F32)]).reshape(1, LANES)
    return rw, rb


def _layer(x2, c, nb, s, cosf, sinf, lg_tab, layer, ada_w, ada_b, norm_pre_mix, norm_post_mix, w_in, fox_f_bias,
           ret_gn_w, fox_qn_w, fox_kn_w, fox_on_w, w_out, norm_pre_ffn, norm_post_ffn,
           router_coarse_w, router_coarse_b, router_fine_w, router_fine_b, w1, w3, w2):
    t, d = x2.shape
    mod3 = _modulation(c, ada_w, ada_b, layer).reshape(nb, N_MOD, d)

    proj, lf = _in_projection(x2, mod3, norm_pre_mix[layer], w_in, fox_f_bias[layer], fox_qn_w[layer],
                              fox_kn_w[layer], cosf, sinf, nb, s, layer)
    cum = _cum_forget(lf)
    proj3 = proj.reshape(nb, s, MAIN_COLS)
    ret = _retention(proj3, ret_gn_w[layer], lg_tab)
    fox = _fox_attention(proj3, cum, fox_on_w[layer])

    rw, rb = _router_weights(router_coarse_w[layer], router_fine_w[layer], router_coarse_b[layer],
                             router_fine_b[layer])
    x1, h2t, idx, wts, counts = _out_projection(
        ret.reshape(t, RET_WIDTH), fox.reshape(t, FOX_WIDTH), _cast_bf16(w_out, layer), x2, mod3,
        norm_post_mix[layer], norm_pre_ffn[layer], rw, rb, s)

    pos, plan = _route_plan(idx, counts)
    pos_a, pos_b = pos[0], pos[1]
    xs = _dispatch(h2t, pos_a, pos_b, plan)
    ys = _expert_gemm(xs, plan, w1, w3, w2, layer)
    return _combine(ys, pos_a, pos_b, x1, wts, mod3, norm_post_ffn[layer], s)


def kernel(x, c, ada_w, ada_b, norm_pre_mix, norm_post_mix, w_in, fox_f_bias, ret_gn_w, fox_qn_w, fox_kn_w,
           fox_on_w, w_out, norm_pre_ffn, norm_post_ffn, router_coarse_w, router_coarse_b, router_fine_w,
           router_fine_b, w1, w3, w2):
    nb, s, d = x.shape
    depth = ada_w.shape[0]
    cosf, sinf = _rope_tables(s)
    log_gamma = jnp.log1p(-jnp.exp2(-5.0 - jnp.arange(RET_HEADS, dtype=F32)))
    lg_tab = jnp.broadcast_to(log_gamma[:, None, None], (RET_HEADS, 1, LANES))
    x2 = x.reshape(nb * s, d)
    for layer in range(depth):
        x2 = _layer(x2, c, nb, s, cosf, sinf, lg_tab, layer, ada_w, ada_b, norm_pre_mix, norm_post_mix,
                    w_in, fox_f_bias, ret_gn_w, fox_qn_w, fox_kn_w, fox_on_w, w_out, norm_pre_ffn,
                    norm_post_ffn, router_coarse_w, router_coarse_b, router_fine_w, router_fine_b, w1, w3, w2)
    return x2.reshape(nb, s, d)
```

```python
import functools

import jax
import jax.numpy as jnp
from jax import lax
from jax.experimental import pallas as pl
from jax.experimental.pallas import tpu as pltpu

F32 = jnp.float32
BF16 = jnp.bfloat16
I32 = jnp.int32

D_MODEL = 2048
HEAD_DIM = 128
RET_WIDTH = D_MODEL // 2
FOX_WIDTH = D_MODEL - RET_WIDTH
RET_HEADS = RET_WIDTH // HEAD_DIM
FOX_HEADS = FOX_WIDTH // HEAD_DIM
ROPE_BASE = 10000.0
N_GROUPS = 4
EXPERTS_PER_GROUP = 8
N_EXPERTS = N_GROUPS * EXPERTS_PER_GROUP
D_EXPERT = D_MODEL // 4
N_MOD = 6
EPS = 1e-6
MAIN_COLS = 4 * RET_WIDTH + 3 * FOX_WIDTH

LANES = 128
SUBLANES = 8
ROW_TILES = D_MODEL // LANES
NEG = -1e30
LOG2E = 1.4426950408889634

VMEM_LIMIT = 56 * 1024 * 1024

MOD_TN = 1024
IN_TM, IN_TN = 2048, 512
IN_RC = 256
IN_XC = 512
RET_CHUNK = 512
RET_HPS = 2
FOX_TQ, FOX_TK = 512, 512
FOX_HPS = 2
FOX_UNROLL = 4
OUT_TM = 512
OUT_RC = 256
EXP_TM = 256
DSP_TM = 512
DSP_UNROLL = 8
CMB_TM = 256
DMA_UNROLL = 8


def _cparams(sem):
    return pltpu.CompilerParams(dimension_semantics=sem, vmem_limit_bytes=VMEM_LIMIT)


def _nt_dot(a, b):
    return lax.dot_general(a, b, (((1,), (1,)), ((), ())), preferred_element_type=F32)


def _tn_dot(a, b):
    return lax.dot_general(a, b, (((0,), (0,)), ((), ())), preferred_element_type=F32)


def _sigmoid(x):
    return 1.0 / (1.0 + jnp.exp(-x))


def _col_to_row(col):
    n = col.shape[0]
    ri = lax.broadcasted_iota(I32, (n, n), 0)
    ci = lax.broadcasted_iota(I32, (n, n), 1)
    return jnp.sum(jnp.where(ri == ci, col, 0.0), axis=0, keepdims=True)


def _row_to_col(row):
    n = row.shape[1]
    ri = lax.broadcasted_iota(I32, (n, n), 0)
    ci = lax.broadcasted_iota(I32, (n, n), 1)
    return jnp.sum(jnp.where(ri == ci, row, 0.0), axis=1, keepdims=True)


def _mod_kernel(cb_ref, w_ref, b_ref, o_ref):
    nb = cb_ref.shape[0]
    for b in range(nb):
        cv = cb_ref[b]
        ca = cv * _sigmoid(cv)
        for j in range(MOD_TN // LANES):
            sl = slice(j * LANES, (j + 1) * LANES)
            col = jnp.sum(w_ref[:, sl] * ca, axis=0, keepdims=True)
            o_ref[b:b + 1, sl] = col + b_ref[:, sl]


def _modulation(c, ada_w, ada_b, layer):
    nb, d = c.shape
    n = ada_w.shape[2]
    cb = jnp.broadcast_to(c[:, :, None], (nb, d, LANES))
    return pl.pallas_call(
        _mod_kernel,
        grid=(n // MOD_TN,),
        in_specs=[pl.BlockSpec((nb, d, LANES), lambda j: (0, 0, 0)),
                  pl.BlockSpec((None, d, MOD_TN), lambda j: (layer, 0, j)),
                  pl.BlockSpec((None, 1, MOD_TN), lambda j: (layer, 0, j))],
        out_specs=pl.BlockSpec((nb, MOD_TN), lambda j: (0, j)),
        out_shape=jax.ShapeDtypeStruct((nb, n), F32),
        compiler_params=_cparams(("arbitrary",)),
        name="adaln_mod",
    )(cb, ada_w, ada_b.reshape(ada_b.shape[0], 1, n))


def _rms_rows(x):
    return x * lax.rsqrt(jnp.mean(x * x, axis=-1, keepdims=True) + EPS)


def _inproj_kernel(x_hbm, mod_ref, nw_ref, w_ref, wf_ref, fb_ref, cos_ref, sin_ref, qn_ref, kn_ref,
                   o_ref, lf_ref, h_scr, wb_scr, xbuf, xsem):
    i = pl.program_id(0)
    j = pl.program_id(1)
    tiles_per_sec = RET_WIDTH // IN_TN
    heads_per_tile = IN_TN // HEAD_DIM

    @pl.when(j == 0)
    def _():
        shift = mod_ref[0, 0:1, :]
        scale = 1.0 + mod_ref[0, 1:2, :]
        nw = nw_ref[...]
        n_chunks = IN_TM // IN_XC

        def x_copy(c):
            rows = pl.ds(i * IN_TM + c * IN_XC, IN_XC)
            return pltpu.make_async_copy(x_hbm.at[rows, :], xbuf.at[c % 2], xsem.at[c % 2])

        x_copy(0).start()
        for c in range(n_chunks):
            if c + 1 < n_chunks:
                x_copy(c + 1).start()
            x_copy(c).wait()
            rows = 128
            for r in range(IN_XC // rows):
                h = _rms_rows(xbuf[c % 2, r * rows:(r + 1) * rows, :]) * nw * scale + shift
                h_scr[c * IN_XC + r * rows:c * IN_XC + (r + 1) * rows, :] = h.astype(BF16)
        z = _nt_dot(wf_ref[...], h_scr[...]) + fb_ref[...]
        lf_ref[0] = jnp.minimum(z, 0.0) - jnp.log1p(jnp.exp(-jnp.abs(z)))

    wb_scr[...] = w_ref[...].astype(BF16)
    sec = j // tiles_per_sec

    def section(epilogue):
        for r in range(IN_TM // IN_RC):
            rs = slice(r * IN_RC, (r + 1) * IN_RC)
            acc = _nt_dot(h_scr[rs, :], wb_scr[...])
            epilogue(acc, rs)

    def per_head(fn):
        def epilogue(acc, rs):
            for hh in range(heads_per_tile):
                sl = slice(hh * HEAD_DIM, (hh + 1) * HEAD_DIM)
                o_ref[rs, sl] = fn(acc[:, sl], rs).astype(BF16)
        return epilogue

    def whole(fn):
        def epilogue(acc, rs):
            o_ref[rs, :] = fn(acc).astype(BF16)
        return epilogue

    def rope(v, rs):
        return v * cos_ref[rs, :] + pltpu.roll(v, HEAD_DIM // 2, 1) * sin_ref[rs, :]

    @pl.when(sec == 0)
    def _():
        section(per_head(rope))

    @pl.when(sec == 1)
    def _():
        section(per_head(lambda v, rs: rope(v, rs) * (HEAD_DIM ** -0.5)))

    @pl.when((sec == 2) | (sec == 6))
    def _():
        section(whole(lambda a: a))

    @pl.when(sec == 3)
    def _():
        section(whole(lambda a: a * _sigmoid(a)))

    @pl.when(sec == 4)
    def _():
        section(per_head(lambda v, rs: _rms_rows(v) * qn_ref[...] * (LOG2E * HEAD_DIM ** -0.5)))

    @pl.when(sec == 5)
    def _():
        section(per_head(lambda v, rs: _rms_rows(v) * kn_ref[...]))


def _in_projection(x2, mod3, norm_w, w_in, fox_f_bias, fox_qn_w, fox_kn_w, cosf, sinf, nb, s, layer):
    t, d = x2.shape
    w_t = jnp.swapaxes(w_in, 1, 2)
    wf_t = w_t[layer, MAIN_COLS:, :].astype(BF16)
    per_b = s // IN_TM
    grid = (t // IN_TM, MAIN_COLS // IN_TN)
    return pl.pallas_call(
        _inproj_kernel,
        grid=grid,
        in_specs=[
            pl.BlockSpec(memory_space=pl.ANY),
            pl.BlockSpec((1, N_MOD, d), lambda i, j: (i // per_b, 0, 0)),
            pl.BlockSpec((1, d), lambda i, j: (0, 0)),
            pl.BlockSpec((None, IN_TN, d), lambda i, j: (layer, j, 0)),
            pl.BlockSpec((FOX_HEADS, d), lambda i, j: (0, 0)),
            pl.BlockSpec((FOX_HEADS, 1), lambda i, j: (0, 0)),
            pl.BlockSpec((IN_TM, HEAD_DIM), lambda i, j: (i % per_b, 0)),
            pl.BlockSpec((IN_TM, HEAD_DIM), lambda i, j: (i % per_b, 0)),
            pl.BlockSpec((1, HEAD_DIM), lambda i, j: (0, 0)),
            pl.BlockSpec((1, HEAD_DIM), lambda i, j: (0, 0)),
        ],
        out_specs=[
            pl.BlockSpec((IN_TM, IN_TN), lambda i, j: (i, j)),
            pl.BlockSpec((1, FOX_HEADS, IN_TM), lambda i, j: (i // per_b, 0, i % per_b)),
        ],
        out_shape=[jax.ShapeDtypeStruct((t, MAIN_COLS), BF16),
                   jax.ShapeDtypeStruct((nb, FOX_HEADS, s), F32)],
        scratch_shapes=[pltpu.VMEM((IN_TM, d), BF16), pltpu.VMEM((IN_TN, d), BF16),
                        pltpu.VMEM((2, IN_XC, d), F32), pltpu.SemaphoreType.DMA((2,))],
        compiler_params=_cparams(("arbitrary", "arbitrary")),
        name="in_projection",
    )(x2, mod3, norm_w.reshape(1, d), w_t, wf_t, fox_f_bias.reshape(FOX_HEADS, 1), cosf, sinf,
      fox_qn_w.reshape(1, HEAD_DIM), fox_kn_w.reshape(1, HEAD_DIM))


def _split3(x):
    hi = x.astype(BF16)
    r1 = x - hi.astype(F32)
    mid = r1.astype(BF16)
    lo = (r1 - mid.astype(F32)).astype(BF16)
    return hi, mid, lo


def _cumsum_kernel(x_ref, o_ref, *, rows_per_seq):
    x = x_ref[...]
    r = x.shape[0]
    ri = lax.broadcasted_iota(I32, (LANES, LANES), 0)
    ci = lax.broadcasted_iota(I32, (LANES, LANES), 1)
    upper = jnp.where(ri <= ci, 1.0, 0.0).astype(BF16)
    rowcum = sum(jnp.dot(p, upper, preferred_element_type=F32) for p in _split3(x))
    tot = jnp.broadcast_to(rowcum[:, LANES - 1:LANES], (r, LANES))
    gi = lax.broadcasted_iota(I32, (r, r), 0)
    gj = lax.broadcasted_iota(I32, (r, r), 1)
    same_seq = (gi // rows_per_seq) == (gj // rows_per_seq)
    lower = jnp.where(same_seq & (gj < gi), 1.0, 0.0).astype(BF16)
    prefix = sum(jnp.dot(lower, p, preferred_element_type=F32) for p in _split3(tot))
    o_ref[...] = rowcum + prefix


def _cum_forget(lf):
    nb, nh, s = lf.shape
    rows = nb * nh * s // LANES
    out = pl.pallas_call(
        functools.partial(_cumsum_kernel, rows_per_seq=s // LANES),
        out_shape=jax.ShapeDtypeStruct((rows, LANES), F32),
        compiler_params=pltpu.CompilerParams(vmem_limit_bytes=VMEM_LIMIT),
        name="forget_cumsum",
    )(lf.reshape(rows, LANES))
    return out.reshape(nb * nh, 1, s)


def _ret_kernel(lg_ref, q_ref, k_ref, v_ref, g_ref, gnw_ref, o_ref):
    c = RET_CHUNK
    s = q_ref.shape[1]
    ri = lax.broadcasted_iota(I32, (c, c), 0)
    ci = lax.broadcasted_iota(I32, (c, c), 1)
    diff = (ri - ci).astype(F32)
    pos = lax.broadcasted_iota(I32, (c, 1), 0).astype(F32)
    consts = []
    for hh in range(RET_HPS):
        lg = lg_ref[hh][:, 0:1]
        decay = jnp.where(diff >= 0.0, jnp.exp(lg * jnp.maximum(diff, 0.0)), 0.0)
        zeta = jnp.exp(lg * (float(c - 1) - pos))
        xi = jnp.exp(lg * (pos + 1.0))
        chunk_decay = jnp.exp(lg * float(c))
        consts.append((decay, zeta, xi, chunk_decay))

    def body(n, states):
        r0 = n * c
        new_states = []
        for hh in range(RET_HPS):
            decay, zeta, xi, chunk_decay = consts[hh]
            sl = slice(hh * HEAD_DIM, (hh + 1) * HEAD_DIM)
            q = q_ref[0, pl.ds(r0, c), sl]
            k = k_ref[0, pl.ds(r0, c), sl]
            v = v_ref[0, pl.ds(r0, c), sl]
            scores = _nt_dot(q, k) * decay
            intra = jnp.dot(scores.astype(BF16), v, preferred_element_type=F32)
            cross = jnp.dot(q, states[hh].astype(BF16), preferred_element_type=F32) * xi
            kz = (k.astype(F32) * zeta).astype(BF16)
            kv = _tn_dot(kz, v)
            o = intra + cross
            mu = jnp.mean(o, axis=-1, keepdims=True)
            oc = o - mu
            var = jnp.mean(oc * oc, axis=-1, keepdims=True)
            y = oc * lax.rsqrt(var + EPS) * gnw_ref[:, sl] * g_ref[0, pl.ds(r0, c), sl].astype(F32)
            o_ref[0, pl.ds(r0, c), sl] = y.astype(BF16)
            new_states.append(states[hh] * chunk_decay + kv)
        return tuple(new_states)

    states = tuple(jnp.zeros((HEAD_DIM, HEAD_DIM), F32) for _ in range(RET_HPS))
    for n in range(s // c):
        states = body(n, states)


def _retention(proj3, ret_gn_w, lg_tab):
    nb, s, _ = proj3.shape
    w = RET_HPS * HEAD_DIM
    gpg = RET_WIDTH // w

    def col(sec):
        return pl.BlockSpec((1, s, w), lambda b, h: (b, 0, sec * gpg + h))

    return pl.pallas_call(
        _ret_kernel,
        grid=(nb, gpg),
        in_specs=[pl.BlockSpec((RET_HPS, 1, LANES), lambda b, h: (h, 0, 0)),
                  col(0), col(1), col(2), col(3),
                  pl.BlockSpec((1, w), lambda b, h: (0, h))],
        out_specs=pl.BlockSpec((1, s, w), lambda b, h: (b, 0, h)),
        out_shape=jax.ShapeDtypeStruct((nb, s, RET_WIDTH), BF16),
        compiler_params=_cparams(("arbitrary", "arbitrary")),
        name="retention",
    )(lg_tab, proj3, proj3, proj3, proj3, ret_gn_w.reshape(1, RET_WIDTH))


def _fox_kernel(q_ref, k_ref, v_ref, cum_ref, onw_ref, o_ref):
    tq, t = FOX_TQ, FOX_TK
    for qi in range(k_ref.shape[1] // tq):
        @pl.when(pl.program_id(2) == qi)
        def _():
            _fox_query_tile(qi, q_ref, k_ref, v_ref, cum_ref, onw_ref, o_ref)


def _fox_query_tile(qi, q_ref, k_ref, v_ref, cum_ref, onw_ref, o_ref):
    tq, t = FOX_TQ, FOX_TK
    q0 = qi * tq
    last = q0 // t
    rows = slice(None)
    ri = lax.broadcasted_iota(I32, (tq, t), 0)
    ci = lax.broadcasted_iota(I32, (tq, t), 1)

    def step(ki, carry, masked):
        k0 = ki * t
        out = []
        for hh in range(FOX_HPS):
            m, l, acc = carry[hh]
            sl = slice(hh * HEAD_DIM, (hh + 1) * HEAD_DIM)
            k = k_ref[0, k0:k0 + t, sl]
            v = v_ref[0, k0:k0 + t, sl]
            sc = _nt_dot(q_ref[0, rows, sl], k) - cum_ref[hh, :, k0:k0 + t] * LOG2E
            if masked:
                sc = jnp.where(ci + k0 <= ri + q0, sc, NEG)
            m_new = jnp.maximum(m, jnp.max(sc, axis=1, keepdims=True))
            p = jnp.exp2(sc - m_new)
            alpha = jnp.exp2(m - m_new)
            l = alpha * l + jnp.sum(p, axis=1, keepdims=True)
            acc = alpha * acc + jnp.dot(p.astype(BF16), v, preferred_element_type=F32)
            out.append((m_new, l, acc))
        return tuple(out)

    init = tuple((jnp.full((tq, 1), NEG, F32), jnp.zeros((tq, 1), F32), jnp.zeros((tq, HEAD_DIM), F32))
                 for _ in range(FOX_HPS))
    carry = init
    for ki in range(last):
        carry = step(ki, carry, False)
    carry = step(last, carry, True)
    for hh in range(FOX_HPS):
        _, l, acc = carry[hh]
        sl = slice(hh * HEAD_DIM, (hh + 1) * HEAD_DIM)
        o = acc / l
        o_ref[0, rows, sl] = (_rms_rows(o) * onw_ref[:, sl]).astype(BF16)


def _fox_attention(proj3, cum, fox_on_w):
    nb, s, _ = proj3.shape
    w = FOX_HPS * HEAD_DIM
    base = 4 * RET_WIDTH // w
    gpg = FOX_WIDTH // w
    return pl.pallas_call(
        _fox_kernel,
        grid=(nb, gpg, s // FOX_TQ),
        in_specs=[pl.BlockSpec((1, FOX_TQ, w), lambda b, h, i: (b, i, base + h)),
                  pl.BlockSpec((1, s, w), lambda b, h, i: (b, 0, base + gpg + h)),
                  pl.BlockSpec((1, s, w), lambda b, h, i: (b, 0, base + 2 * gpg + h)),
                  pl.BlockSpec((FOX_HPS, 1, s), lambda b, h, i: (b * gpg + h, 0, 0)),
                  pl.BlockSpec((1, w), lambda b, h, i: (0, h))],
        out_specs=pl.BlockSpec((1, FOX_TQ, w), lambda b, h, i: (b, i, h)),
        out_shape=jax.ShapeDtypeStruct((nb, s, FOX_WIDTH), BF16),
        compiler_params=_cparams(("arbitrary", "arbitrary", "arbitrary")),
        name="fox_attention",
    )(proj3, proj3, proj3, cum, fox_on_w.reshape(1, FOX_WIDTH))


def _outproj_kernel(ret_ref, fox_ref, wo_ref, x_ref, mod_ref, npost_ref, npre_ref, rw_ref, rb_ref,
                    x1_ref, h2_ref, idx_ref, wts_ref, cnt_ref, carry_scr):
    i = pl.program_id(0)

    @pl.when(i == 0)
    def _():
        carry_scr[...] = jnp.zeros_like(carry_scr)

    idx_ref[...] = jnp.zeros_like(idx_ref)
    subtiles = [slice(r * OUT_RC, (r + 1) * OUT_RC) for r in range(OUT_TM // OUT_RC)]
    ys = [jnp.dot(ret_ref[rs, :], wo_ref[:RET_WIDTH, :], preferred_element_type=F32)
          + jnp.dot(fox_ref[rs, :], wo_ref[RET_WIDTH:, :], preferred_element_type=F32) for rs in subtiles]
    for rs, y in zip(subtiles, ys):
        _outproj_rows(rs, y, x_ref, mod_ref, npost_ref, npre_ref, rw_ref, rb_ref, x1_ref, h2_ref, idx_ref,
                      wts_ref, cnt_ref, carry_scr)


def _outproj_rows(rs, y, x_ref, mod_ref, npost_ref, npre_ref, rw_ref, rb_ref,
                  x1_ref, h2_ref, idx_ref, wts_ref, cnt_ref, carry_scr):
    tm = OUT_RC
    gate1 = mod_ref[0, 2:3, :]
    shift2 = mod_ref[0, 3:4, :]
    scale2 = 1.0 + mod_ref[0, 4:5, :]
    x1 = x_ref[rs, :] + gate1 * (_rms_rows(y) * npost_ref[...])
    x1_ref[rs, :] = x1
    h2 = _rms_rows(x1) * npre_ref[...] * scale2 + shift2
    h2_ref[rs, :, :] = h2.astype(BF16).reshape(tm, ROW_TILES, LANES)

    hi = h2.astype(BF16)
    lo = (h2 - hi.astype(F32)).astype(BF16)
    rw = rw_ref[...]
    lg2 = jnp.dot(hi, rw, preferred_element_type=F32) + jnp.dot(lo, rw, preferred_element_type=F32)
    logits = lg2 + pltpu.roll(lg2, LANES // 2, 1) + rb_ref[...]

    lane = lax.broadcasted_iota(I32, (tm, LANES), 1)
    lanef = lane.astype(F32)
    big = float(LANES)

    def first_argmax(vals):
        vmax = jnp.max(vals, axis=1, keepdims=True)
        idx = jnp.min(jnp.where(vals == vmax, lanef, big), axis=1, keepdims=True)
        return vmax, idx

    is_coarse = (lane >= N_EXPERTS) & (lane < N_EXPERTS + N_GROUPS)
    cvals = jnp.where(is_coarse, logits, NEG)
    cmax, cidx = first_argmax(cvals)
    gidx = cidx - float(N_EXPERTS)
    gprob = 1.0 / jnp.sum(jnp.where(is_coarse, jnp.exp(cvals - cmax), 0.0), axis=1, keepdims=True)
    glo = gidx * float(EXPERTS_PER_GROUP)
    in_group = (lanef >= glo) & (lanef < glo + float(EXPERTS_PER_GROUP))
    fvals = jnp.where(in_group, logits, NEG)
    v1, e1 = first_argmax(fvals)
    fvals2 = jnp.where(lanef == e1, NEG, fvals)
    v2, e2 = first_argmax(fvals2)
    ex = jnp.exp(v2 - v1)
    w1 = gprob * (1.0 / (1.0 + ex))
    w2 = gprob * (ex / (1.0 + ex))

    sel1 = lanef == e1
    sel2 = lanef == e2
    onehot = jnp.where(sel1 | sel2, 1.0, 0.0)
    ri = lax.broadcasted_iota(I32, (tm, tm), 0)
    ci = lax.broadcasted_iota(I32, (tm, tm), 1)
    lower = jnp.where(ci < ri, 1.0, 0.0).astype(BF16)
    before = jnp.dot(lower, onehot.astype(BF16), preferred_element_type=F32) + carry_scr[...]
    r1 = jnp.sum(jnp.where(sel1, before, 0.0), axis=1, keepdims=True)
    r2 = jnp.sum(jnp.where(sel2, before, 0.0), axis=1, keepdims=True)
    carry_new = carry_scr[...] + jnp.sum(onehot, axis=0, keepdims=True)
    carry_scr[...] = carry_new
    cnt_ref[...] = carry_new.astype(I32)

    for row, col in enumerate((e1, e2, r1, r2)):
        idx_ref[row:row + 1, rs] = _col_to_row(col).astype(I32)
    wts_ref[rs, :] = jnp.where(lane == 0, w1, jnp.where(lane == 1, w2, 0.0))


def _out_projection(ret2, fox2, w_out_bf, x2, mod3, npost, npre, rw, rb, s):
    t, d = x2.shape
    per_b = s // OUT_TM
    row = lambda i: (i, 0)
    const = lambda i: (0, 0)
    return pl.pallas_call(
        _outproj_kernel,
        grid=(t // OUT_TM,),
        in_specs=[pl.BlockSpec((OUT_TM, RET_WIDTH), row),
                  pl.BlockSpec((OUT_TM, FOX_WIDTH), row),
                  pl.BlockSpec((d, d), const),
                  pl.BlockSpec((OUT_TM, d), row),
                  pl.BlockSpec((1, N_MOD, d), lambda i: (i // per_b, 0, 0)),
                  pl.BlockSpec((1, d), const),
                  pl.BlockSpec((1, d), const),
                  pl.BlockSpec((d, LANES), const),
                  pl.BlockSpec((1, LANES), const)],
        out_specs=[pl.BlockSpec((OUT_TM, d), row),
                   pl.BlockSpec((OUT_TM, ROW_TILES, LANES), lambda i: (i, 0, 0)),
                   pl.BlockSpec((SUBLANES, OUT_TM), lambda i: (0, i)),
                   pl.BlockSpec((OUT_TM, LANES), row),
                   pl.BlockSpec((1, LANES), const)],
        out_shape=[jax.ShapeDtypeStruct((t, d), F32),
                   jax.ShapeDtypeStruct((t, ROW_TILES, LANES), BF16),
                   jax.ShapeDtypeStruct((SUBLANES, t), I32),
                   jax.ShapeDtypeStruct((t, LANES), F32),
                   jax.ShapeDtypeStruct((1, LANES), I32)],
        scratch_shapes=[pltpu.VMEM((1, LANES), F32)],
        compiler_params=_cparams(("arbitrary",)),
        name="out_projection_router",
    )(ret2, fox2, w_out_bf, x2, mod3, npost.reshape(1, d), npre.reshape(1, d), rw, rb)


def _cast_kernel(w_ref, o_ref):
    o_ref[...] = w_ref[...].astype(BF16)


def _cast_bf16(w, layer, tr=256):
    _, r, c = w.shape
    return pl.pallas_call(
        _cast_kernel,
        grid=(r // tr,),
        in_specs=[pl.BlockSpec((None, tr, c), lambda i: (layer, i, 0))],
        out_specs=pl.BlockSpec((tr, c), lambda i: (i, 0)),
        out_shape=jax.ShapeDtypeStruct((r, c), BF16),
        compiler_params=_cparams(("arbitrary",)),
        name="cast_bf16",
    )(w)


PLAN_ROW_START = 0
PLAN_TILE_EXPERT = 1
PLAN_NUM_TILES = 2
PLAN_TILE_END = 3
PLAN_COUNT = 4


def _plan_kernel(idx_ref, cnt_ref, pos_ref, plan_ref):
    lane = lax.broadcasted_iota(I32, (1, LANES), 1)
    cnt = jnp.where(lane < N_EXPERTS, cnt_ref[...], 0).astype(F32)
    tiles = jnp.floor((cnt + float(EXP_TM - 1)) * (1.0 / EXP_TM))
    ri = lax.broadcasted_iota(I32, (LANES, LANES), 0)
    ci = lax.broadcasted_iota(I32, (LANES, LANES), 1)
    upper = jnp.where(ri <= ci, 1.0, 0.0).astype(BF16)
    tiles8 = jnp.broadcast_to(tiles, (SUBLANES, LANES)).astype(BF16)
    tile_end = jnp.dot(tiles8, upper, preferred_element_type=F32)[0:1, :]
    row_start = (tile_end - tiles) * float(EXP_TM)
    num_tiles = jnp.max(tile_end, axis=1, keepdims=True)
    end_col = _row_to_col(tile_end)
    owned = jnp.where((end_col <= ci.astype(F32)) & (ri < N_EXPERTS), 1.0, 0.0)
    tile_expert = jnp.sum(owned, axis=0, keepdims=True)
    last_expert = jnp.max(jnp.where(lane.astype(F32) < num_tiles, tile_expert, 0.0), axis=1, keepdims=True)
    tile_expert = jnp.minimum(tile_expert, last_expert)
    plan_ref[...] = jnp.zeros_like(plan_ref)
    plan_ref[PLAN_ROW_START:PLAN_ROW_START + 1, :] = row_start.astype(I32)
    plan_ref[PLAN_TILE_EXPERT:PLAN_TILE_EXPERT + 1, :] = tile_expert.astype(I32)
    plan_ref[PLAN_NUM_TILES:PLAN_NUM_TILES + 1, :] = jnp.broadcast_to(num_tiles, (1, LANES)).astype(I32)
    plan_ref[PLAN_TILE_END:PLAN_TILE_END + 1, :] = tile_end.astype(I32)
    plan_ref[PLAN_COUNT:PLAN_COUNT + 1, :] = cnt.astype(I32)

    experts = idx_ref[0:2, :]
    start_i = row_start.astype(I32)
    base = jnp.zeros(experts.shape, I32)
    for e in range(N_EXPERTS):
        base = jnp.where(experts == e, start_i[:, e:e + 1], base)
    pos_ref[...] = jnp.zeros_like(pos_ref)
    pos_ref[0:2, :] = idx_ref[2:4, :] + base


def _route_plan(idx, counts):
    t = idx.shape[1]
    return pl.pallas_call(
        _plan_kernel,
        out_shape=[jax.ShapeDtypeStruct((SUBLANES, t), I32),
                   jax.ShapeDtypeStruct((SUBLANES, LANES), I32)],
        compiler_params=pltpu.CompilerParams(vmem_limit_bytes=VMEM_LIMIT),
        name="route_plan",
    )(idx, counts)


def _dispatch_kernel(pa_ref, pb_ref, plan_ref, h2_hbm, xs_ref, inv, h2v, sem):
    i = pl.program_id(0)
    n_tok = h2_hbm.shape[0]
    nt = plan_ref[PLAN_NUM_TILES, 0]
    tiles_per_step = DSP_TM // EXP_TM
    table_copy = pltpu.make_async_copy(h2_hbm, h2v, sem)

    @pl.when(i == 0)
    def _():
        table_copy.start()

        def fill(p, _):
            inv[p] = 0
            return 0
        for e in range(N_EXPERTS):
            lax.fori_loop(plan_ref[PLAN_ROW_START, e] + plan_ref[PLAN_COUNT, e],
                          plan_ref[PLAN_TILE_END, e] * EXP_TM, fill, 0)

        def scatter(j, _):
            for u in range(DMA_UNROLL):
                tok = j * DMA_UNROLL + u
                inv[pa_ref[tok]] = tok
                inv[pb_ref[tok]] = tok
            return 0
        lax.fori_loop(0, n_tok // DMA_UNROLL, scatter, 0)
        table_copy.wait()

    for sub in range(tiles_per_step):
        tile = i * tiles_per_step + sub
        rows = pl.ds(sub * EXP_TM, EXP_TM)

        @pl.when(tile < nt)
        def _():
            base = tile * EXP_TM

            def body(j, _):
                for u in range(DSP_UNROLL):
                    r = j * DSP_UNROLL + u
                    xs_ref[sub * EXP_TM + r] = h2v[inv[base + r]]
                return 0
            lax.fori_loop(0, EXP_TM // DSP_UNROLL, body, 0)

        @pl.when(tile >= nt)
        def _():
            xs_ref[rows] = jnp.zeros((EXP_TM, ROW_TILES, LANES), BF16)


def _dispatch(h2t, pos_a, pos_b, plan):
    t = h2t.shape[0]
    n_rows = ((t * 2) // EXP_TM + N_EXPERTS) * EXP_TM
    return pl.pallas_call(
        _dispatch_kernel,
        grid_spec=pltpu.PrefetchScalarGridSpec(
            num_scalar_prefetch=3,
            grid=(n_rows // DSP_TM,),
            in_specs=[pl.BlockSpec(memory_space=pl.ANY)],
            out_specs=pl.BlockSpec((DSP_TM, ROW_TILES, LANES), lambda i, pa, pb, plan: (i, 0, 0)),
            scratch_shapes=[pltpu.SMEM((n_rows,), I32),
                            pltpu.VMEM((t, ROW_TILES, LANES), BF16),
                            pltpu.SemaphoreType.DMA(())]),
        out_shape=jax.ShapeDtypeStruct((n_rows, ROW_TILES, LANES), BF16),
        compiler_params=_cparams(("arbitrary",)),
        name="moe_dispatch",
    )(pos_a, pos_b, plan, h2t)


def _expert_kernel(plan_ref, xs_ref, w1_hbm, w3_hbm, w2_hbm, ys_ref,
                   seq, w1s, w3s, w2s, wsem, w1b, w3b, w2b, *, layer):
    i = pl.program_id(0)
    tm = EXP_TM
    nt = plan_ref[PLAN_NUM_TILES, 0]
    expert = plan_ref[PLAN_TILE_EXPERT, i]

    def weight_copies(e, buf):
        return (pltpu.make_async_copy(w1_hbm.at[layer, e], w1s.at[buf], wsem.at[buf, 0]),
                pltpu.make_async_copy(w3_hbm.at[layer, e], w3s.at[buf], wsem.at[buf, 1]),
                pltpu.make_async_copy(w2_hbm.at[layer, e], w2s.at[buf], wsem.at[buf, 2]))

    def weight_start(e, buf):
        for cp in weight_copies(e, buf):
            cp.start(priority=1)

    def next_used_tile(e):
        return plan_ref[PLAN_TILE_END, e]

    @pl.when(i == 0)
    def _():
        first_expert = plan_ref[PLAN_TILE_EXPERT, 0]
        weight_start(first_expert, 0)
        t1 = next_used_tile(first_expert)

        @pl.when(t1 < nt)
        def _():
            weight_start(plan_ref[PLAN_TILE_EXPERT, t1], 1)
        seq[0] = 0

    valid = i < nt
    first = (i == 0) | (expert != plan_ref[PLAN_TILE_EXPERT, jnp.maximum(i - 1, 0)])

    @pl.when(valid & first)
    def _():
        buf = seq[0] % 2
        for cp in weight_copies(expert, buf):
            cp.wait()
        w1b[...] = w1s[buf].astype(BF16)
        w3b[...] = w3s[buf].astype(BF16)
        w2b[...] = w2s[buf].astype(BF16)
        seq[0] = seq[0] + 1
        t1 = next_used_tile(expert)
        t2 = next_used_tile(plan_ref[PLAN_TILE_EXPERT, jnp.minimum(t1, LANES - 1)])

        @pl.when((t1 < nt) & (t2 < nt))
        def _():
            weight_start(plan_ref[PLAN_TILE_EXPERT, t2], buf)

    @pl.when(valid)
    def _():
        xt = xs_ref[...].reshape(tm, D_MODEL)
        a = jnp.dot(xt, w1b[...], preferred_element_type=F32)
        g = jnp.dot(xt, w3b[...], preferred_element_type=F32)
        act = (a * _sigmoid(a) * g).astype(BF16)
        y = jnp.dot(act, w2b[...], preferred_element_type=F32)
        ys_ref[...] = y.astype(BF16).reshape(tm, ROW_TILES, LANES)

    @pl.when(jnp.logical_not(valid))
    def _():
        ys_ref[...] = jnp.zeros_like(ys_ref)


def _expert_gemm(xs, plan, w1, w3, w2, layer):
    n_rows = xs.shape[0]
    n_tiles = n_rows // EXP_TM
    d, de = w1.shape[2], w1.shape[3]
    any_spec = pl.BlockSpec(memory_space=pl.ANY)

    def used_tile(i, plan):
        return (jnp.minimum(i, plan[PLAN_NUM_TILES, 0] - 1), 0, 0)

    return pl.pallas_call(
        functools.partial(_expert_kernel, layer=layer),
        grid_spec=pltpu.PrefetchScalarGridSpec(
            num_scalar_prefetch=1,
            grid=(n_tiles,),
            in_specs=[pl.BlockSpec((EXP_TM, ROW_TILES, LANES), used_tile), any_spec, any_spec, any_spec],
            out_specs=pl.BlockSpec((EXP_TM, ROW_TILES, LANES), lambda i, plan: (i, 0, 0)),
            scratch_shapes=[pltpu.SMEM((1,), I32),
                            pltpu.VMEM((2, d, de), F32),
                            pltpu.VMEM((2, d, de), F32),
                            pltpu.VMEM((2, de, d), F32),
                            pltpu.SemaphoreType.DMA((2, 3)),
                            pltpu.VMEM((d, de), BF16),
                            pltpu.VMEM((d, de), BF16),
                            pltpu.VMEM((de, d), BF16)]),
        out_shape=jax.ShapeDtypeStruct((n_rows, ROW_TILES, LANES), BF16),
        compiler_params=_cparams(("arbitrary",)),
        name="moe_expert_gemm",
    )(plan, xs, w1, w3, w2)


def _combine_kernel(pa_ref, pb_ref, ys_hbm, x1_ref, wts_ref, mod_ref, npost_ref, o_ref, buf_a, buf_b, sems):
    i = pl.program_id(0)
    n = pl.num_programs(0)
    tm = CMB_TM

    def start_tile(tile, slot):
        def body(j, _):
            for u in range(DMA_UNROLL):
                r = j * DMA_UNROLL + u
                tok = tile * tm + r
                pltpu.make_async_copy(ys_hbm.at[pa_ref[tok]], buf_a.at[slot, r], sems.at[0, slot]).start()
                pltpu.make_async_copy(ys_hbm.at[pb_ref[tok]], buf_b.at[slot, r], sems.at[1, slot]).start()
            return 0
        lax.fori_loop(0, tm // DMA_UNROLL, body, 0)

    def wait_tile(slot):
        pltpu.make_async_copy(ys_hbm.at[pl.ds(0, tm)], buf_a.at[slot], sems.at[0, slot]).wait()
        pltpu.make_async_copy(ys_hbm.at[pl.ds(0, tm)], buf_b.at[slot], sems.at[1, slot]).wait()

    slot = i % 2

    @pl.when(i == 0)
    def _():
        start_tile(0, 0)

    @pl.when(i + 1 < n)
    def _():
        start_tile(i + 1, 1 - slot)

    wait_tile(slot)

    wa = wts_ref[:, 0:1]
    wb = wts_ref[:, 1:2]
    ya = buf_a[slot].reshape(tm, D_MODEL).astype(F32)
    yb = buf_b[slot].reshape(tm, D_MODEL).astype(F32)
    y = wa * ya + wb * yb
    gate2 = mod_ref[0, 5:6, :]
    o_ref[...] = x1_ref[...] + gate2 * (_rms_rows(y) * npost_ref[...])


def _combine(ys, pos_a, pos_b, x1, wts, mod3, npost, s):
    t, d = x1.shape
    per_b = s // CMB_TM
    row = lambda i, pa, pb: (i, 0)
    return pl.pallas_call(
        _combine_kernel,
        grid_spec=pltpu.PrefetchScalarGridSpec(
            num_scalar_prefetch=2,
            grid=(t // CMB_TM,),
            in_specs=[pl.BlockSpec(memory_space=pl.ANY),
                      pl.BlockSpec((CMB_TM, d), row),
                      pl.BlockSpec((CMB_TM, LANES), row),
                      pl.BlockSpec((1, N_MOD, d), lambda i, pa, pb: (i // per_b, 0, 0)),
                      pl.BlockSpec((1, d), lambda i, pa, pb: (0, 0))],
            out_specs=pl.BlockSpec((CMB_TM, d), row),
            scratch_shapes=[pltpu.VMEM((2, CMB_TM, ROW_TILES, LANES), BF16),
                            pltpu.VMEM((2, CMB_TM, ROW_TILES, LANES), BF16),
                            pltpu.SemaphoreType.DMA((2, 2))]),
        out_shape=jax.ShapeDtypeStruct((t, d), F32),
        compiler_params=_cparams(("arbitrary",)),
        name="moe_combine",
    )(pos_a, pos_b, ys, x1, wts, mod3, npost.reshape(1, d))


def _rope_tables(s):
    inv_freq = ROPE_BASE ** (-jnp.arange(0, HEAD_DIM, 2, dtype=F32) / HEAD_DIM)
    angle = jnp.arange(s, dtype=F32)[:, None] * inv_freq[None, :]
    cos = jnp.cos(angle)
    sin = jnp.sin(angle)
    return jnp.concatenate([cos, cos], axis=-1), jnp.concatenate([-sin, sin], axis=-1)


def _router_weights(router_coarse_w, router_fine_w, router_coarse_b, router_fine_b):
    d = router_fine_w.shape[0]
    w = jnp.concatenate([router_fine_w, router_coarse_w], axis=1)
    n = w.shape[1]
    w_hi = w.astype(BF16)
    w_lo = (w - w_hi.astype(F32)).astype(BF16)
    pad = jnp.zeros((d, LANES // 2 - n), BF16)
    rw = jnp.concatenate([w_hi, pad, w_lo, pad], axis=1)
    rb = jnp.concatenate([router_fine_b, router_coarse_b, jnp.zeros((LANES - n,), F32)]).reshape(1, LANES)
    return rw, rb


def _layer(x2, c, nb, s, cosf, sinf, lg_tab, layer, ada_w, ada_b, norm_pre_mix, norm_post_mix, w_in, fox_f_bias,
           ret_gn_w, fox_qn_w, fox_kn_w, fox_on_w, w_out, norm_pre_ffn, norm_post_ffn,
           router_coarse_w, router_coarse_b, router_fine_w, router_fine_b, w1, w3, w2):
    t, d = x2.shape
    mod3 = _modulation(c, ada_w, ada_b, layer).reshape(nb, N_MOD, d)

    proj, lf = _in_projection(x2, mod3, norm_pre_mix[layer], w_in, fox_f_bias[layer], fox_qn_w[layer],
                              fox_kn_w[layer], cosf, sinf, nb, s, layer)
    cum = _cum_forget(lf)
    proj3 = proj.reshape(nb, s, MAIN_COLS)
    ret = _retention(proj3, ret_gn_w[layer], lg_tab)
    fox = _fox_attention(proj3, cum, fox_on_w[layer])

    rw, rb = _router_weights(router_coarse_w[layer], router_fine_w[layer], router_coarse_b[layer],
                             router_fine_b[layer])
    x1, h2t, idx, wts, counts = _out_projection(
        ret.reshape(t, RET_WIDTH), fox.reshape(t, FOX_WIDTH), _cast_bf16(w_out, layer), x2, mod3,
        norm_post_mix[layer], norm_pre_ffn[layer], rw, rb, s)

    pos, plan = _route_plan(idx, counts)
    pos_a, pos_b = pos[0], pos[1]
    xs = _dispatch(h2t, pos_a, pos_b, plan)
    ys = _expert_gemm(xs, plan, w1, w3, w2, layer)
    return _combine(ys, pos_a, pos_b, x1, wts, mod3, norm_post_ffn[layer], s)


def kernel(x, c, ada_w, ada_b, norm_pre_mix, norm_post_mix, w_in, fox_f_bias, ret_gn_w, fox_qn_w, fox_kn_w,
           fox_on_w, w_out, norm_pre_ffn, norm_post_ffn, router_coarse_w, router_coarse_b, router_fine_w,
           router_fine_b, w1, w3, w2):
    nb, s, d = x.shape
    depth = ada_w.shape[0]
    cosf, sinf = _rope_tables(s)
    log_gamma = jnp.log1p(-jnp.exp2(-5.0 - jnp.arange(RET_HEADS, dtype=F32)))
    lg_tab = jnp.broadcast_to(log_gamma[:, None, None], (RET_HEADS, 1, LANES))
    x2 = x.reshape(nb * s, d)
    for layer in range(depth):
        x2 = _layer(x2, c, nb, s, cosf, sinf, lg_tab, layer, ada_w, ada_b, norm_pre_mix, norm_post_mix,
                    w_in, fox_f_bias, ret_gn_w, fox_qn_w, fox_kn_w, fox_on_w, w_out, norm_pre_ffn,
                    norm_post_ffn, router_coarse_w, router_coarse_b, router_fine_w, router_fine_b, w1, w3, w2)
    return x2.reshape(nb, s, d)
```

```python
import functools

import jax
import jax.numpy as jnp
from jax import lax
from jax.experimental import pallas as pl
from jax.experimental.pallas import tpu as pltpu

F32 = jnp.float32
BF16 = jnp.bfloat16
I32 = jnp.int32

D_MODEL = 2048
HEAD_DIM = 128
RET_WIDTH = D_MODEL // 2
FOX_WIDTH = D_MODEL - RET_WIDTH
RET_HEADS = RET_WIDTH // HEAD_DIM
FOX_HEADS = FOX_WIDTH // HEAD_DIM
ROPE_BASE = 10000.0
N_GROUPS = 4
EXPERTS_PER_GROUP = 8
N_EXPERTS = N_GROUPS * EXPERTS_PER_GROUP
D_EXPERT = D_MODEL // 4
N_MOD = 6
MOD_HEAD = 2
EPS = 1e-6
MAIN_COLS = 4 * RET_WIDTH + 3 * FOX_WIDTH

LANES = 128
SUBLANES = 8
ROW_TILES = D_MODEL // LANES
NEG = -1e30
LOG2E = 1.4426950408889634

VMEM_LIMIT = 56 * 1024 * 1024

MOD_TN = 1024
IN_TM, IN_TN = 2048, 512
IN_RC = 256
IN_XC = 512
RET_CHUNK = 512
RET_HPS = 2
FOX_TQ, FOX_TK = 512, 512
FOX_HPS = 2
OUT_TM = 512
OUT_RC = 256
EXP_TM = 256
DSP_TM = 512
DSP_UNROLL = 8
CMB_TM = 256
DMA_UNROLL = 8


def _cparams(sem):
    return pltpu.CompilerParams(dimension_semantics=sem, vmem_limit_bytes=VMEM_LIMIT)


def _nt_dot(a, b):
    return lax.dot_general(a, b, (((1,), (1,)), ((), ())), preferred_element_type=F32)


def _tn_dot(a, b):
    return lax.dot_general(a, b, (((0,), (0,)), ((), ())), preferred_element_type=F32)


def _sigmoid(x):
    return 1.0 / (1.0 + jnp.exp(-x))


def _col_to_row(col):
    n = col.shape[0]
    ri = lax.broadcasted_iota(I32, (n, n), 0)
    ci = lax.broadcasted_iota(I32, (n, n), 1)
    return jnp.sum(jnp.where(ri == ci, col, 0.0), axis=0, keepdims=True)


def _row_to_col(row):
    n = row.shape[1]
    ri = lax.broadcasted_iota(I32, (n, n), 0)
    ci = lax.broadcasted_iota(I32, (n, n), 1)
    return jnp.sum(jnp.where(ri == ci, row, 0.0), axis=1, keepdims=True)


def _mod_columns(ca_ref, w_ref, b_ref, o_ref):
    for b in range(ca_ref.shape[0]):
        ca = ca_ref[b]
        for j in range(w_ref.shape[1] // LANES):
            sl = slice(j * LANES, (j + 1) * LANES)
            col = jnp.sum(w_ref[:, sl] * ca, axis=0, keepdims=True)
            o_ref[b:b + 1, sl] = col + b_ref[:, sl]


def _mod_head_kernel(cb_ref, w_ref, b_ref, o_ref, ca_ref):
    cv = cb_ref[...]
    ca_ref[...] = cv * _sigmoid(cv)
    _mod_columns(ca_ref, w_ref, b_ref, o_ref)


def _modulation_head(c, ada_w, ada_b3, layer, n_cols):
    nb, d = c.shape
    cb = jnp.broadcast_to(c[:, :, None], (nb, d, LANES))
    return pl.pallas_call(
        _mod_head_kernel,
        grid=(n_cols // MOD_TN,),
        in_specs=[pl.BlockSpec((nb, d, LANES), lambda j: (0, 0, 0)),
                  pl.BlockSpec((None, d, MOD_TN), lambda j: (layer, 0, j)),
                  pl.BlockSpec((None, 1, MOD_TN), lambda j: (layer, 0, j))],
        out_specs=[pl.BlockSpec((nb, MOD_TN), lambda j: (0, j)),
                   pl.BlockSpec((nb, d, LANES), lambda j: (0, 0, 0))],
        out_shape=[jax.ShapeDtypeStruct((nb, n_cols), F32),
                   jax.ShapeDtypeStruct((nb, d, LANES), F32)],
        compiler_params=_cparams(("arbitrary",)),
        name="adaln_mod",
    )(cb, ada_w, ada_b3)


def _rms_rows(x):
    return x * lax.rsqrt(jnp.mean(x * x, axis=-1, keepdims=True) + EPS)


def _inproj_kernel(x_hbm, mod_ref, nw_ref, w_ref, wf_ref, fb_ref, cos_ref, sin_ref, qn_ref, kn_ref,
                   o_ref, lf_ref, h_scr, wb_scr, xbuf, xsem):
    i = pl.program_id(0)
    j = pl.program_id(1)
    tiles_per_sec = RET_WIDTH // IN_TN
    heads_per_tile = IN_TN // HEAD_DIM

    @pl.when(j == 0)
    def _():
        shift = mod_ref[0, 0:1, :]
        scale = 1.0 + mod_ref[0, 1:2, :]
        nw = nw_ref[...]
        n_chunks = IN_TM // IN_XC

        def x_copy(c):
            rows = pl.ds(i * IN_TM + c * IN_XC, IN_XC)
            return pltpu.make_async_copy(x_hbm.at[rows, :], xbuf.at[c % 2], xsem.at[c % 2])

        x_copy(0).start()
        for c in range(n_chunks):
            if c + 1 < n_chunks:
                x_copy(c + 1).start()
            x_copy(c).wait()
            rows = 128
            for r in range(IN_XC // rows):
                h = _rms_rows(xbuf[c % 2, r * rows:(r + 1) * rows, :]) * nw * scale + shift
                h_scr[c * IN_XC + r * rows:c * IN_XC + (r + 1) * rows, :] = h.astype(BF16)
        z = _nt_dot(wf_ref[...], h_scr[...]) + fb_ref[...]
        lf_ref[0] = jnp.minimum(z, 0.0) - jnp.log1p(jnp.exp(-jnp.abs(z)))

    wb_scr[...] = w_ref[...].astype(BF16)
    sec = j // tiles_per_sec

    def section(epilogue):
        for r in range(IN_TM // IN_RC):
            rs = slice(r * IN_RC, (r + 1) * IN_RC)
            acc = _nt_dot(h_scr[rs, :], wb_scr[...])
            epilogue(acc, rs)

    def per_head(fn):
        def epilogue(acc, rs):
            for hh in range(heads_per_tile):
                sl = slice(hh * HEAD_DIM, (hh + 1) * HEAD_DIM)
                o_ref[rs, sl] = fn(acc[:, sl], rs).astype(BF16)
        return epilogue

    def whole(fn):
        def epilogue(acc, rs):
            o_ref[rs, :] = fn(acc).astype(BF16)
        return epilogue

    def rope(v, rs):
        return v * cos_ref[rs, :] + pltpu.roll(v, HEAD_DIM // 2, 1) * sin_ref[rs, :]

    @pl.when(sec == 0)
    def _():
        section(per_head(rope))

    @pl.when(sec == 1)
    def _():
        section(per_head(lambda v, rs: rope(v, rs) * (HEAD_DIM ** -0.5)))

    @pl.when((sec == 2) | (sec == 6))
    def _():
        section(whole(lambda a: a))

    @pl.when(sec == 3)
    def _():
        section(whole(lambda a: a * _sigmoid(a)))

    @pl.when(sec == 4)
    def _():
        section(per_head(lambda v, rs: _rms_rows(v) * qn_ref[...] * (LOG2E * HEAD_DIM ** -0.5)))

    @pl.when(sec == 5)
    def _():
        section(per_head(lambda v, rs: _rms_rows(v) * kn_ref[...]))


def _in_projection(x2, mod_head, norm_w, w_in, fox_f_bias, fox_qn_w, fox_kn_w, cosf, sinf, nb, s, layer):
    t, d = x2.shape
    w_t = jnp.swapaxes(w_in, 1, 2)
    wf_t = w_t[layer, MAIN_COLS:, :].astype(BF16)
    per_b = s // IN_TM
    grid = (t // IN_TM, MAIN_COLS // IN_TN)
    return pl.pallas_call(
        _inproj_kernel,
        grid=grid,
        in_specs=[
            pl.BlockSpec(memory_space=pl.ANY),
            pl.BlockSpec((1, MOD_HEAD, d), lambda i, j: (i // per_b, 0, 0)),
            pl.BlockSpec((1, d), lambda i, j: (0, 0)),
            pl.BlockSpec((None, IN_TN, d), lambda i, j: (layer, j, 0)),
            pl.BlockSpec((FOX_HEADS, d), lambda i, j: (0, 0)),
            pl.BlockSpec((FOX_HEADS, 1), lambda i, j: (0, 0)),
            pl.BlockSpec((IN_TM, HEAD_DIM), lambda i, j: (i % per_b, 0)),
            pl.BlockSpec((IN_TM, HEAD_DIM), lambda i, j: (i % per_b, 0)),
            pl.BlockSpec((1, HEAD_DIM), lambda i, j: (0, 0)),
            pl.BlockSpec((1, HEAD_DIM), lambda i, j: (0, 0)),
        ],
        out_specs=[
            pl.BlockSpec((IN_TM, IN_TN), lambda i, j: (i, j)),
            pl.BlockSpec((1, FOX_HEADS, IN_TM), lambda i, j: (i // per_b, 0, i % per_b)),
        ],
        out_shape=[jax.ShapeDtypeStruct((t, MAIN_COLS), BF16),
                   jax.ShapeDtypeStruct((nb, FOX_HEADS, s), F32)],
        scratch_shapes=[pltpu.VMEM((IN_TM, d), BF16), pltpu.VMEM((IN_TN, d), BF16),
                        pltpu.VMEM((2, IN_XC, d), F32), pltpu.SemaphoreType.DMA((2,))],
        compiler_params=_cparams(("arbitrary", "arbitrary")),
        name="in_projection",
    )(x2, mod_head, norm_w.reshape(1, d), w_t, wf_t, fox_f_bias.reshape(FOX_HEADS, 1), cosf, sinf,
      fox_qn_w.reshape(1, HEAD_DIM), fox_kn_w.reshape(1, HEAD_DIM))


def _split3(x):
    hi = x.astype(BF16)
    r1 = x - hi.astype(F32)
    mid = r1.astype(BF16)
    lo = (r1 - mid.astype(F32)).astype(BF16)
    return hi, mid, lo


def _cumsum_kernel(x_ref, o_ref, *, rows_per_seq):
    x = x_ref[...]
    r = x.shape[0]
    ri = lax.broadcasted_iota(I32, (LANES, LANES), 0)
    ci = lax.broadcasted_iota(I32, (LANES, LANES), 1)
    upper = jnp.where(ri <= ci, 1.0, 0.0).astype(BF16)
    rowcum = sum(jnp.dot(p, upper, preferred_element_type=F32) for p in _split3(x))
    tot = jnp.broadcast_to(rowcum[:, LANES - 1:LANES], (r, LANES))
    gi = lax.broadcasted_iota(I32, (r, r), 0)
    gj = lax.broadcasted_iota(I32, (r, r), 1)
    same_seq = (gi // rows_per_seq) == (gj // rows_per_seq)
    lower = jnp.where(same_seq & (gj < gi), 1.0, 0.0).astype(BF16)
    prefix = sum(jnp.dot(lower, p, preferred_element_type=F32) for p in _split3(tot))
    o_ref[...] = rowcum + prefix


def _cum_forget(lf):
    nb, nh, s = lf.shape
    rows = nb * nh * s // LANES
    out = pl.pallas_call(
        functools.partial(_cumsum_kernel, rows_per_seq=s // LANES),
        out_shape=jax.ShapeDtypeStruct((rows, LANES), F32),
        compiler_params=pltpu.CompilerParams(vmem_limit_bytes=VMEM_LIMIT),
        name="forget_cumsum",
    )(lf.reshape(rows, LANES))
    return out.reshape(nb * nh, 1, s)


def _ret_kernel(lg_ref, q_ref, k_ref, v_ref, g_ref, gnw_ref, o_ref):
    c = RET_CHUNK
    s = q_ref.shape[1]
    ri = lax.broadcasted_iota(I32, (c, c), 0)
    ci = lax.broadcasted_iota(I32, (c, c), 1)
    diff = (ri - ci).astype(F32)
    pos = lax.broadcasted_iota(I32, (c, 1), 0).astype(F32)
    consts = []
    for hh in range(RET_HPS):
        lg = lg_ref[hh][:, 0:1]
        decay = jnp.where(diff >= 0.0, jnp.exp(lg * jnp.maximum(diff, 0.0)), 0.0)
        zeta = jnp.exp(lg * (float(c - 1) - pos))
        xi = jnp.exp(lg * (pos + 1.0))
        chunk_decay = jnp.exp(lg * float(c))
        consts.append((decay, zeta, xi, chunk_decay))

    def body(n, states):
        r0 = n * c
        new_states = []
        for hh in range(RET_HPS):
            decay, zeta, xi, chunk_decay = consts[hh]
            sl = slice(hh * HEAD_DIM, (hh + 1) * HEAD_DIM)
            q = q_ref[0, pl.ds(r0, c), sl]
            k = k_ref[0, pl.ds(r0, c), sl]
            v = v_ref[0, pl.ds(r0, c), sl]
            scores = _nt_dot(q, k) * decay
            intra = jnp.dot(scores.astype(BF16), v, preferred_element_type=F32)
            cross = jnp.dot(q, states[hh].astype(BF16), preferred_element_type=F32) * xi
            kz = (k.astype(F32) * zeta).astype(BF16)
            kv = _tn_dot(kz, v)
            o = intra + cross
            mu = jnp.mean(o, axis=-1, keepdims=True)
            oc = o - mu
            var = jnp.mean(oc * oc, axis=-1, keepdims=True)
            y = oc * lax.rsqrt(var + EPS) * gnw_ref[:, sl] * g_ref[0, pl.ds(r0, c), sl].astype(F32)
            o_ref[0, pl.ds(r0, c), sl] = y.astype(BF16)
            new_states.append(states[hh] * chunk_decay + kv)
        return tuple(new_states)

    states = tuple(jnp.zeros((HEAD_DIM, HEAD_DIM), F32) for _ in range(RET_HPS))
    for n in range(s // c):
        states = body(n, states)


def _retention(proj3, ret_gn_w, lg_tab):
    nb, s, _ = proj3.shape
    w = RET_HPS * HEAD_DIM
    gpg = RET_WIDTH // w

    def col(sec):
        return pl.BlockSpec((1, s, w), lambda b, h: (b, 0, sec * gpg + h))

    return pl.pallas_call(
        _ret_kernel,
        grid=(nb, gpg),
        in_specs=[pl.BlockSpec((RET_HPS, 1, LANES), lambda b, h: (h, 0, 0)),
                  col(0), col(1), col(2), col(3),
                  pl.BlockSpec((1, w), lambda b, h: (0, h))],
        out_specs=pl.BlockSpec((1, s, w), lambda b, h: (b, 0, h)),
        out_shape=jax.ShapeDtypeStruct((nb, s, RET_WIDTH), BF16),
        compiler_params=_cparams(("arbitrary", "arbitrary")),
        name="retention",
    )(lg_tab, proj3, proj3, proj3, proj3, ret_gn_w.reshape(1, RET_WIDTH))


def _fox_kernel(q_ref, k_ref, v_ref, cum_ref, onw_ref, ca_ref, aw_ref, ab_ref, wo_ref,
                o_ref, mod_ref, wob_ref):
    tq, t = FOX_TQ, FOX_TK
    for qi in range(k_ref.shape[1] // tq):
        @pl.when(pl.program_id(2) == qi)
        def _():
            _mod_columns(ca_ref, aw_ref, ab_ref, mod_ref)
            wob_ref[...] = wo_ref[...].astype(BF16)
            _fox_query_tile(qi, q_ref, k_ref, v_ref, cum_ref, onw_ref, o_ref)


def _fox_query_tile(qi, q_ref, k_ref, v_ref, cum_ref, onw_ref, o_ref):
    tq, t = FOX_TQ, FOX_TK
    q0 = qi * tq
    last = q0 // t
    rows = slice(None)
    ri = lax.broadcasted_iota(I32, (tq, t), 0)
    ci = lax.broadcasted_iota(I32, (tq, t), 1)

    def step(ki, carry, masked):
        k0 = ki * t
        out = []
        for hh in range(FOX_HPS):
            m, l, acc = carry[hh]
            sl = slice(hh * HEAD_DIM, (hh + 1) * HEAD_DIM)
            k = k_ref[0, k0:k0 + t, sl]
            v = v_ref[0, k0:k0 + t, sl]
            sc = _nt_dot(q_ref[0, rows, sl], k) - cum_ref[hh, :, k0:k0 + t] * LOG2E
            if masked:
                sc = jnp.where(ci + k0 <= ri + q0, sc, NEG)
            m_new = jnp.maximum(m, jnp.max(sc, axis=1, keepdims=True))
            p = jnp.exp2(sc - m_new)
            alpha = jnp.exp2(m - m_new)
            l = alpha * l + jnp.sum(p, axis=1, keepdims=True)
            acc = alpha * acc + jnp.dot(p.astype(BF16), v, preferred_element_type=F32)
            out.append((m_new, l, acc))
        return tuple(out)

    init = tuple((jnp.full((tq, 1), NEG, F32), jnp.zeros((tq, 1), F32), jnp.zeros((tq, HEAD_DIM), F32))
                 for _ in range(FOX_HPS))
    carry = init
    for ki in range(last):
        carry = step(ki, carry, False)
    carry = step(last, carry, True)
    for hh in range(FOX_HPS):
        _, l, acc = carry[hh]
        sl = slice(hh * HEAD_DIM, (hh + 1) * HEAD_DIM)
        o = acc / l
        o_ref[0, rows, sl] = (_rms_rows(o) * onw_ref[:, sl]).astype(BF16)


def _fox_attention(proj3, cum, fox_on_w, ca, ada_w, ada_b3, w_out, layer, mod_col0):
    nb, s, _ = proj3.shape
    d = ca.shape[1]
    w = FOX_HPS * HEAD_DIM
    base = 4 * RET_WIDTH // w
    gpg = FOX_WIDTH // w
    nq = s // FOX_TQ
    n_steps = nb * gpg * nq
    n_mod = ada_w.shape[2] - mod_col0
    assert n_mod == n_steps * LANES and d % n_steps == 0
    wo_rows = d // n_steps
    step = lambda b, h, i: (b * gpg + h) * nq + i
    return pl.pallas_call(
        _fox_kernel,
        grid=(nb, gpg, nq),
        in_specs=[pl.BlockSpec((1, FOX_TQ, w), lambda b, h, i: (b, i, base + h)),
                  pl.BlockSpec((1, s, w), lambda b, h, i: (b, 0, base + gpg + h)),
                  pl.BlockSpec((1, s, w), lambda b, h, i: (b, 0, base + 2 * gpg + h)),
                  pl.BlockSpec((FOX_HPS, 1, s), lambda b, h, i: (b * gpg + h, 0, 0)),
                  pl.BlockSpec((1, w), lambda b, h, i: (0, h)),
                  pl.BlockSpec((nb, d, LANES), lambda b, h, i: (0, 0, 0)),
                  pl.BlockSpec((None, d, LANES), lambda b, h, i: (layer, 0, mod_col0 // LANES + step(b, h, i))),
                  pl.BlockSpec((None, 1, LANES), lambda b, h, i: (layer, 0, mod_col0 // LANES + step(b, h, i))),
                  pl.BlockSpec((None, wo_rows, d), lambda b, h, i: (layer, step(b, h, i), 0))],
        out_specs=[pl.BlockSpec((1, FOX_TQ, w), lambda b, h, i: (b, i, h)),
                   pl.BlockSpec((nb, LANES), lambda b, h, i: (0, step(b, h, i))),
                   pl.BlockSpec((wo_rows, d), lambda b, h, i: (step(b, h, i), 0))],
        out_shape=[jax.ShapeDtypeStruct((nb, s, FOX_WIDTH), BF16),
                   jax.ShapeDtypeStruct((nb, n_mod), F32),
                   jax.ShapeDtypeStruct((d, d), BF16)],
        compiler_params=_cparams(("arbitrary", "arbitrary", "arbitrary")),
        name="fox_attention",
    )(proj3, proj3, proj3, cum, fox_on_w.reshape(1, FOX_WIDTH), ca, ada_w, ada_b3, w_out)


def _outproj_kernel(ret_ref, fox_ref, wo_ref, x_ref, mod_ref, npost_ref, npre_ref, rw_ref, rb_ref,
                    x1_ref, h2_ref, idx_ref, wts_ref, cnt_ref, carry_scr):
    i = pl.program_id(0)

    @pl.when(i == 0)
    def _():
        carry_scr[...] = jnp.zeros_like(carry_scr)

    idx_ref[...] = jnp.zeros_like(idx_ref)
    subtiles = [slice(r * OUT_RC, (r + 1) * OUT_RC) for r in range(OUT_TM // OUT_RC)]
    ys = [jnp.dot(ret_ref[rs, :], wo_ref[:RET_WIDTH, :], preferred_element_type=F32)
          + jnp.dot(fox_ref[rs, :], wo_ref[RET_WIDTH:, :], preferred_element_type=F32) for rs in subtiles]
    for rs, y in zip(subtiles, ys):
        _outproj_rows(rs, y, x_ref, mod_ref, npost_ref, npre_ref, rw_ref, rb_ref, x1_ref, h2_ref, idx_ref,
                      wts_ref, cnt_ref, carry_scr)


def _outproj_rows(rs, y, x_ref, mod_ref, npost_ref, npre_ref, rw_ref, rb_ref,
                  x1_ref, h2_ref, idx_ref, wts_ref, cnt_ref, carry_scr):
    tm = OUT_RC
    gate1 = mod_ref[0, 2:3, :]
    shift2 = mod_ref[0, 3:4, :]
    scale2 = 1.0 + mod_ref[0, 4:5, :]
    x1 = x_ref[rs, :] + gate1 * (_rms_rows(y) * npost_ref[...])
    x1_ref[rs, :] = x1
    h2 = _rms_rows(x1) * npre_ref[...] * scale2 + shift2
    h2_ref[rs, :, :] = h2.astype(BF16).reshape(tm, ROW_TILES, LANES)

    hi = h2.astype(BF16)
    lo = (h2 - hi.astype(F32)).astype(BF16)
    rw = rw_ref[...]
    lg2 = jnp.dot(hi, rw, preferred_element_type=F32) + jnp.dot(lo, rw, preferred_element_type=F32)
    logits = lg2 + pltpu.roll(lg2, LANES // 2, 1) + rb_ref[...]

    lane = lax.broadcasted_iota(I32, (tm, LANES), 1)
    lanef = lane.astype(F32)
    big = float(LANES)

    def first_argmax(vals):
        vmax = jnp.max(vals, axis=1, keepdims=True)
        idx = jnp.min(jnp.where(vals == vmax, lanef, big), axis=1, keepdims=True)
        return vmax, idx

    is_coarse = (lane >= N_EXPERTS) & (lane < N_EXPERTS + N_GROUPS)
    cvals = jnp.where(is_coarse, logits, NEG)
    cmax, cidx = first_argmax(cvals)
    gidx = cidx - float(N_EXPERTS)
    gprob = 1.0 / jnp.sum(jnp.where(is_coarse, jnp.exp(cvals - cmax), 0.0), axis=1, keepdims=True)
    glo = gidx * float(EXPERTS_PER_GROUP)
    in_group = (lanef >= glo) & (lanef < glo + float(EXPERTS_PER_GROUP))
    fvals = jnp.where(in_group, logits, NEG)
    v1, e1 = first_argmax(fvals)
    fvals2 = jnp.where(lanef == e1, NEG, fvals)
    v2, e2 = first_argmax(fvals2)
    ex = jnp.exp(v2 - v1)
    w1 = gprob * (1.0 / (1.0 + ex))
    w2 = gprob * (ex / (1.0 + ex))

    sel1 = lanef == e1
    sel2 = lanef == e2
    onehot = jnp.where(sel1 | sel2, 1.0, 0.0)
    ri = lax.broadcasted_iota(I32, (tm, tm), 0)
    ci = lax.broadcasted_iota(I32, (tm, tm), 1)
    lower = jnp.where(ci < ri, 1.0, 0.0).astype(BF16)
    before = jnp.dot(lower, onehot.astype(BF16), preferred_element_type=F32) + carry_scr[...]
    r1 = jnp.sum(jnp.where(sel1, before, 0.0), axis=1, keepdims=True)
    r2 = jnp.sum(jnp.where(sel2, before, 0.0), axis=1, keepdims=True)
    carry_new = carry_scr[...] + jnp.sum(onehot, axis=0, keepdims=True)
    carry_scr[...] = carry_new
    cnt_ref[...] = carry_new.astype(I32)

    for row, col in enumerate((e1, e2, r1, r2)):
        idx_ref[row:row + 1, rs] = _col_to_row(col).astype(I32)
    wts_ref[rs, :] = jnp.where(lane == 0, w1, jnp.where(lane == 1, w2, 0.0))


def _out_projection(ret2, fox2, w_out_bf, x2, mod3, npost, npre, rw, rb, s):
    t, d = x2.shape
    per_b = s // OUT_TM
    row = lambda i: (i, 0)
    const = lambda i: (0, 0)
    return pl.pallas_call(
        _outproj_kernel,
        grid=(t // OUT_TM,),
        in_specs=[pl.BlockSpec((OUT_TM, RET_WIDTH), row),
                  pl.BlockSpec((OUT_TM, FOX_WIDTH), row),
                  pl.BlockSpec((d, d), const),
                  pl.BlockSpec((OUT_TM, d), row),
                  pl.BlockSpec((1, N_MOD, d), lambda i: (i // per_b, 0, 0)),
                  pl.BlockSpec((1, d), const),
                  pl.BlockSpec((1, d), const),
                  pl.BlockSpec((d, LANES), const),
                  pl.BlockSpec((1, LANES), const)],
        out_specs=[pl.BlockSpec((OUT_TM, d), row),
                   pl.BlockSpec((OUT_TM, ROW_TILES, LANES), lambda i: (i, 0, 0)),
                   pl.BlockSpec((SUBLANES, OUT_TM), lambda i: (0, i)),
                   pl.BlockSpec((OUT_TM, LANES), row),
                   pl.BlockSpec((1, LANES), const)],
        out_shape=[jax.ShapeDtypeStruct((t, d), F32),
                   jax.ShapeDtypeStruct((t, ROW_TILES, LANES), BF16),
                   jax.ShapeDtypeStruct((SUBLANES, t), I32),
                   jax.ShapeDtypeStruct((t, LANES), F32),
                   jax.ShapeDtypeStruct((1, LANES), I32)],
        scratch_shapes=[pltpu.VMEM((1, LANES), F32)],
        compiler_params=_cparams(("arbitrary",)),
        name="out_projection_router",
    )(ret2, fox2, w_out_bf, x2, mod3, npost.reshape(1, d), npre.reshape(1, d), rw, rb)


PLAN_ROW_START = 0
PLAN_TILE_EXPERT = 1
PLAN_NUM_TILES = 2
PLAN_TILE_END = 3
PLAN_COUNT = 4


def _plan_kernel(idx_ref, cnt_ref, pos_ref, plan_ref):
    lane = lax.broadcasted_iota(I32, (1, LANES), 1)
    cnt = jnp.where(lane < N_EXPERTS, cnt_ref[...], 0).astype(F32)
    tiles = jnp.floor((cnt + float(EXP_TM - 1)) * (1.0 / EXP_TM))
    ri = lax.broadcasted_iota(I32, (LANES, LANES), 0)
    ci = lax.broadcasted_iota(I32, (LANES, LANES), 1)
    upper = jnp.where(ri <= ci, 1.0, 0.0).astype(BF16)
    tiles8 = jnp.broadcast_to(tiles, (SUBLANES, LANES)).astype(BF16)
    tile_end = jnp.dot(tiles8, upper, preferred_element_type=F32)[0:1, :]
    row_start = (tile_end - tiles) * float(EXP_TM)
    num_tiles = jnp.max(tile_end, axis=1, keepdims=True)
    end_col = _row_to_col(tile_end)
    owned = jnp.where((end_col <= ci.astype(F32)) & (ri < N_EXPERTS), 1.0, 0.0)
    tile_expert = jnp.sum(owned, axis=0, keepdims=True)
    last_expert = jnp.max(jnp.where(lane.astype(F32) < num_tiles, tile_expert, 0.0), axis=1, keepdims=True)
    tile_expert = jnp.minimum(tile_expert, last_expert)
    plan_ref[...] = jnp.zeros_like(plan_ref)
    plan_ref[PLAN_ROW_START:PLAN_ROW_START + 1, :] = row_start.astype(I32)
    plan_ref[PLAN_TILE_EXPERT:PLAN_TILE_EXPERT + 1, :] = tile_expert.astype(I32)
    plan_ref[PLAN_NUM_TILES:PLAN_NUM_TILES + 1, :] = jnp.broadcast_to(num_tiles, (1, LANES)).astype(I32)
    plan_ref[PLAN_TILE_END:PLAN_TILE_END + 1, :] = tile_end.astype(I32)
    plan_ref[PLAN_COUNT:PLAN_COUNT + 1, :] = cnt.astype(I32)

    experts = idx_ref[0:2, :]
    start_i = row_start.astype(I32)
    base = jnp.zeros(experts.shape, I32)
    for e in range(N_EXPERTS):
        base = jnp.where(experts == e, start_i[:, e:e + 1], base)
    pos_ref[...] = jnp.zeros_like(pos_ref)
    pos_ref[0:2, :] = idx_ref[2:4, :] + base


def _route_plan(idx, counts):
    t = idx.shape[1]
    return pl.pallas_call(
        _plan_kernel,
        out_shape=[jax.ShapeDtypeStruct((SUBLANES, t), I32),
                   jax.ShapeDtypeStruct((SUBLANES, LANES), I32)],
        compiler_params=pltpu.CompilerParams(vmem_limit_bytes=VMEM_LIMIT),
        name="route_plan",
    )(idx, counts)


def _dispatch_kernel(pa_ref, pb_ref, plan_ref, h2_hbm, xs_ref, inv, h2v, sem):
    i = pl.program_id(0)
    n_tok = h2_hbm.shape[0]
    nt = plan_ref[PLAN_NUM_TILES, 0]
    tiles_per_step = DSP_TM // EXP_TM
    table_copy = pltpu.make_async_copy(h2_hbm, h2v, sem)

    @pl.when(i == 0)
    def _():
        table_copy.start()

        def fill(p, _):
            inv[p] = 0
            return 0
        for e in range(N_EXPERTS):
            lax.fori_loop(plan_ref[PLAN_ROW_START, e] + plan_ref[PLAN_COUNT, e],
                          plan_ref[PLAN_TILE_END, e] * EXP_TM, fill, 0)

        def scatter(j, _):
            for u in range(DMA_UNROLL):
                tok = j * DMA_UNROLL + u
                inv[pa_ref[tok]] = tok
                inv[pb_ref[tok]] = tok
            return 0
        lax.fori_loop(0, n_tok // DMA_UNROLL, scatter, 0)
        table_copy.wait()

    for sub in range(tiles_per_step):
        tile = i * tiles_per_step + sub
        rows = pl.ds(sub * EXP_TM, EXP_TM)

        @pl.when(tile < nt)
        def _():
            base = tile * EXP_TM

            def body(j, _):
                for u in range(DSP_UNROLL):
                    r = j * DSP_UNROLL + u
                    xs_ref[sub * EXP_TM + r] = h2v[inv[base + r]]
                return 0
            lax.fori_loop(0, EXP_TM // DSP_UNROLL, body, 0)

        @pl.when(tile >= nt)
        def _():
            xs_ref[rows] = jnp.zeros((EXP_TM, ROW_TILES, LANES), BF16)


def _dispatch(h2t, pos_a, pos_b, plan):
    t = h2t.shape[0]
    n_rows = ((t * 2) // EXP_TM + N_EXPERTS) * EXP_TM
    return pl.pallas_call(
        _dispatch_kernel,
        grid_spec=pltpu.PrefetchScalarGridSpec(
            num_scalar_prefetch=3,
            grid=(n_rows // DSP_TM,),
            in_specs=[pl.BlockSpec(memory_space=pl.ANY)],
            out_specs=pl.BlockSpec((DSP_TM, ROW_TILES, LANES), lambda i, pa, pb, plan: (i, 0, 0)),
            scratch_shapes=[pltpu.SMEM((n_rows,), I32),
                            pltpu.VMEM((t, ROW_TILES, LANES), BF16),
                            pltpu.SemaphoreType.DMA(())]),
        out_shape=jax.ShapeDtypeStruct((n_rows, ROW_TILES, LANES), BF16),
        compiler_params=_cparams(("arbitrary",)),
        name="moe_dispatch",
    )(pos_a, pos_b, plan, h2t)


def _expert_kernel(plan_ref, xs_ref, w1_hbm, w3_hbm, w2_hbm, ys_ref,
                   seq, w1s, w3s, w2s, wsem, w1b, w3b, w2b, *, layer):
    i = pl.program_id(0)
    tm = EXP_TM
    nt = plan_ref[PLAN_NUM_TILES, 0]
    expert = plan_ref[PLAN_TILE_EXPERT, i]

    def weight_copies(e, buf):
        return (pltpu.make_async_copy(w1_hbm.at[layer, e], w1s.at[buf], wsem.at[buf, 0]),
                pltpu.make_async_copy(w3_hbm.at[layer, e], w3s.at[buf], wsem.at[buf, 1]),
                pltpu.make_async_copy(w2_hbm.at[layer, e], w2s.at[buf], wsem.at[buf, 2]))

    def weight_start(e, buf):
        for cp in weight_copies(e, buf):
            cp.start(priority=1)

    def next_used_tile(e):
        return plan_ref[PLAN_TILE_END, e]

    @pl.when(i == 0)
    def _():
        first_expert = plan_ref[PLAN_TILE_EXPERT, 0]
        weight_start(first_expert, 0)
        t1 = next_used_tile(first_expert)

        @pl.when(t1 < nt)
        def _():
            weight_start(plan_ref[PLAN_TILE_EXPERT, t1], 1)
        seq[0] = 0

    valid = i < nt
    first = (i == 0) | (expert != plan_ref[PLAN_TILE_EXPERT, jnp.maximum(i - 1, 0)])

    @pl.when(valid & first)
    def _():
        buf = seq[0] % 2
        for cp in weight_copies(expert, buf):
            cp.wait()
        w1b[...] = w1s[buf].astype(BF16)
        w3b[...] = w3s[buf].astype(BF16)
        w2b[...] = w2s[buf].astype(BF16)
        seq[0] = seq[0] + 1
        t1 = next_used_tile(expert)
        t2 = next_used_tile(plan_ref[PLAN_TILE_EXPERT, jnp.minimum(t1, LANES - 1)])

        @pl.when((t1 < nt) & (t2 < nt))
        def _():
            weight_start(plan_ref[PLAN_TILE_EXPERT, t2], buf)

    @pl.when(valid)
    def _():
        xt = xs_ref[...].reshape(tm, D_MODEL)
        a = jnp.dot(xt, w1b[...], preferred_element_type=F32)
        g = jnp.dot(xt, w3b[...], preferred_element_type=F32)
        act = (a * _sigmoid(a) * g).astype(BF16)
        y = jnp.dot(act, w2b[...], preferred_element_type=F32)
        ys_ref[...] = y.astype(BF16).reshape(tm, ROW_TILES, LANES)

    @pl.when(jnp.logical_not(valid))
    def _():
        ys_ref[...] = jnp.zeros_like(ys_ref)


def _expert_gemm(xs, plan, w1, w3, w2, layer):
    n_rows = xs.shape[0]
    n_tiles = n_rows // EXP_TM
    d, de = w1.shape[2], w1.shape[3]
    any_spec = pl.BlockSpec(memory_space=pl.ANY)

    def used_tile(i, plan):
        return (jnp.minimum(i, plan[PLAN_NUM_TILES, 0] - 1), 0, 0)

    return pl.pallas_call(
        functools.partial(_expert_kernel, layer=layer),
        grid_spec=pltpu.PrefetchScalarGridSpec(
            num_scalar_prefetch=1,
            grid=(n_tiles,),
            in_specs=[pl.BlockSpec((EXP_TM, ROW_TILES, LANES), used_tile), any_spec, any_spec, any_spec],
            out_specs=pl.BlockSpec((EXP_TM, ROW_TILES, LANES), lambda i, plan: (i, 0, 0)),
            scratch_shapes=[pltpu.SMEM((1,), I32),
                            pltpu.VMEM((2, d, de), F32),
                            pltpu.VMEM((2, d, de), F32),
                            pltpu.VMEM((2, de, d), F32),
                            pltpu.SemaphoreType.DMA((2, 3)),
                            pltpu.VMEM((d, de), BF16),
                            pltpu.VMEM((d, de), BF16),
                            pltpu.VMEM((de, d), BF16)]),
        out_shape=jax.ShapeDtypeStruct((n_rows, ROW_TILES, LANES), BF16),
        compiler_params=_cparams(("arbitrary",)),
        name="moe_expert_gemm",
    )(plan, xs, w1, w3, w2)


def _combine_kernel(pa_ref, pb_ref, ys_hbm, x1_ref, wts_ref, mod_ref, npost_ref, o_ref, buf_a, buf_b, sems):
    i = pl.program_id(0)
    n = pl.num_programs(0)
    tm = CMB_TM

    def start_tile(tile, slot):
        def body(j, _):
            for u in range(DMA_UNROLL):
                r = j * DMA_UNROLL + u
                tok = tile * tm + r
                pltpu.make_async_copy(ys_hbm.at[pa_ref[tok]], buf_a.at[slot, r], sems.at[0, slot]).start()
                pltpu.make_async_copy(ys_hbm.at[pb_ref[tok]], buf_b.at[slot, r], sems.at[1, slot]).start()
            return 0
        lax.fori_loop(0, tm // DMA_UNROLL, body, 0)

    def wait_tile(slot):
        pltpu.make_async_copy(ys_hbm.at[pl.ds(0, tm)], buf_a.at[slot], sems.at[0, slot]).wait()
        pltpu.make_async_copy(ys_hbm.at[pl.ds(0, tm)], buf_b.at[slot], sems.at[1, slot]).wait()

    slot = i % 2

    @pl.when(i == 0)
    def _():
        start_tile(0, 0)

    @pl.when(i + 1 < n)
    def _():
        start_tile(i + 1, 1 - slot)

    wait_tile(slot)

    wa = wts_ref[:, 0:1]
    wb = wts_ref[:, 1:2]
    ya = buf_a[slot].reshape(tm, D_MODEL).astype(F32)
    yb = buf_b[slot].reshape(tm, D_MODEL).astype(F32)
    y = wa * ya + wb * yb
    gate2 = mod_ref[0, 5:6, :]
    o_ref[...] = x1_ref[...] + gate2 * (_rms_rows(y) * npost_ref[...])


def _combine(ys, pos_a, pos_b, x1, wts, mod3, npost, s):
    t, d = x1.shape
    per_b = s // CMB_TM
    row = lambda i, pa, pb: (i, 0)
    return pl.pallas_call(
        _combine_kernel,
        grid_spec=pltpu.PrefetchScalarGridSpec(
            num_scalar_prefetch=2,
            grid=(t // CMB_TM,),
            in_specs=[pl.BlockSpec(memory_space=pl.ANY),
                      pl.BlockSpec((CMB_TM, d), row),
                      pl.BlockSpec((CMB_TM, LANES), row),
                      pl.BlockSpec((1, N_MOD, d), lambda i, pa, pb: (i // per_b, 0, 0)),
                      pl.BlockSpec((1, d), lambda i, pa, pb: (0, 0))],
            out_specs=pl.BlockSpec((CMB_TM, d), row),
            scratch_shapes=[pltpu.VMEM((2, CMB_TM, ROW_TILES, LANES), BF16),
                            pltpu.VMEM((2, CMB_TM, ROW_TILES, LANES), BF16),
                            pltpu.SemaphoreType.DMA((2, 2))]),
        out_shape=jax.ShapeDtypeStruct((t, d), F32),
        compiler_params=_cparams(("arbitrary",)),
        name="moe_combine",
    )(pos_a, pos_b, ys, x1, wts, mod3, npost.reshape(1, d))


def _rope_tables(s):
    inv_freq = ROPE_BASE ** (-jnp.arange(0, HEAD_DIM, 2, dtype=F32) / HEAD_DIM)
    angle = jnp.arange(s, dtype=F32)[:, None] * inv_freq[None, :]
    cos = jnp.cos(angle)
    sin = jnp.sin(angle)
    return jnp.concatenate([cos, cos], axis=-1), jnp.concatenate([-sin, sin], axis=-1)


def _router_weights(router_coarse_w, router_fine_w, router_coarse_b, router_fine_b):
    d = router_fine_w.shape[0]
    w = jnp.concatenate([router_fine_w, router_coarse_w], axis=1)
    n = w.shape[1]
    w_hi = w.astype(BF16)
    w_lo = (w - w_hi.astype(F32)).astype(BF16)
    pad = jnp.zeros((d, LANES // 2 - n), BF16)
    rw = jnp.concatenate([w_hi, pad, w_lo, pad], axis=1)
    rb = jnp.concatenate([router_fine_b, router_coarse_b, jnp.zeros((LANES - n,), F32)]).reshape(1, LANES)
    return rw, rb


def _layer(x2, c, nb, s, cosf, sinf, lg_tab, layer, ada_w, ada_b, norm_pre_mix, norm_post_mix, w_in, fox_f_bias,
           ret_gn_w, fox_qn_w, fox_kn_w, fox_on_w, w_out, norm_pre_ffn, norm_post_ffn,
           router_coarse_w, router_coarse_b, router_fine_w, router_fine_b, w1, w3, w2):
    t, d = x2.shape
    ada_b3 = ada_b.reshape(ada_b.shape[0], 1, N_MOD * d)
    mod_head, ca = _modulation_head(c, ada_w, ada_b3, layer, MOD_HEAD * d)
    mod_head = mod_head.reshape(nb, MOD_HEAD, d)

    proj, lf = _in_projection(x2, mod_head, norm_pre_mix[layer], w_in, fox_f_bias[layer], fox_qn_w[layer],
                              fox_kn_w[layer], cosf, sinf, nb, s, layer)
    cum = _cum_forget(lf)
    proj3 = proj.reshape(nb, s, MAIN_COLS)
    ret = _retention(proj3, ret_gn_w[layer], lg_tab)
    fox, mod_tail, w_out_bf = _fox_attention(proj3, cum, fox_on_w[layer], ca, ada_w, ada_b3, w_out, layer,
                                             MOD_HEAD * d)
    mod3 = jnp.concatenate([mod_head, mod_tail.reshape(nb, N_MOD - MOD_HEAD, d)], axis=1)

    rw, rb = _router_weights(router_coarse_w[layer], router_fine_w[layer], router_coarse_b[layer],
                             router_fine_b[layer])
    x1, h2t, idx, wts, counts = _out_projection(
        ret.reshape(t, RET_WIDTH), fox.reshape(t, FOX_WIDTH), w_out_bf, x2, mod3,
        norm_post_mix[layer], norm_pre_ffn[layer], rw, rb, s)

    pos, plan = _route_plan(idx, counts)
    pos_a, pos_b = pos[0], pos[1]
    xs = _dispatch(h2t, pos_a, pos_b, plan)
    ys = _expert_gemm(xs, plan, w1, w3, w2, layer)
    return _combine(ys, pos_a, pos_b, x1, wts, mod3, norm_post_ffn[layer], s)


def kernel(x, c, ada_w, ada_b, norm_pre_mix, norm_post_mix, w_in, fox_f_bias, ret_gn_w, fox_qn_w, fox_kn_w,
           fox_on_w, w_out, norm_pre_ffn, norm_post_ffn, router_coarse_w, router_coarse_b, router_fine_w,
           router_fine_b, w1, w3, w2):
    nb, s, d = x.shape
    depth = ada_w.shape[0]
    cosf, sinf = _rope_tables(s)
    log_gamma = jnp.log1p(-jnp.exp2(-5.0 - jnp.arange(RET_HEADS, dtype=F32)))
    lg_tab = jnp.broadcast_to(log_gamma[:, None, None], (RET_HEADS, 1, LANES))
    x2 = x.reshape(nb * s, d)
    for layer in range(depth):
        x2 = _layer(x2, c, nb, s, cosf, sinf, lg_tab, layer, ada_w, ada_b, norm_pre_mix, norm_post_mix,
                    w_in, fox_f_bias, ret_gn_w, fox_qn_w, fox_kn_w, fox_on_w, w_out, norm_pre_ffn,
                    norm_post_ffn, router_coarse_w, router_coarse_b, router_fine_w, router_fine_b, w1, w3, w2)
    return x2.reshape(nb, s, d)
```

```python
import functools

import jax
import jax.numpy as jnp
import numpy as np
from jax import lax
from jax.experimental import pallas as pl
from jax.experimental.pallas import tpu as pltpu

F32 = jnp.float32
BF16 = jnp.bfloat16
I32 = jnp.int32

D_MODEL = 2048
HEAD_DIM = 128
RET_WIDTH = D_MODEL // 2
FOX_WIDTH = D_MODEL - RET_WIDTH
RET_HEADS = RET_WIDTH // HEAD_DIM
FOX_HEADS = FOX_WIDTH // HEAD_DIM
ROPE_BASE = 10000.0
N_GROUPS = 4
EXPERTS_PER_GROUP = 8
N_EXPERTS = N_GROUPS * EXPERTS_PER_GROUP
D_EXPERT = D_MODEL // 4
N_MOD = 6
MOD_HEAD = 2
EPS = 1e-6
MAIN_COLS = 4 * RET_WIDTH + 3 * FOX_WIDTH

LANES = 128
SUBLANES = 8
ROW_TILES = D_MODEL // LANES
NEG = -1e30
LOG2E = 1.4426950408889634

VMEM_LIMIT = 56 * 1024 * 1024
FUSED_VMEM_LIMIT = 62 * 1024 * 1024

MOD_TN = 1024
IN_TM, IN_TN = 2048, 512
IN_RC = 256
IN_XC = 512
RET_CHUNK = 512
RET_HPS = 2
FOX_TQ, FOX_TK = 512, 512
FOX_HPS = 2
OUT_TM = 512
OUT_RC = 256
EXP_TM = 256
DSP_UNROLL = 8
CMB_TM = 256
DMA_UNROLL = 8


def _cparams(sem):
    return pltpu.CompilerParams(dimension_semantics=sem, vmem_limit_bytes=VMEM_LIMIT)


def _nt_dot(a, b):
    return lax.dot_general(a, b, (((1,), (1,)), ((), ())), preferred_element_type=F32)


def _tn_dot(a, b):
    return lax.dot_general(a, b, (((0,), (0,)), ((), ())), preferred_element_type=F32)


def _sigmoid(x):
    return 1.0 / (1.0 + jnp.exp(-x))


def _col_to_row(col):
    n = col.shape[0]
    ri = lax.broadcasted_iota(I32, (n, n), 0)
    ci = lax.broadcasted_iota(I32, (n, n), 1)
    return jnp.sum(jnp.where(ri == ci, col, 0.0), axis=0, keepdims=True)


def _row_to_col(row):
    n = row.shape[1]
    ri = lax.broadcasted_iota(I32, (n, n), 0)
    ci = lax.broadcasted_iota(I32, (n, n), 1)
    return jnp.sum(jnp.where(ri == ci, row, 0.0), axis=1, keepdims=True)


def _mod_columns(ca_ref, w_ref, b_ref, o_ref):
    for b in range(ca_ref.shape[0]):
        ca = ca_ref[b]
        for j in range(w_ref.shape[1] // LANES):
            sl = slice(j * LANES, (j + 1) * LANES)
            col = jnp.sum(w_ref[:, sl] * ca, axis=0, keepdims=True)
            o_ref[b:b + 1, sl] = col + b_ref[:, sl]


def _mod_head_kernel(cb_ref, w_ref, b_ref, o_ref, ca_ref):
    cv = cb_ref[...]
    ca_ref[...] = cv * _sigmoid(cv)
    _mod_columns(ca_ref, w_ref, b_ref, o_ref)


def _modulation_head(c, ada_w, ada_b3, layer, n_cols):
    nb, d = c.shape
    cb = jnp.broadcast_to(c[:, :, None], (nb, d, LANES))
    return pl.pallas_call(
        _mod_head_kernel,
        grid=(n_cols // MOD_TN,),
        in_specs=[pl.BlockSpec((nb, d, LANES), lambda j: (0, 0, 0)),
                  pl.BlockSpec((None, d, MOD_TN), lambda j: (layer, 0, j)),
                  pl.BlockSpec((None, 1, MOD_TN), lambda j: (layer, 0, j))],
        out_specs=[pl.BlockSpec((nb, MOD_TN), lambda j: (0, j)),
                   pl.BlockSpec((nb, d, LANES), lambda j: (0, 0, 0))],
        out_shape=[jax.ShapeDtypeStruct((nb, n_cols), F32),
                   jax.ShapeDtypeStruct((nb, d, LANES), F32)],
        compiler_params=_cparams(("arbitrary",)),
        name="adaln_mod",
    )(cb, ada_w, ada_b3)


def _rms_rows(x):
    return x * lax.rsqrt(jnp.mean(x * x, axis=-1, keepdims=True) + EPS)


def _inproj_kernel(x_hbm, mod_ref, nw_ref, w_ref, wf_ref, fb_ref, cos_ref, sin_ref, qn_ref, kn_ref,
                   o_ref, lf_ref, h_scr, wb_scr, xbuf, xsem):
    i = pl.program_id(0)
    j = pl.program_id(1)
    tiles_per_sec = RET_WIDTH // IN_TN
    heads_per_tile = IN_TN // HEAD_DIM

    @pl.when(j == 0)
    def _():
        shift = mod_ref[0, 0:1, :]
        scale = 1.0 + mod_ref[0, 1:2, :]
        nw = nw_ref[...]
        n_chunks = IN_TM // IN_XC

        def x_copy(c):
            rows = pl.ds(i * IN_TM + c * IN_XC, IN_XC)
            return pltpu.make_async_copy(x_hbm.at[rows, :], xbuf.at[c % 2], xsem.at[c % 2])

        x_copy(0).start()
        for c in range(n_chunks):
            if c + 1 < n_chunks:
                x_copy(c + 1).start()
            x_copy(c).wait()
            rows = 128
            for r in range(IN_XC // rows):
                h = _rms_rows(xbuf[c % 2, r * rows:(r + 1) * rows, :]) * nw * scale + shift
                h_scr[c * IN_XC + r * rows:c * IN_XC + (r + 1) * rows, :] = h.astype(BF16)
        z = _nt_dot(wf_ref[...], h_scr[...]) + fb_ref[...]
        lf_ref[0] = jnp.minimum(z, 0.0) - jnp.log1p(jnp.exp(-jnp.abs(z)))

    wb_scr[...] = w_ref[...].astype(BF16)
    sec = j // tiles_per_sec

    def section(epilogue):
        for r in range(IN_TM // IN_RC):
            rs = slice(r * IN_RC, (r + 1) * IN_RC)
            acc = _nt_dot(h_scr[rs, :], wb_scr[...])
            epilogue(acc, rs)

    def per_head(fn):
        def epilogue(acc, rs):
            for hh in range(heads_per_tile):
                sl = slice(hh * HEAD_DIM, (hh + 1) * HEAD_DIM)
                o_ref[rs, sl] = fn(acc[:, sl], rs).astype(BF16)
        return epilogue

    def whole(fn):
        def epilogue(acc, rs):
            o_ref[rs, :] = fn(acc).astype(BF16)
        return epilogue

    def rope(v, rs):
        return v * cos_ref[rs, :] + pltpu.roll(v, HEAD_DIM // 2, 1) * sin_ref[rs, :]

    @pl.when(sec == 0)
    def _():
        section(per_head(rope))

    @pl.when(sec == 1)
    def _():
        section(per_head(lambda v, rs: rope(v, rs) * (HEAD_DIM ** -0.5)))

    @pl.when((sec == 2) | (sec == 6))
    def _():
        section(whole(lambda a: a))

    @pl.when(sec == 3)
    def _():
        section(whole(lambda a: a * _sigmoid(a)))

    @pl.when(sec == 4)
    def _():
        section(per_head(lambda v, rs: _rms_rows(v) * qn_ref[...] * (LOG2E * HEAD_DIM ** -0.5)))

    @pl.when(sec == 5)
    def _():
        section(per_head(lambda v, rs: _rms_rows(v) * kn_ref[...]))


def _in_projection(x2, mod_head, norm_w, w_in, fox_f_bias, fox_qn_w, fox_kn_w, cosf, sinf, nb, s, layer):
    t, d = x2.shape
    w_t = jnp.swapaxes(w_in, 1, 2)
    wf_t = w_t[layer, MAIN_COLS:, :].astype(BF16)
    per_b = s // IN_TM
    grid = (t // IN_TM, MAIN_COLS // IN_TN)
    return pl.pallas_call(
        _inproj_kernel,
        grid=grid,
        in_specs=[
            pl.BlockSpec(memory_space=pl.ANY),
            pl.BlockSpec((1, MOD_HEAD, d), lambda i, j: (i // per_b, 0, 0)),
            pl.BlockSpec((1, d), lambda i, j: (0, 0)),
            pl.BlockSpec((None, IN_TN, d), lambda i, j: (layer, j, 0)),
            pl.BlockSpec((FOX_HEADS, d), lambda i, j: (0, 0)),
            pl.BlockSpec((FOX_HEADS, 1), lambda i, j: (0, 0)),
            pl.BlockSpec((IN_TM, HEAD_DIM), lambda i, j: (i % per_b, 0)),
            pl.BlockSpec((IN_TM, HEAD_DIM), lambda i, j: (i % per_b, 0)),
            pl.BlockSpec((1, HEAD_DIM), lambda i, j: (0, 0)),
            pl.BlockSpec((1, HEAD_DIM), lambda i, j: (0, 0)),
        ],
        out_specs=[
            pl.BlockSpec((IN_TM, IN_TN), lambda i, j: (i, j)),
            pl.BlockSpec((1, FOX_HEADS, IN_TM), lambda i, j: (i // per_b, 0, i % per_b)),
        ],
        out_shape=[jax.ShapeDtypeStruct((t, MAIN_COLS), BF16),
                   jax.ShapeDtypeStruct((nb, FOX_HEADS, s), F32)],
        scratch_shapes=[pltpu.VMEM((IN_TM, d), BF16), pltpu.VMEM((IN_TN, d), BF16),
                        pltpu.VMEM((2, IN_XC, d), F32), pltpu.SemaphoreType.DMA((2,))],
        compiler_params=_cparams(("arbitrary", "arbitrary")),
        name="in_projection",
    )(x2, mod_head, norm_w.reshape(1, d), w_t, wf_t, fox_f_bias.reshape(FOX_HEADS, 1), cosf, sinf,
      fox_qn_w.reshape(1, HEAD_DIM), fox_kn_w.reshape(1, HEAD_DIM))


def _split3(x):
    hi = x.astype(BF16)
    r1 = x - hi.astype(F32)
    mid = r1.astype(BF16)
    lo = (r1 - mid.astype(F32)).astype(BF16)
    return hi, mid, lo


def _cumsum_kernel(x_ref, o_ref, *, rows_per_seq):
    x = x_ref[...]
    r = x.shape[0]
    ri = lax.broadcasted_iota(I32, (LANES, LANES), 0)
    ci = lax.broadcasted_iota(I32, (LANES, LANES), 1)
    upper = jnp.where(ri <= ci, 1.0, 0.0).astype(BF16)
    rowcum = sum(jnp.dot(p, upper, preferred_element_type=F32) for p in _split3(x))
    tot = jnp.broadcast_to(rowcum[:, LANES - 1:LANES], (r, LANES))
    gi = lax.broadcasted_iota(I32, (r, r), 0)
    gj = lax.broadcasted_iota(I32, (r, r), 1)
    same_seq = (gi // rows_per_seq) == (gj // rows_per_seq)
    lower = jnp.where(same_seq & (gj < gi), 1.0, 0.0).astype(BF16)
    prefix = sum(jnp.dot(lower, p, preferred_element_type=F32) for p in _split3(tot))
    o_ref[...] = rowcum + prefix


def _cum_forget(lf):
    nb, nh, s = lf.shape
    rows = nb * nh * s // LANES
    out = pl.pallas_call(
        functools.partial(_cumsum_kernel, rows_per_seq=s // LANES),
        out_shape=jax.ShapeDtypeStruct((rows, LANES), F32),
        compiler_params=pltpu.CompilerParams(vmem_limit_bytes=VMEM_LIMIT),
        name="forget_cumsum",
    )(lf.reshape(rows, LANES))
    return out.reshape(nb * nh, 1, s)


def _ret_kernel(lg_ref, q_ref, k_ref, v_ref, g_ref, gnw_ref, o_ref):
    c = RET_CHUNK
    s = q_ref.shape[1]
    ri = lax.broadcasted_iota(I32, (c, c), 0)
    ci = lax.broadcasted_iota(I32, (c, c), 1)
    diff = (ri - ci).astype(F32)
    pos = lax.broadcasted_iota(I32, (c, 1), 0).astype(F32)
    consts = []
    for hh in range(RET_HPS):
        lg = lg_ref[hh][:, 0:1]
        decay = jnp.where(diff >= 0.0, jnp.exp(lg * jnp.maximum(diff, 0.0)), 0.0)
        zeta = jnp.exp(lg * (float(c - 1) - pos))
        xi = jnp.exp(lg * (pos + 1.0))
        chunk_decay = jnp.exp(lg * float(c))
        consts.append((decay, zeta, xi, chunk_decay))

    def body(n, states):
        r0 = n * c
        new_states = []
        for hh in range(RET_HPS):
            decay, zeta, xi, chunk_decay = consts[hh]
            sl = slice(hh * HEAD_DIM, (hh + 1) * HEAD_DIM)
            q = q_ref[0, pl.ds(r0, c), sl]
            k = k_ref[0, pl.ds(r0, c), sl]
            v = v_ref[0, pl.ds(r0, c), sl]
            scores = _nt_dot(q, k) * decay
            intra = jnp.dot(scores.astype(BF16), v, preferred_element_type=F32)
            cross = jnp.dot(q, states[hh].astype(BF16), preferred_element_type=F32) * xi
            kz = (k.astype(F32) * zeta).astype(BF16)
            kv = _tn_dot(kz, v)
            o = intra + cross
            mu = jnp.mean(o, axis=-1, keepdims=True)
            oc = o - mu
            var = jnp.mean(oc * oc, axis=-1, keepdims=True)
            y = oc * lax.rsqrt(var + EPS) * gnw_ref[:, sl] * g_ref[0, pl.ds(r0, c), sl].astype(F32)
            o_ref[0, pl.ds(r0, c), sl] = y.astype(BF16)
            new_states.append(states[hh] * chunk_decay + kv)
        return tuple(new_states)

    states = tuple(jnp.zeros((HEAD_DIM, HEAD_DIM), F32) for _ in range(RET_HPS))
    for n in range(s // c):
        states = body(n, states)


def _retention(proj3, ret_gn_w, lg_tab):
    nb, s, _ = proj3.shape
    w = RET_HPS * HEAD_DIM
    gpg = RET_WIDTH // w

    def col(sec):
        return pl.BlockSpec((1, s, w), lambda b, h: (b, 0, sec * gpg + h))

    return pl.pallas_call(
        _ret_kernel,
        grid=(nb, gpg),
        in_specs=[pl.BlockSpec((RET_HPS, 1, LANES), lambda b, h: (h, 0, 0)),
                  col(0), col(1), col(2), col(3),
                  pl.BlockSpec((1, w), lambda b, h: (0, h))],
        out_specs=pl.BlockSpec((1, s, w), lambda b, h: (b, 0, h)),
        out_shape=jax.ShapeDtypeStruct((nb, s, RET_WIDTH), BF16),
        compiler_params=_cparams(("arbitrary", "arbitrary")),
        name="retention",
    )(lg_tab, proj3, proj3, proj3, proj3, ret_gn_w.reshape(1, RET_WIDTH))


def _fox_kernel(q_ref, k_ref, v_ref, cum_ref, onw_ref, ca_ref, aw_ref, ab_ref, wo_ref,
                o_ref, mod_ref, wob_ref):
    tq, t = FOX_TQ, FOX_TK
    for qi in range(k_ref.shape[1] // tq):
        @pl.when(pl.program_id(2) == qi)
        def _():
            _mod_columns(ca_ref, aw_ref, ab_ref, mod_ref)
            wob_ref[...] = wo_ref[...].astype(BF16)
            _fox_query_tile(qi, q_ref, k_ref, v_ref, cum_ref, onw_ref, o_ref)


def _fox_query_tile(qi, q_ref, k_ref, v_ref, cum_ref, onw_ref, o_ref):
    tq, t = FOX_TQ, FOX_TK
    q0 = qi * tq
    last = q0 // t
    rows = slice(None)
    ri = lax.broadcasted_iota(I32, (tq, t), 0)
    ci = lax.broadcasted_iota(I32, (tq, t), 1)

    def step(ki, carry, masked):
        k0 = ki * t
        out = []
        for hh in range(FOX_HPS):
            m, l, acc = carry[hh]
            sl = slice(hh * HEAD_DIM, (hh + 1) * HEAD_DIM)
            k = k_ref[0, k0:k0 + t, sl]
            v = v_ref[0, k0:k0 + t, sl]
            sc = _nt_dot(q_ref[0, rows, sl], k) - cum_ref[hh, :, k0:k0 + t] * LOG2E
            if masked:
                sc = jnp.where(ci + k0 <= ri + q0, sc, NEG)
            m_new = jnp.maximum(m, jnp.max(sc, axis=1, keepdims=True))
            p = jnp.exp2(sc - m_new)
            alpha = jnp.exp2(m - m_new)
            l = alpha * l + jnp.sum(p, axis=1, keepdims=True)
            acc = alpha * acc + jnp.dot(p.astype(BF16), v, preferred_element_type=F32)
            out.append((m_new, l, acc))
        return tuple(out)

    init = tuple((jnp.full((tq, 1), NEG, F32), jnp.zeros((tq, 1), F32), jnp.zeros((tq, HEAD_DIM), F32))
                 for _ in range(FOX_HPS))
    carry = init
    for ki in range(last):
        carry = step(ki, carry, False)
    carry = step(last, carry, True)
    for hh in range(FOX_HPS):
        _, l, acc = carry[hh]
        sl = slice(hh * HEAD_DIM, (hh + 1) * HEAD_DIM)
        o = acc / l
        o_ref[0, rows, sl] = (_rms_rows(o) * onw_ref[:, sl]).astype(BF16)


def _fox_attention(proj3, cum, fox_on_w, ca, ada_w, ada_b3, w_out, layer, mod_col0):
    nb, s, _ = proj3.shape
    d = ca.shape[1]
    w = FOX_HPS * HEAD_DIM
    base = 4 * RET_WIDTH // w
    gpg = FOX_WIDTH // w
    nq = s // FOX_TQ
    n_steps = nb * gpg * nq
    n_mod = ada_w.shape[2] - mod_col0
    assert n_mod == n_steps * LANES and d % n_steps == 0
    wo_rows = d // n_steps
    step = lambda b, h, i: (b * gpg + h) * nq + i
    return pl.pallas_call(
        _fox_kernel,
        grid=(nb, gpg, nq),
        in_specs=[pl.BlockSpec((1, FOX_TQ, w), lambda b, h, i: (b, i, base + h)),
                  pl.BlockSpec((1, s, w), lambda b, h, i: (b, 0, base + gpg + h)),
                  pl.BlockSpec((1, s, w), lambda b, h, i: (b, 0, base + 2 * gpg + h)),
                  pl.BlockSpec((FOX_HPS, 1, s), lambda b, h, i: (b * gpg + h, 0, 0)),
                  pl.BlockSpec((1, w), lambda b, h, i: (0, h)),
                  pl.BlockSpec((nb, d, LANES), lambda b, h, i: (0, 0, 0)),
                  pl.BlockSpec((None, d, LANES), lambda b, h, i: (layer, 0, mod_col0 // LANES + step(b, h, i))),
                  pl.BlockSpec((None, 1, LANES), lambda b, h, i: (layer, 0, mod_col0 // LANES + step(b, h, i))),
                  pl.BlockSpec((None, wo_rows, d), lambda b, h, i: (layer, step(b, h, i), 0))],
        out_specs=[pl.BlockSpec((1, FOX_TQ, w), lambda b, h, i: (b, i, h)),
                   pl.BlockSpec((nb, LANES), lambda b, h, i: (0, step(b, h, i))),
                   pl.BlockSpec((wo_rows, d), lambda b, h, i: (step(b, h, i), 0))],
        out_shape=[jax.ShapeDtypeStruct((nb, s, FOX_WIDTH), BF16),
                   jax.ShapeDtypeStruct((nb, n_mod), F32),
                   jax.ShapeDtypeStruct((d, d), BF16)],
        compiler_params=_cparams(("arbitrary", "arbitrary", "arbitrary")),
        name="fox_attention",
    )(proj3, proj3, proj3, cum, fox_on_w.reshape(1, FOX_WIDTH), ca, ada_w, ada_b3, w_out)


def _outproj_kernel(ret_ref, fox_ref, wo_ref, x_ref, mod_ref, npost_ref, npre_ref, rw_ref, rb_ref,
                    x1_ref, h2_ref, idx_ref, wts_ref, cnt_ref, carry_scr):
    i = pl.program_id(0)

    @pl.when(i == 0)
    def _():
        carry_scr[...] = jnp.zeros_like(carry_scr)

    idx_ref[...] = jnp.zeros_like(idx_ref)
    subtiles = [slice(r * OUT_RC, (r + 1) * OUT_RC) for r in range(OUT_TM // OUT_RC)]
    ys = [jnp.dot(ret_ref[rs, :], wo_ref[:RET_WIDTH, :], preferred_element_type=F32)
          + jnp.dot(fox_ref[rs, :], wo_ref[RET_WIDTH:, :], preferred_element_type=F32) for rs in subtiles]
    for rs, y in zip(subtiles, ys):
        _outproj_rows(rs, y, x_ref, mod_ref, npost_ref, npre_ref, rw_ref, rb_ref, x1_ref, h2_ref, idx_ref,
                      wts_ref, cnt_ref, carry_scr)


def _outproj_rows(rs, y, x_ref, mod_ref, npost_ref, npre_ref, rw_ref, rb_ref,
                  x1_ref, h2_ref, idx_ref, wts_ref, cnt_ref, carry_scr):
    tm = OUT_RC
    gate1 = mod_ref[0, 2:3, :]
    shift2 = mod_ref[0, 3:4, :]
    scale2 = 1.0 + mod_ref[0, 4:5, :]
    x1 = x_ref[rs, :] + gate1 * (_rms_rows(y) * npost_ref[...])
    x1_ref[rs, :] = x1
    h2 = _rms_rows(x1) * npre_ref[...] * scale2 + shift2
    h2_ref[rs, :, :] = h2.astype(BF16).reshape(tm, ROW_TILES, LANES)

    hi = h2.astype(BF16)
    lo = (h2 - hi.astype(F32)).astype(BF16)
    rw = rw_ref[...]
    lg2 = jnp.dot(hi, rw, preferred_element_type=F32) + jnp.dot(lo, rw, preferred_element_type=F32)
    logits = lg2 + pltpu.roll(lg2, LANES // 2, 1) + rb_ref[...]

    lane = lax.broadcasted_iota(I32, (tm, LANES), 1)
    lanef = lane.astype(F32)
    big = float(LANES)

    def first_argmax(vals):
        vmax = jnp.max(vals, axis=1, keepdims=True)
        idx = jnp.min(jnp.where(vals == vmax, lanef, big), axis=1, keepdims=True)
        return vmax, idx

    is_coarse = (lane >= N_EXPERTS) & (lane < N_EXPERTS + N_GROUPS)
    cvals = jnp.where(is_coarse, logits, NEG)
    cmax, cidx = first_argmax(cvals)
    gidx = cidx - float(N_EXPERTS)
    gprob = 1.0 / jnp.sum(jnp.where(is_coarse, jnp.exp(cvals - cmax), 0.0), axis=1, keepdims=True)
    glo = gidx * float(EXPERTS_PER_GROUP)
    in_group = (lanef >= glo) & (lanef < glo + float(EXPERTS_PER_GROUP))
    fvals = jnp.where(in_group, logits, NEG)
    v1, e1 = first_argmax(fvals)
    fvals2 = jnp.where(lanef == e1, NEG, fvals)
    v2, e2 = first_argmax(fvals2)
    ex = jnp.exp(v2 - v1)
    w1 = gprob * (1.0 / (1.0 + ex))
    w2 = gprob * (ex / (1.0 + ex))

    sel1 = lanef == e1
    sel2 = lanef == e2
    onehot = jnp.where(sel1 | sel2, 1.0, 0.0)
    ri = lax.broadcasted_iota(I32, (tm, tm), 0)
    ci = lax.broadcasted_iota(I32, (tm, tm), 1)
    lower = jnp.where(ci < ri, 1.0, 0.0).astype(BF16)
    before = jnp.dot(lower, onehot.astype(BF16), preferred_element_type=F32) + carry_scr[...]
    r1 = jnp.sum(jnp.where(sel1, before, 0.0), axis=1, keepdims=True)
    r2 = jnp.sum(jnp.where(sel2, before, 0.0), axis=1, keepdims=True)
    carry_new = carry_scr[...] + jnp.sum(onehot, axis=0, keepdims=True)
    carry_scr[...] = carry_new
    cnt_ref[...] = carry_new.astype(I32)

    for row, col in enumerate((e1, e2, r1, r2)):
        idx_ref[row:row + 1, rs] = _col_to_row(col).astype(I32)
    wts_ref[rs, :] = jnp.where(lane == 0, w1, jnp.where(lane == 1, w2, 0.0))


def _out_projection(ret2, fox2, w_out_bf, x2, mod3, npost, npre, rw, rb, s):
    t, d = x2.shape
    per_b = s // OUT_TM
    row = lambda i: (i, 0)
    const = lambda i: (0, 0)
    return pl.pallas_call(
        _outproj_kernel,
        grid=(t // OUT_TM,),
        in_specs=[pl.BlockSpec((OUT_TM, RET_WIDTH), row),
                  pl.BlockSpec((OUT_TM, FOX_WIDTH), row),
                  pl.BlockSpec((d, d), const),
                  pl.BlockSpec((OUT_TM, d), row),
                  pl.BlockSpec((1, N_MOD, d), lambda i: (i // per_b, 0, 0)),
                  pl.BlockSpec((1, d), const),
                  pl.BlockSpec((1, d), const),
                  pl.BlockSpec((d, LANES), const),
                  pl.BlockSpec((1, LANES), const)],
        out_specs=[pl.BlockSpec((OUT_TM, d), row),
                   pl.BlockSpec((OUT_TM, ROW_TILES, LANES), lambda i: (i, 0, 0)),
                   pl.BlockSpec((SUBLANES, OUT_TM), lambda i: (0, i)),
                   pl.BlockSpec((OUT_TM, LANES), row),
                   pl.BlockSpec((1, LANES), const)],
        out_shape=[jax.ShapeDtypeStruct((t, d), F32),
                   jax.ShapeDtypeStruct((t, ROW_TILES, LANES), BF16),
                   jax.ShapeDtypeStruct((SUBLANES, t), I32),
                   jax.ShapeDtypeStruct((t, LANES), F32),
                   jax.ShapeDtypeStruct((1, LANES), I32)],
        scratch_shapes=[pltpu.VMEM((1, LANES), F32)],
        compiler_params=_cparams(("arbitrary",)),
        name="out_projection_router",
    )(ret2, fox2, w_out_bf, x2, mod3, npost.reshape(1, d), npre.reshape(1, d), rw, rb)


PLAN_ROW_START = 0
PLAN_TILE_EXPERT = 1
PLAN_NUM_TILES = 2
PLAN_TILE_END = 3
PLAN_COUNT = 4


def _plan_kernel(idx_ref, cnt_ref, pos_ref, plan_ref):
    lane = lax.broadcasted_iota(I32, (1, LANES), 1)
    cnt = jnp.where(lane < N_EXPERTS, cnt_ref[...], 0).astype(F32)
    tiles = jnp.floor((cnt + float(EXP_TM - 1)) * (1.0 / EXP_TM))
    ri = lax.broadcasted_iota(I32, (LANES, LANES), 0)
    ci = lax.broadcasted_iota(I32, (LANES, LANES), 1)
    upper = jnp.where(ri <= ci, 1.0, 0.0).astype(BF16)
    tiles8 = jnp.broadcast_to(tiles, (SUBLANES, LANES)).astype(BF16)
    tile_end = jnp.dot(tiles8, upper, preferred_element_type=F32)[0:1, :]
    row_start = (tile_end - tiles) * float(EXP_TM)
    num_tiles = jnp.max(tile_end, axis=1, keepdims=True)
    end_col = _row_to_col(tile_end)
    owned = jnp.where((end_col <= ci.astype(F32)) & (ri < N_EXPERTS), 1.0, 0.0)
    tile_expert = jnp.sum(owned, axis=0, keepdims=True)
    last_expert = jnp.max(jnp.where(lane.astype(F32) < num_tiles, tile_expert, 0.0), axis=1, keepdims=True)
    tile_expert = jnp.minimum(tile_expert, last_expert)
    plan_ref[...] = jnp.zeros_like(plan_ref)
    plan_ref[PLAN_ROW_START:PLAN_ROW_START + 1, :] = row_start.astype(I32)
    plan_ref[PLAN_TILE_EXPERT:PLAN_TILE_EXPERT + 1, :] = tile_expert.astype(I32)
    plan_ref[PLAN_NUM_TILES:PLAN_NUM_TILES + 1, :] = jnp.broadcast_to(num_tiles, (1, LANES)).astype(I32)
    plan_ref[PLAN_TILE_END:PLAN_TILE_END + 1, :] = tile_end.astype(I32)
    plan_ref[PLAN_COUNT:PLAN_COUNT + 1, :] = cnt.astype(I32)

    experts = idx_ref[0:2, :]
    start_i = row_start.astype(I32)
    base = jnp.zeros(experts.shape, I32)
    for e in range(N_EXPERTS):
        base = jnp.where(experts == e, start_i[:, e:e + 1], base)
    pos_ref[...] = jnp.zeros_like(pos_ref)
    pos_ref[0:2, :] = idx_ref[2:4, :] + base


def _route_plan(idx, counts):
    t = idx.shape[1]
    return pl.pallas_call(
        _plan_kernel,
        out_shape=[jax.ShapeDtypeStruct((SUBLANES, t), I32),
                   jax.ShapeDtypeStruct((SUBLANES, LANES), I32)],
        compiler_params=pltpu.CompilerParams(vmem_limit_bytes=VMEM_LIMIT),
        name="route_plan",
    )(idx, counts)


def _fused_expert_kernel(pa_ref, pb_ref, plan_ref, h2_hbm, w1_hbm, w3_hbm, w2_hbm, ys_ref,
                         inv, h2v, tsem, xbuf, w1s, w3s, w2s, wsem, w1b, w3b, w2b, *, layer):
    i = pl.program_id(0)
    tm = EXP_TM
    n_tok = h2_hbm.shape[0]
    nt = plan_ref[PLAN_NUM_TILES, 0]
    expert = plan_ref[PLAN_TILE_EXPERT, i]
    table_copy = pltpu.make_async_copy(h2_hbm, h2v, tsem)

    def weight_copies(e):
        return (pltpu.make_async_copy(w1_hbm.at[layer, e], w1s, wsem.at[0]),
                pltpu.make_async_copy(w3_hbm.at[layer, e], w3s, wsem.at[1]),
                pltpu.make_async_copy(w2_hbm.at[layer, e], w2s, wsem.at[2]))

    @pl.when(i == 0)
    def _():
        for cp in weight_copies(plan_ref[PLAN_TILE_EXPERT, 0]):
            cp.start()
        table_copy.start()

        def fill(p, _):
            inv[p] = 0
            return 0
        for e in range(N_EXPERTS):
            lax.fori_loop(plan_ref[PLAN_ROW_START, e] + plan_ref[PLAN_COUNT, e],
                          plan_ref[PLAN_TILE_END, e] * tm, fill, 0)

        def scatter(j, _):
            for u in range(DMA_UNROLL):
                tok = j * DMA_UNROLL + u
                inv[pa_ref[tok]] = tok
                inv[pb_ref[tok]] = tok
            return 0
        lax.fori_loop(0, n_tok // DMA_UNROLL, scatter, 0)
        table_copy.wait()

        def first_rows(j, _):
            for u in range(DSP_UNROLL):
                r = j * DSP_UNROLL + u
                xbuf[0, r] = h2v[inv[r]]
            return 0
        lax.fori_loop(0, tm // DSP_UNROLL, first_rows, 0)

    valid = i < nt
    first = (i == 0) | (expert != plan_ref[PLAN_TILE_EXPERT, jnp.maximum(i - 1, 0)])

    def tile(parity, new_expert):
        if new_expert:
            for cp in weight_copies(expert):
                cp.wait()
        nxt_base = jnp.minimum(i + 1, nt - 1) * tm
        for r in range(tm):
            xbuf[1 - parity, r] = h2v[inv[nxt_base + r]]
        xt = xbuf[parity].reshape(tm, D_MODEL)
        if new_expert:
            w1b[...] = w1s[...].astype(BF16)
        a = jnp.dot(xt, w1b[...], preferred_element_type=F32)
        if new_expert:
            w3b[...] = w3s[...].astype(BF16)
        g = jnp.dot(xt, w3b[...], preferred_element_type=F32)
        act = (a * _sigmoid(a) * g).astype(BF16)
        if new_expert:
            w2b[...] = w2s[...].astype(BF16)
        y = jnp.dot(act, w2b[...], preferred_element_type=F32)
        ys_ref[...] = y.astype(BF16).reshape(tm, ROW_TILES, LANES)
        if new_expert:
            nxt = plan_ref[PLAN_TILE_END, expert]

            @pl.when(nxt < nt)
            def _():
                for cp in weight_copies(plan_ref[PLAN_TILE_EXPERT, nxt]):
                    cp.start()

    for parity in range(2):
        for new_expert in (False, True):
            here = first if new_expert else jnp.logical_not(first)
            pl.when(valid & (i % 2 == parity) & here)(functools.partial(tile, parity, new_expert))

    @pl.when(jnp.logical_not(valid))
    def _():
        ys_ref[...] = jnp.zeros_like(ys_ref)


def _fused_expert_gemm(h2t, pos_a, pos_b, plan, w1, w3, w2, layer):
    t = h2t.shape[0]
    n_tiles = (t * 2) // EXP_TM + N_EXPERTS
    n_rows = n_tiles * EXP_TM
    d, de = w1.shape[2], w1.shape[3]
    any_spec = pl.BlockSpec(memory_space=pl.ANY)
    return pl.pallas_call(
        functools.partial(_fused_expert_kernel, layer=layer),
        grid_spec=pltpu.PrefetchScalarGridSpec(
            num_scalar_prefetch=3,
            grid=(n_tiles,),
            in_specs=[any_spec, any_spec, any_spec, any_spec],
            out_specs=pl.BlockSpec((EXP_TM, ROW_TILES, LANES), lambda i, pa, pb, plan: (i, 0, 0)),
            scratch_shapes=[pltpu.SMEM((n_rows,), I32),
                            pltpu.VMEM((t, ROW_TILES, LANES), BF16),
                            pltpu.SemaphoreType.DMA(()),
                            pltpu.VMEM((2, EXP_TM, ROW_TILES, LANES), BF16),
                            pltpu.VMEM((d, de), F32),
                            pltpu.VMEM((d, de), F32),
                            pltpu.VMEM((de, d), F32),
                            pltpu.SemaphoreType.DMA((3,)),
                            pltpu.VMEM((d, de), BF16),
                            pltpu.VMEM((d, de), BF16),
                            pltpu.VMEM((de, d), BF16)]),
        out_shape=jax.ShapeDtypeStruct((n_rows, ROW_TILES, LANES), BF16),
        compiler_params=pltpu.CompilerParams(dimension_semantics=("arbitrary",),
                                             vmem_limit_bytes=FUSED_VMEM_LIMIT),
        name="moe_expert_gemm",
    )(pos_a, pos_b, plan, h2t, w1, w3, w2)


def _combine_kernel(pa_ref, pb_ref, ys_hbm, x1_ref, wts_ref, mod_ref, npost_ref, o_ref, buf_a, buf_b, sems):
    i = pl.program_id(0)
    n = pl.num_programs(0)
    tm = CMB_TM

    def start_tile(tile, slot):
        def body(j, _):
            for u in range(DMA_UNROLL):
                r = j * DMA_UNROLL + u
                tok = tile * tm + r
                pltpu.make_async_copy(ys_hbm.at[pa_ref[tok]], buf_a.at[slot, r], sems.at[0, slot]).start()
                pltpu.make_async_copy(ys_hbm.at[pb_ref[tok]], buf_b.at[slot, r], sems.at[1, slot]).start()
            return 0
        lax.fori_loop(0, tm // DMA_UNROLL, body, 0)

    def wait_tile(slot):
        pltpu.make_async_copy(ys_hbm.at[pl.ds(0, tm)], buf_a.at[slot], sems.at[0, slot]).wait()
        pltpu.make_async_copy(ys_hbm.at[pl.ds(0, tm)], buf_b.at[slot], sems.at[1, slot]).wait()

    slot = i % 2

    @pl.when(i == 0)
    def _():
        start_tile(0, 0)

    @pl.when(i + 1 < n)
    def _():
        start_tile(i + 1, 1 - slot)

    wait_tile(slot)

    wa = wts_ref[:, 0:1]
    wb = wts_ref[:, 1:2]
    ya = buf_a[slot].reshape(tm, D_MODEL).astype(F32)
    yb = buf_b[slot].reshape(tm, D_MODEL).astype(F32)
    y = wa * ya + wb * yb
    gate2 = mod_ref[0, 5:6, :]
    o_ref[...] = x1_ref[...] + gate2 * (_rms_rows(y) * npost_ref[...])


def _combine(ys, pos_a, pos_b, x1, wts, mod3, npost, s):
    t, d = x1.shape
    per_b = s // CMB_TM
    row = lambda i, pa, pb: (i, 0)
    return pl.pallas_call(
        _combine_kernel,
        grid_spec=pltpu.PrefetchScalarGridSpec(
            num_scalar_prefetch=2,
            grid=(t // CMB_TM,),
            in_specs=[pl.BlockSpec(memory_space=pl.ANY),
                      pl.BlockSpec((CMB_TM, d), row),
                      pl.BlockSpec((CMB_TM, LANES), row),
                      pl.BlockSpec((1, N_MOD, d), lambda i, pa, pb: (i // per_b, 0, 0)),
                      pl.BlockSpec((1, d), lambda i, pa, pb: (0, 0))],
            out_specs=pl.BlockSpec((CMB_TM, d), row),
            scratch_shapes=[pltpu.VMEM((2, CMB_TM, ROW_TILES, LANES), BF16),
                            pltpu.VMEM((2, CMB_TM, ROW_TILES, LANES), BF16),
                            pltpu.SemaphoreType.DMA((2, 2))]),
        out_shape=jax.ShapeDtypeStruct((t, d), F32),
        compiler_params=_cparams(("arbitrary",)),
        name="moe_combine",
    )(pos_a, pos_b, ys, x1, wts, mod3, npost.reshape(1, d))


def _rope_tables(s):
    inv_freq = ROPE_BASE ** (-np.arange(0, HEAD_DIM, 2, dtype=np.float64) / HEAD_DIM)
    angle = np.arange(s, dtype=np.float64)[:, None] * inv_freq[None, :]
    cos, sin = np.cos(angle), np.sin(angle)
    return (jnp.asarray(np.concatenate([cos, cos], axis=-1), F32),
            jnp.asarray(np.concatenate([-sin, sin], axis=-1), F32))


def _router_weights(router_coarse_w, router_fine_w, router_coarse_b, router_fine_b):
    d = router_fine_w.shape[0]
    w = jnp.concatenate([router_fine_w, router_coarse_w], axis=1)
    n = w.shape[1]
    w_hi = w.astype(BF16)
    w_lo = (w - w_hi.astype(F32)).astype(BF16)
    pad = jnp.zeros((d, LANES // 2 - n), BF16)
    rw = jnp.concatenate([w_hi, pad, w_lo, pad], axis=1)
    rb = jnp.concatenate([router_fine_b, router_coarse_b, jnp.zeros((LANES - n,), F32)]).reshape(1, LANES)
    return rw, rb


def _layer(x2, c, nb, s, cosf, sinf, lg_tab, layer, ada_w, ada_b, norm_pre_mix, norm_post_mix, w_in, fox_f_bias,
           ret_gn_w, fox_qn_w, fox_kn_w, fox_on_w, w_out, norm_pre_ffn, norm_post_ffn,
           router_coarse_w, router_coarse_b, router_fine_w, router_fine_b, w1, w3, w2):
    t, d = x2.shape
    ada_b3 = ada_b.reshape(ada_b.shape[0], 1, N_MOD * d)
    mod_head, ca = _modulation_head(c, ada_w, ada_b3, layer, MOD_HEAD * d)
    mod_head = mod_head.reshape(nb, MOD_HEAD, d)

    proj, lf = _in_projection(x2, mod_head, norm_pre_mix[layer], w_in, fox_f_bias[layer], fox_qn_w[layer],
                              fox_kn_w[layer], cosf, sinf, nb, s, layer)
    cum = _cum_forget(lf)
    proj3 = proj.reshape(nb, s, MAIN_COLS)
    ret = _retention(proj3, ret_gn_w[layer], lg_tab)
    fox, mod_tail, w_out_bf = _fox_attention(proj3, cum, fox_on_w[layer], ca, ada_w, ada_b3, w_out, layer,
                                             MOD_HEAD * d)
    mod3 = jnp.concatenate([mod_head, mod_tail.reshape(nb, N_MOD - MOD_HEAD, d)], axis=1)

    rw, rb = _router_weights(router_coarse_w[layer], router_fine_w[layer], router_coarse_b[layer],
                             router_fine_b[layer])
    x1, h2t, idx, wts, counts = _out_projection(
        ret.reshape(t, RET_WIDTH), fox.reshape(t, FOX_WIDTH), w_out_bf, x2, mod3,
        norm_post_mix[layer], norm_pre_ffn[layer], rw, rb, s)

    pos, plan = _route_plan(idx, counts)
    pos_a, pos_b = pos[0], pos[1]
    ys = _fused_expert_gemm(h2t, pos_a, pos_b, plan, w1, w3, w2, layer)
    return _combine(ys, pos_a, pos_b, x1, wts, mod3, norm_post_ffn[layer], s)


def kernel(x, c, ada_w, ada_b, norm_pre_mix, norm_post_mix, w_in, fox_f_bias, ret_gn_w, fox_qn_w, fox_kn_w,
           fox_on_w, w_out, norm_pre_ffn, norm_post_ffn, router_coarse_w, router_coarse_b, router_fine_w,
           router_fine_b, w1, w3, w2):
    nb, s, d = x.shape
    depth = ada_w.shape[0]
    cosf, sinf = _rope_tables(s)
    log_gamma = jnp.log1p(-jnp.exp2(-5.0 - jnp.arange(RET_HEADS, dtype=F32)))
    lg_tab = jnp.broadcast_to(log_gamma[:, None, None], (RET_HEADS, 1, LANES))
    x2 = x.reshape(nb * s, d)
    for layer in range(depth):
        x2 = _layer(x2, c, nb, s, cosf, sinf, lg_tab, layer, ada_w, ada_b, norm_pre_mix, norm_post_mix,
                    w_in, fox_f_bias, ret_gn_w, fox_qn_w, fox_kn_w, fox_on_w, w_out, norm_pre_ffn,
                    norm_post_ffn, router_coarse_w, router_coarse_b, router_fine_w, router_fine_b, w1, w3, w2)
    return x2.reshape(nb, s, d)
```

```python
import functools

import jax
import jax.numpy as jnp
import numpy as np
from jax import lax
from jax.experimental import pallas as pl
from jax.experimental.pallas import tpu as pltpu

F32 = jnp.float32
BF16 = jnp.bfloat16
I32 = jnp.int32

D_MODEL = 2048
HEAD_DIM = 128
RET_WIDTH = D_MODEL // 2
FOX_WIDTH = D_MODEL - RET_WIDTH
RET_HEADS = RET_WIDTH // HEAD_DIM
FOX_HEADS = FOX_WIDTH // HEAD_DIM
ROPE_BASE = 10000.0
N_GROUPS = 4
EXPERTS_PER_GROUP = 8
N_EXPERTS = N_GROUPS * EXPERTS_PER_GROUP
D_EXPERT = D_MODEL // 4
N_MOD = 6
MOD_HEAD = 2
EPS = 1e-6
MAIN_COLS = 4 * RET_WIDTH + 3 * FOX_WIDTH

LANES = 128
SUBLANES = 8
ROW_TILES = D_MODEL // LANES
NEG = -1e30
LOG2E = 1.4426950408889634

VMEM_LIMIT = 56 * 1024 * 1024
FUSED_VMEM_LIMIT = 62 * 1024 * 1024

MOD_TN = 1024
IN_TM, IN_TN = 2048, 512
IN_RC = 256
IN_XC = 512
RET_CHUNK = 512
RET_HPS = 2
FOX_TQ, FOX_TK = 512, 512
FOX_HPS = 2
OUT_TM = 512
OUT_RC = 256
EXP_TM = 256
DSP_UNROLL = 8
CMB_TM = 256
DMA_UNROLL = 8


def _cparams(sem):
    return pltpu.CompilerParams(dimension_semantics=sem, vmem_limit_bytes=VMEM_LIMIT)


def _nt_dot(a, b):
    return lax.dot_general(a, b, (((1,), (1,)), ((), ())), preferred_element_type=F32)


def _tn_dot(a, b):
    return lax.dot_general(a, b, (((0,), (0,)), ((), ())), preferred_element_type=F32)


def _sigmoid(x):
    return 1.0 / (1.0 + jnp.exp(-x))


def _col_to_row(col):
    n = col.shape[0]
    ri = lax.broadcasted_iota(I32, (n, n), 0)
    ci = lax.broadcasted_iota(I32, (n, n), 1)
    return jnp.sum(jnp.where(ri == ci, col, 0.0), axis=0, keepdims=True)


def _row_to_col(row):
    n = row.shape[1]
    ri = lax.broadcasted_iota(I32, (n, n), 0)
    ci = lax.broadcasted_iota(I32, (n, n), 1)
    return jnp.sum(jnp.where(ri == ci, row, 0.0), axis=1, keepdims=True)


def _mod_columns(ca_ref, w_ref, b_ref, o_ref):
    for b in range(ca_ref.shape[0]):
        ca = ca_ref[b]
        for j in range(w_ref.shape[1] // LANES):
            sl = slice(j * LANES, (j + 1) * LANES)
            col = jnp.sum(w_ref[:, sl] * ca, axis=0, keepdims=True)
            o_ref[b:b + 1, sl] = col + b_ref[:, sl]


def _mod_head_kernel(cb_ref, w_ref, b_ref, o_ref, ca_ref):
    cv = cb_ref[...]
    ca_ref[...] = cv * _sigmoid(cv)
    _mod_columns(ca_ref, w_ref, b_ref, o_ref)


def _modulation_head(c, ada_w, ada_b3, layer, n_cols):
    nb, d = c.shape
    cb = jnp.broadcast_to(c[:, :, None], (nb, d, LANES))
    return pl.pallas_call(
        _mod_head_kernel,
        grid=(n_cols // MOD_TN,),
        in_specs=[pl.BlockSpec((nb, d, LANES), lambda j: (0, 0, 0)),
                  pl.BlockSpec((None, d, MOD_TN), lambda j: (layer, 0, j)),
                  pl.BlockSpec((None, 1, MOD_TN), lambda j: (layer, 0, j))],
        out_specs=[pl.BlockSpec((nb, MOD_TN), lambda j: (0, j)),
                   pl.BlockSpec((nb, d, LANES), lambda j: (0, 0, 0))],
        out_shape=[jax.ShapeDtypeStruct((nb, n_cols), F32),
                   jax.ShapeDtypeStruct((nb, d, LANES), F32)],
        compiler_params=_cparams(("arbitrary",)),
        name="adaln_mod",
    )(cb, ada_w, ada_b3)


def _rms_rows(x):
    return x * lax.rsqrt(jnp.mean(x * x, axis=-1, keepdims=True) + EPS)


def _inproj_kernel(x_hbm, mod_ref, nw_ref, w_ref, wf_ref, fb_ref, cos_ref, sin_ref, qn_ref, kn_ref,
                   o_ref, lf_ref, h_scr, wb_scr, xbuf, xsem):
    i = pl.program_id(0)
    j = pl.program_id(1)
    tiles_per_sec = RET_WIDTH // IN_TN
    heads_per_tile = IN_TN // HEAD_DIM

    @pl.when(j == 0)
    def _():
        shift = mod_ref[0, 0:1, :]
        scale = 1.0 + mod_ref[0, 1:2, :]
        nw = nw_ref[...]
        n_chunks = IN_TM // IN_XC

        def x_copy(c):
            rows = pl.ds(i * IN_TM + c * IN_XC, IN_XC)
            return pltpu.make_async_copy(x_hbm.at[rows, :], xbuf.at[c % 2], xsem.at[c % 2])

        x_copy(0).start()
        for c in range(n_chunks):
            if c + 1 < n_chunks:
                x_copy(c + 1).start()
            x_copy(c).wait()
            rows = 128
            for r in range(IN_XC // rows):
                h = _rms_rows(xbuf[c % 2, r * rows:(r + 1) * rows, :]) * nw * scale + shift
                h_scr[c * IN_XC + r * rows:c * IN_XC + (r + 1) * rows, :] = h.astype(BF16)
        z = _nt_dot(wf_ref[...], h_scr[...]) + fb_ref[...]
        lf_ref[0] = jnp.minimum(z, 0.0) - jnp.log1p(jnp.exp(-jnp.abs(z)))

    wb_scr[...] = w_ref[...].astype(BF16)
    sec = j // tiles_per_sec

    def section(epilogue):
        for r in range(IN_TM // IN_RC):
            rs = slice(r * IN_RC, (r + 1) * IN_RC)
            acc = _nt_dot(h_scr[rs, :], wb_scr[...])
            epilogue(acc, rs)

    def per_head(fn):
        def epilogue(acc, rs):
            for hh in range(heads_per_tile):
                sl = slice(hh * HEAD_DIM, (hh + 1) * HEAD_DIM)
                o_ref[rs, sl] = fn(acc[:, sl], rs).astype(BF16)
        return epilogue

    def whole(fn):
        def epilogue(acc, rs):
            o_ref[rs, :] = fn(acc).astype(BF16)
        return epilogue

    def rope(v, rs):
        return v * cos_ref[rs, :] + pltpu.roll(v, HEAD_DIM // 2, 1) * sin_ref[rs, :]

    @pl.when(sec == 0)
    def _():
        section(per_head(rope))

    @pl.when(sec == 1)
    def _():
        section(per_head(lambda v, rs: rope(v, rs) * (HEAD_DIM ** -0.5)))

    @pl.when((sec == 2) | (sec == 6))
    def _():
        section(whole(lambda a: a))

    @pl.when(sec == 3)
    def _():
        section(whole(lambda a: a * _sigmoid(a)))

    @pl.when(sec == 4)
    def _():
        section(per_head(lambda v, rs: _rms_rows(v) * qn_ref[...] * (LOG2E * HEAD_DIM ** -0.5)))

    @pl.when(sec == 5)
    def _():
        section(per_head(lambda v, rs: _rms_rows(v) * kn_ref[...]))


def _in_projection(x2, mod_head, norm_w, w_in, fox_f_bias, fox_qn_w, fox_kn_w, cosf, sinf, nb, s, layer):
    t, d = x2.shape
    w_t = jnp.swapaxes(w_in, 1, 2)
    wf_t = w_t[layer, MAIN_COLS:, :].astype(BF16)
    per_b = s // IN_TM
    grid = (t // IN_TM, MAIN_COLS // IN_TN)
    return pl.pallas_call(
        _inproj_kernel,
        grid=grid,
        in_specs=[
            pl.BlockSpec(memory_space=pl.ANY),
            pl.BlockSpec((1, MOD_HEAD, d), lambda i, j: (i // per_b, 0, 0)),
            pl.BlockSpec((1, d), lambda i, j: (0, 0)),
            pl.BlockSpec((None, IN_TN, d), lambda i, j: (layer, j, 0)),
            pl.BlockSpec((FOX_HEADS, d), lambda i, j: (0, 0)),
            pl.BlockSpec((FOX_HEADS, 1), lambda i, j: (0, 0)),
            pl.BlockSpec((IN_TM, HEAD_DIM), lambda i, j: (i % per_b, 0)),
            pl.BlockSpec((IN_TM, HEAD_DIM), lambda i, j: (i % per_b, 0)),
            pl.BlockSpec((1, HEAD_DIM), lambda i, j: (0, 0)),
            pl.BlockSpec((1, HEAD_DIM), lambda i, j: (0, 0)),
        ],
        out_specs=[
            pl.BlockSpec((IN_TM, IN_TN), lambda i, j: (i, j)),
            pl.BlockSpec((1, FOX_HEADS, IN_TM), lambda i, j: (i // per_b, 0, i % per_b)),
        ],
        out_shape=[jax.ShapeDtypeStruct((t, MAIN_COLS), BF16),
                   jax.ShapeDtypeStruct((nb, FOX_HEADS, s), F32)],
        scratch_shapes=[pltpu.VMEM((IN_TM, d), BF16), pltpu.VMEM((IN_TN, d), BF16),
                        pltpu.VMEM((2, IN_XC, d), F32), pltpu.SemaphoreType.DMA((2,))],
        compiler_params=_cparams(("arbitrary", "arbitrary")),
        name="in_projection",
    )(x2, mod_head, norm_w.reshape(1, d), w_t, wf_t, fox_f_bias.reshape(FOX_HEADS, 1), cosf, sinf,
      fox_qn_w.reshape(1, HEAD_DIM), fox_kn_w.reshape(1, HEAD_DIM))


def _split3(x):
    hi = x.astype(BF16)
    r1 = x - hi.astype(F32)
    mid = r1.astype(BF16)
    lo = (r1 - mid.astype(F32)).astype(BF16)
    return hi, mid, lo


def _cumsum_kernel(x_ref, o_ref, *, rows_per_seq):
    x = x_ref[...]
    r = x.shape[0]
    ri = lax.broadcasted_iota(I32, (LANES, LANES), 0)
    ci = lax.broadcasted_iota(I32, (LANES, LANES), 1)
    upper = jnp.where(ri <= ci, 1.0, 0.0).astype(BF16)
    rowcum = sum(jnp.dot(p, upper, preferred_element_type=F32) for p in _split3(x))
    tot = jnp.broadcast_to(rowcum[:, LANES - 1:LANES], (r, LANES))
    gi = lax.broadcasted_iota(I32, (r, r), 0)
    gj = lax.broadcasted_iota(I32, (r, r), 1)
    same_seq = (gi // rows_per_seq) == (gj // rows_per_seq)
    lower = jnp.where(same_seq & (gj < gi), 1.0, 0.0).astype(BF16)
    prefix = sum(jnp.dot(lower, p, preferred_element_type=F32) for p in _split3(tot))
    o_ref[...] = rowcum + prefix


def _cum_forget(lf):
    nb, nh, s = lf.shape
    rows = nb * nh * s // LANES
    out = pl.pallas_call(
        functools.partial(_cumsum_kernel, rows_per_seq=s // LANES),
        out_shape=jax.ShapeDtypeStruct((rows, LANES), F32),
        compiler_params=pltpu.CompilerParams(vmem_limit_bytes=VMEM_LIMIT),
        name="forget_cumsum",
    )(lf.reshape(rows, LANES))
    return out.reshape(nb * nh, 1, s)


def _ret_kernel(lg_ref, q_ref, k_ref, v_ref, g_ref, gnw_ref, o_ref):
    c = RET_CHUNK
    s = q_ref.shape[1]
    ri = lax.broadcasted_iota(I32, (c, c), 0)
    ci = lax.broadcasted_iota(I32, (c, c), 1)
    diff = (ri - ci).astype(F32)
    pos = lax.broadcasted_iota(I32, (c, 1), 0).astype(F32)
    consts = []
    for hh in range(RET_HPS):
        lg = lg_ref[hh][:, 0:1]
        decay = jnp.where(diff >= 0.0, jnp.exp(lg * jnp.maximum(diff, 0.0)), 0.0)
        zeta = jnp.exp(lg * (float(c - 1) - pos))
        xi = jnp.exp(lg * (pos + 1.0))
        chunk_decay = jnp.exp(lg * float(c))
        consts.append((decay, zeta, xi, chunk_decay))

    def body(n, states):
        r0 = n * c
        new_states = []
        for hh in range(RET_HPS):
            decay, zeta, xi, chunk_decay = consts[hh]
            sl = slice(hh * HEAD_DIM, (hh + 1) * HEAD_DIM)
            q = q_ref[0, pl.ds(r0, c), sl]
            k = k_ref[0, pl.ds(r0, c), sl]
            v = v_ref[0, pl.ds(r0, c), sl]
            scores = _nt_dot(q, k) * decay
            intra = jnp.dot(scores.astype(BF16), v, preferred_element_type=F32)
            cross = jnp.dot(q, states[hh].astype(BF16), preferred_element_type=F32) * xi
            kz = (k.astype(F32) * zeta).astype(BF16)
            kv = _tn_dot(kz, v)
            o = intra + cross
            mu = jnp.mean(o, axis=-1, keepdims=True)
            oc = o - mu
            var = jnp.mean(oc * oc, axis=-1, keepdims=True)
            y = oc * lax.rsqrt(var + EPS) * gnw_ref[:, sl] * g_ref[0, pl.ds(r0, c), sl].astype(F32)
            o_ref[0, pl.ds(r0, c), sl] = y.astype(BF16)
            new_states.append(states[hh] * chunk_decay + kv)
        return tuple(new_states)

    states = tuple(jnp.zeros((HEAD_DIM, HEAD_DIM), F32) for _ in range(RET_HPS))
    for n in range(s // c):
        states = body(n, states)


def _retention(proj3, ret_gn_w, lg_tab):
    nb, s, _ = proj3.shape
    w = RET_HPS * HEAD_DIM
    gpg = RET_WIDTH // w

    def col(sec):
        return pl.BlockSpec((1, s, w), lambda b, h: (b, 0, sec * gpg + h))

    return pl.pallas_call(
        _ret_kernel,
        grid=(nb, gpg),
        in_specs=[pl.BlockSpec((RET_HPS, 1, LANES), lambda b, h: (h, 0, 0)),
                  col(0), col(1), col(2), col(3),
                  pl.BlockSpec((1, w), lambda b, h: (0, h))],
        out_specs=pl.BlockSpec((1, s, w), lambda b, h: (b, 0, h)),
        out_shape=jax.ShapeDtypeStruct((nb, s, RET_WIDTH), BF16),
        compiler_params=_cparams(("arbitrary", "arbitrary")),
        name="retention",
    )(lg_tab, proj3, proj3, proj3, proj3, ret_gn_w.reshape(1, RET_WIDTH))


def _fox_kernel(q_ref, k_ref, v_ref, cum_ref, onw_ref, ca_ref, aw_ref, ab_ref, wo_ref,
                o_ref, mod_ref, wob_ref):
    tq, t = FOX_TQ, FOX_TK
    for qi in range(k_ref.shape[1] // tq):
        @pl.when(pl.program_id(2) == qi)
        def _():
            _mod_columns(ca_ref, aw_ref, ab_ref, mod_ref)
            wob_ref[...] = wo_ref[...].astype(BF16)
            _fox_query_tile(qi, q_ref, k_ref, v_ref, cum_ref, onw_ref, o_ref)


def _fox_query_tile(qi, q_ref, k_ref, v_ref, cum_ref, onw_ref, o_ref):
    tq, t = FOX_TQ, FOX_TK
    q0 = qi * tq
    last = q0 // t
    rows = slice(None)
    ri = lax.broadcasted_iota(I32, (tq, t), 0)
    ci = lax.broadcasted_iota(I32, (tq, t), 1)

    ones = jnp.ones((t, HEAD_DIM), BF16)

    def step(ki, carry, masked):
        k0 = ki * t
        out = []
        for hh in range(FOX_HPS):
            m, acc = carry[hh]
            sl = slice(hh * HEAD_DIM, (hh + 1) * HEAD_DIM)
            k = k_ref[0, k0:k0 + t, sl]
            v1 = jnp.concatenate([v_ref[0, k0:k0 + t, sl], ones], axis=1)
            sc = _nt_dot(q_ref[0, rows, sl], k) - cum_ref[hh, :, k0:k0 + t] * LOG2E
            if masked:
                sc = jnp.where(ci + k0 <= ri + q0, sc, NEG)
            m_new = jnp.maximum(m, jnp.max(sc, axis=1, keepdims=True))
            p = jnp.exp2((sc - m_new).astype(BF16))
            alpha = jnp.exp2(m - m_new)
            acc = alpha * acc + jnp.dot(p, v1, preferred_element_type=F32)
            out.append((m_new, acc))
        return tuple(out)

    init = tuple((jnp.full((tq, 1), NEG, F32), jnp.zeros((tq, 2 * HEAD_DIM), F32)) for _ in range(FOX_HPS))
    carry = init
    for ki in range(last):
        carry = step(ki, carry, False)
    carry = step(last, carry, True)
    for hh in range(FOX_HPS):
        _, acc = carry[hh]
        sl = slice(hh * HEAD_DIM, (hh + 1) * HEAD_DIM)
        o = acc[:, :HEAD_DIM] / acc[:, HEAD_DIM:]
        o_ref[0, rows, sl] = (_rms_rows(o) * onw_ref[:, sl]).astype(BF16)


def _fox_attention(proj3, cum, fox_on_w, ca, ada_w, ada_b3, w_out, layer, mod_col0):
    nb, s, _ = proj3.shape
    d = ca.shape[1]
    w = FOX_HPS * HEAD_DIM
    base = 4 * RET_WIDTH // w
    gpg = FOX_WIDTH // w
    nq = s // FOX_TQ
    n_steps = nb * gpg * nq
    n_mod = ada_w.shape[2] - mod_col0
    assert n_mod == n_steps * LANES and d % n_steps == 0
    wo_rows = d // n_steps
    step = lambda b, h, i: (b * gpg + h) * nq + i
    return pl.pallas_call(
        _fox_kernel,
        grid=(nb, gpg, nq),
        in_specs=[pl.BlockSpec((1, FOX_TQ, w), lambda b, h, i: (b, i, base + h)),
                  pl.BlockSpec((1, s, w), lambda b, h, i: (b, 0, base + gpg + h)),
                  pl.BlockSpec((1, s, w), lambda b, h, i: (b, 0, base + 2 * gpg + h)),
                  pl.BlockSpec((FOX_HPS, 1, s), lambda b, h, i: (b * gpg + h, 0, 0)),
                  pl.BlockSpec((1, w), lambda b, h, i: (0, h)),
                  pl.BlockSpec((nb, d, LANES), lambda b, h, i: (0, 0, 0)),
                  pl.BlockSpec((None, d, LANES), lambda b, h, i: (layer, 0, mod_col0 // LANES + step(b, h, i))),
                  pl.BlockSpec((None, 1, LANES), lambda b, h, i: (layer, 0, mod_col0 // LANES + step(b, h, i))),
                  pl.BlockSpec((None, wo_rows, d), lambda b, h, i: (layer, step(b, h, i), 0))],
        out_specs=[pl.BlockSpec((1, FOX_TQ, w), lambda b, h, i: (b, i, h)),
                   pl.BlockSpec((nb, LANES), lambda b, h, i: (0, step(b, h, i))),
                   pl.BlockSpec((wo_rows, d), lambda b, h, i: (step(b, h, i), 0))],
        out_shape=[jax.ShapeDtypeStruct((nb, s, FOX_WIDTH), BF16),
                   jax.ShapeDtypeStruct((nb, n_mod), F32),
                   jax.ShapeDtypeStruct((d, d), BF16)],
        compiler_params=_cparams(("arbitrary", "arbitrary", "arbitrary")),
        name="fox_attention",
    )(proj3, proj3, proj3, cum, fox_on_w.reshape(1, FOX_WIDTH), ca, ada_w, ada_b3, w_out)


def _outproj_kernel(ret_ref, fox_ref, wo_ref, x_ref, mod_ref, npost_ref, npre_ref, rw_ref, rb_ref,
                    x1_ref, h2_ref, idx_ref, wts_ref, cnt_ref, carry_scr):
    i = pl.program_id(0)

    @pl.when(i == 0)
    def _():
        carry_scr[...] = jnp.zeros_like(carry_scr)

    idx_ref[...] = jnp.zeros_like(idx_ref)
    subtiles = [slice(r * OUT_RC, (r + 1) * OUT_RC) for r in range(OUT_TM // OUT_RC)]
    ys = [jnp.dot(ret_ref[rs, :], wo_ref[:RET_WIDTH, :], preferred_element_type=F32)
          + jnp.dot(fox_ref[rs, :], wo_ref[RET_WIDTH:, :], preferred_element_type=F32) for rs in subtiles]
    for rs, y in zip(subtiles, ys):
        _outproj_rows(rs, y, x_ref, mod_ref, npost_ref, npre_ref, rw_ref, rb_ref, x1_ref, h2_ref, idx_ref,
                      wts_ref, cnt_ref, carry_scr)


def _outproj_rows(rs, y, x_ref, mod_ref, npost_ref, npre_ref, rw_ref, rb_ref,
                  x1_ref, h2_ref, idx_ref, wts_ref, cnt_ref, carry_scr):
    tm = OUT_RC
    gate1 = mod_ref[0, 2:3, :]
    shift2 = mod_ref[0, 3:4, :]
    scale2 = 1.0 + mod_ref[0, 4:5, :]
    x1 = x_ref[rs, :] + gate1 * (_rms_rows(y) * npost_ref[...])
    x1_ref[rs, :] = x1
    h2 = _rms_rows(x1) * npre_ref[...] * scale2 + shift2
    h2_ref[rs, :, :] = h2.astype(BF16).reshape(tm, ROW_TILES, LANES)

    hi = h2.astype(BF16)
    lo = (h2 - hi.astype(F32)).astype(BF16)
    rw = rw_ref[...]
    lg2 = jnp.dot(hi, rw, preferred_element_type=F32) + jnp.dot(lo, rw, preferred_element_type=F32)
    logits = lg2 + pltpu.roll(lg2, LANES // 2, 1) + rb_ref[...]

    lane = lax.broadcasted_iota(I32, (tm, LANES), 1)
    lanef = lane.astype(F32)
    big = float(LANES)

    def first_argmax(vals):
        vmax = jnp.max(vals, axis=1, keepdims=True)
        idx = jnp.min(jnp.where(vals == vmax, lanef, big), axis=1, keepdims=True)
        return vmax, idx

    is_coarse = (lane >= N_EXPERTS) & (lane < N_EXPERTS + N_GROUPS)
    cvals = jnp.where(is_coarse, logits, NEG)
    cmax, cidx = first_argmax(cvals)
    gidx = cidx - float(N_EXPERTS)
    gprob = 1.0 / jnp.sum(jnp.where(is_coarse, jnp.exp(cvals - cmax), 0.0), axis=1, keepdims=True)
    glo = gidx * float(EXPERTS_PER_GROUP)
    in_group = (lanef >= glo) & (lanef < glo + float(EXPERTS_PER_GROUP))
    fvals = jnp.where(in_group, logits, NEG)
    v1, e1 = first_argmax(fvals)
    fvals2 = jnp.where(lanef == e1, NEG, fvals)
    v2, e2 = first_argmax(fvals2)
    ex = jnp.exp(v2 - v1)
    w1 = gprob * (1.0 / (1.0 + ex))
    w2 = gprob * (ex / (1.0 + ex))

    sel1 = lanef == e1
    sel2 = lanef == e2
    onehot = jnp.where(sel1 | sel2, 1.0, 0.0)
    ri = lax.broadcasted_iota(I32, (tm, tm), 0)
    ci = lax.broadcasted_iota(I32, (tm, tm), 1)
    lower = jnp.where(ci < ri, 1.0, 0.0).astype(BF16)
    before = jnp.dot(lower, onehot.astype(BF16), preferred_element_type=F32) + carry_scr[...]
    r1 = jnp.sum(jnp.where(sel1, before, 0.0), axis=1, keepdims=True)
    r2 = jnp.sum(jnp.where(sel2, before, 0.0), axis=1, keepdims=True)
    carry_new = carry_scr[...] + jnp.sum(onehot, axis=0, keepdims=True)
    carry_scr[...] = carry_new
    cnt_ref[...] = carry_new.astype(I32)

    for row, col in enumerate((e1, e2, r1, r2)):
        idx_ref[row:row + 1, rs] = _col_to_row(col).astype(I32)
    wts_ref[rs, :] = jnp.where(lane == 0, w1, jnp.where(lane == 1, w2, 0.0))


def _out_projection(ret2, fox2, w_out_bf, x2, mod3, npost, npre, rw, rb, s):
    t, d = x2.shape
    per_b = s // OUT_TM
    row = lambda i: (i, 0)
    const = lambda i: (0, 0)
    return pl.pallas_call(
        _outproj_kernel,
        grid=(t // OUT_TM,),
        in_specs=[pl.BlockSpec((OUT_TM, RET_WIDTH), row),
                  pl.BlockSpec((OUT_TM, FOX_WIDTH), row),
                  pl.BlockSpec((d, d), const),
                  pl.BlockSpec((OUT_TM, d), row),
                  pl.BlockSpec((1, N_MOD, d), lambda i: (i // per_b, 0, 0)),
                  pl.BlockSpec((1, d), const),
                  pl.BlockSpec((1, d), const),
                  pl.BlockSpec((d, LANES), const),
                  pl.BlockSpec((1, LANES), const)],
        out_specs=[pl.BlockSpec((OUT_TM, d), row),
                   pl.BlockSpec((OUT_TM, ROW_TILES, LANES), lambda i: (i, 0, 0)),
                   pl.BlockSpec((SUBLANES, OUT_TM), lambda i: (0, i)),
                   pl.BlockSpec((OUT_TM, LANES), row),
                   pl.BlockSpec((1, LANES), const)],
        out_shape=[jax.ShapeDtypeStruct((t, d), F32),
                   jax.ShapeDtypeStruct((t, ROW_TILES, LANES), BF16),
                   jax.ShapeDtypeStruct((SUBLANES, t), I32),
                   jax.ShapeDtypeStruct((t, LANES), F32),
                   jax.ShapeDtypeStruct((1, LANES), I32)],
        scratch_shapes=[pltpu.VMEM((1, LANES), F32)],
        compiler_params=_cparams(("arbitrary",)),
        name="out_projection_router",
    )(ret2, fox2, w_out_bf, x2, mod3, npost.reshape(1, d), npre.reshape(1, d), rw, rb)


PLAN_ROW_START = 0
PLAN_TILE_EXPERT = 1
PLAN_NUM_TILES = 2
PLAN_TILE_END = 3
PLAN_COUNT = 4


def _plan_kernel(idx_ref, cnt_ref, pos_ref, plan_ref):
    lane = lax.broadcasted_iota(I32, (1, LANES), 1)
    cnt = jnp.where(lane < N_EXPERTS, cnt_ref[...], 0).astype(F32)
    tiles = jnp.floor((cnt + float(EXP_TM - 1)) * (1.0 / EXP_TM))
    ri = lax.broadcasted_iota(I32, (LANES, LANES), 0)
    ci = lax.broadcasted_iota(I32, (LANES, LANES), 1)
    upper = jnp.where(ri <= ci, 1.0, 0.0).astype(BF16)
    tiles8 = jnp.broadcast_to(tiles, (SUBLANES, LANES)).astype(BF16)
    tile_end = jnp.dot(tiles8, upper, preferred_element_type=F32)[0:1, :]
    row_start = (tile_end - tiles) * float(EXP_TM)
    num_tiles = jnp.max(tile_end, axis=1, keepdims=True)
    end_col = _row_to_col(tile_end)
    owned = jnp.where((end_col <= ci.astype(F32)) & (ri < N_EXPERTS), 1.0, 0.0)
    tile_expert = jnp.sum(owned, axis=0, keepdims=True)
    last_expert = jnp.max(jnp.where(lane.astype(F32) < num_tiles, tile_expert, 0.0), axis=1, keepdims=True)
    tile_expert = jnp.minimum(tile_expert, last_expert)
    plan_ref[...] = jnp.zeros_like(plan_ref)
    plan_ref[PLAN_ROW_START:PLAN_ROW_START + 1, :] = row_start.astype(I32)
    plan_ref[PLAN_TILE_EXPERT:PLAN_TILE_EXPERT + 1, :] = tile_expert.astype(I32)
    plan_ref[PLAN_NUM_TILES:PLAN_NUM_TILES + 1, :] = jnp.broadcast_to(num_tiles, (1, LANES)).astype(I32)
    plan_ref[PLAN_TILE_END:PLAN_TILE_END + 1, :] = tile_end.astype(I32)
    plan_ref[PLAN_COUNT:PLAN_COUNT + 1, :] = cnt.astype(I32)

    experts = idx_ref[0:2, :]
    start_i = row_start.astype(I32)
    base = jnp.zeros(experts.shape, I32)
    for e in range(N_EXPERTS):
        base = jnp.where(experts == e, start_i[:, e:e + 1], base)
    pos_ref[...] = jnp.zeros_like(pos_ref)
    pos_ref[0:2, :] = idx_ref[2:4, :] + base


def _route_plan(idx, counts):
    t = idx.shape[1]
    return pl.pallas_call(
        _plan_kernel,
        out_shape=[jax.ShapeDtypeStruct((SUBLANES, t), I32),
                   jax.ShapeDtypeStruct((SUBLANES, LANES), I32)],
        compiler_params=pltpu.CompilerParams(vmem_limit_bytes=VMEM_LIMIT),
        name="route_plan",
    )(idx, counts)


def _fused_expert_kernel(pa_ref, pb_ref, plan_ref, h2_hbm, w1_hbm, w3_hbm, w2_hbm, ys_ref,
                         inv, h2v, tsem, xbuf, w1s, w3s, w2s, wsem, w1b, w3b, w2b, *, layer):
    i = pl.program_id(0)
    tm = EXP_TM
    n_tok = h2_hbm.shape[0]
    nt = plan_ref[PLAN_NUM_TILES, 0]
    expert = plan_ref[PLAN_TILE_EXPERT, i]
    table_copy = pltpu.make_async_copy(h2_hbm, h2v, tsem)

    def weight_copies(e):
        return (pltpu.make_async_copy(w1_hbm.at[layer, e], w1s, wsem.at[0]),
                pltpu.make_async_copy(w3_hbm.at[layer, e], w3s, wsem.at[1]),
                pltpu.make_async_copy(w2_hbm.at[layer, e], w2s, wsem.at[2]))

    @pl.when(i == 0)
    def _():
        for cp in weight_copies(plan_ref[PLAN_TILE_EXPERT, 0]):
            cp.start()
        table_copy.start()

        def fill(p, _):
            inv[p] = 0
            return 0
        for e in range(N_EXPERTS):
            lax.fori_loop(plan_ref[PLAN_ROW_START, e] + plan_ref[PLAN_COUNT, e],
                          plan_ref[PLAN_TILE_END, e] * tm, fill, 0)

        def scatter(j, _):
            for u in range(DMA_UNROLL):
                tok = j * DMA_UNROLL + u
                inv[pa_ref[tok]] = tok
                inv[pb_ref[tok]] = tok
            return 0
        lax.fori_loop(0, n_tok // DMA_UNROLL, scatter, 0)
        table_copy.wait()

        def first_rows(j, _):
            for u in range(DSP_UNROLL):
                r = j * DSP_UNROLL + u
                xbuf[0, r] = h2v[inv[r]]
            return 0
        lax.fori_loop(0, tm // DSP_UNROLL, first_rows, 0)

    valid = i < nt
    first = (i == 0) | (expert != plan_ref[PLAN_TILE_EXPERT, jnp.maximum(i - 1, 0)])

    def tile(parity, new_expert):
        if new_expert:
            for cp in weight_copies(expert):
                cp.wait()
        nxt_base = jnp.minimum(i + 1, nt - 1) * tm
        for r in range(tm):
            xbuf[1 - parity, r] = h2v[inv[nxt_base + r]]
        xt = xbuf[parity].reshape(tm, D_MODEL)
        if new_expert:
            w1b[...] = w1s[...].astype(BF16)
        a = jnp.dot(xt, w1b[...], preferred_element_type=F32)
        if new_expert:
            w3b[...] = w3s[...].astype(BF16)
        g = jnp.dot(xt, w3b[...], preferred_element_type=F32)
        act = (a * _sigmoid(a) * g).astype(BF16)
        if new_expert:
            w2b[...] = w2s[...].astype(BF16)
        y = jnp.dot(act, w2b[...], preferred_element_type=F32)
        ys_ref[...] = y.astype(BF16).reshape(tm, ROW_TILES, LANES)
        if new_expert:
            nxt = plan_ref[PLAN_TILE_END, expert]

            @pl.when(nxt < nt)
            def _():
                for cp in weight_copies(plan_ref[PLAN_TILE_EXPERT, nxt]):
                    cp.start()

    for parity in range(2):
        for new_expert in (False, True):
            here = first if new_expert else jnp.logical_not(first)
            pl.when(valid & (i % 2 == parity) & here)(functools.partial(tile, parity, new_expert))

    @pl.when(jnp.logical_not(valid))
    def _():
        ys_ref[...] = jnp.zeros_like(ys_ref)


def _fused_expert_gemm(h2t, pos_a, pos_b, plan, w1, w3, w2, layer):
    t = h2t.shape[0]
    n_tiles = (t * 2) // EXP_TM + N_EXPERTS
    n_rows = n_tiles * EXP_TM
    d, de = w1.shape[2], w1.shape[3]
    any_spec = pl.BlockSpec(memory_space=pl.ANY)
    return pl.pallas_call(
        functools.partial(_fused_expert_kernel, layer=layer),
        grid_spec=pltpu.PrefetchScalarGridSpec(
            num_scalar_prefetch=3,
            grid=(n_tiles,),
            in_specs=[any_spec, any_spec, any_spec, any_spec],
            out_specs=pl.BlockSpec((EXP_TM, ROW_TILES, LANES), lambda i, pa, pb, plan: (i, 0, 0)),
            scratch_shapes=[pltpu.SMEM((n_rows,), I32),
                            pltpu.VMEM((t, ROW_TILES, LANES), BF16),
                            pltpu.SemaphoreType.DMA(()),
                            pltpu.VMEM((2, EXP_TM, ROW_TILES, LANES), BF16),
                            pltpu.VMEM((d, de), F32),
                            pltpu.VMEM((d, de), F32),
                            pltpu.VMEM((de, d), F32),
                            pltpu.SemaphoreType.DMA((3,)),
                            pltpu.VMEM((d, de), BF16),
                            pltpu.VMEM((d, de), BF16),
                            pltpu.VMEM((de, d), BF16)]),
        out_shape=jax.ShapeDtypeStruct((n_rows, ROW_TILES, LANES), BF16),
        compiler_params=pltpu.CompilerParams(dimension_semantics=("arbitrary",),
                                             vmem_limit_bytes=FUSED_VMEM_LIMIT),
        name="moe_expert_gemm",
    )(pos_a, pos_b, plan, h2t, w1, w3, w2)


def _combine_kernel(pa_ref, pb_ref, ys_hbm, x1_ref, wts_ref, mod_ref, npost_ref, o_ref, buf_a, buf_b, sems):
    i = pl.program_id(0)
    n = pl.num_programs(0)
    tm = CMB_TM

    def start_tile(tile, slot):
        def body(j, _):
            for u in range(DMA_UNROLL):
                r = j * DMA_UNROLL + u
                tok = tile * tm + r
                pltpu.make_async_copy(ys_hbm.at[pa_ref[tok]], buf_a.at[slot, r], sems.at[0, slot]).start()
                pltpu.make_async_copy(ys_hbm.at[pb_ref[tok]], buf_b.at[slot, r], sems.at[1, slot]).start()
            return 0
        lax.fori_loop(0, tm // DMA_UNROLL, body, 0)

    def wait_tile(slot):
        pltpu.make_async_copy(ys_hbm.at[pl.ds(0, tm)], buf_a.at[slot], sems.at[0, slot]).wait()
        pltpu.make_async_copy(ys_hbm.at[pl.ds(0, tm)], buf_b.at[slot], sems.at[1, slot]).wait()

    slot = i % 2

    @pl.when(i == 0)
    def _():
        start_tile(0, 0)

    @pl.when(i + 1 < n)
    def _():
        start_tile(i + 1, 1 - slot)

    wait_tile(slot)

    wa = wts_ref[:, 0:1]
    wb = wts_ref[:, 1:2]
    ya = buf_a[slot].reshape(tm, D_MODEL).astype(F32)
    yb = buf_b[slot].reshape(tm, D_MODEL).astype(F32)
    y = wa * ya + wb * yb
    gate2 = mod_ref[0, 5:6, :]
    o_ref[...] = x1_ref[...] + gate2 * (_rms_rows(y) * npost_ref[...])


def _combine(ys, pos_a, pos_b, x1, wts, mod3, npost, s):
    t, d = x1.shape
    per_b = s // CMB_TM
    row = lambda i, pa, pb: (i, 0)
    return pl.pallas_call(
        _combine_kernel,
        grid_spec=pltpu.PrefetchScalarGridSpec(
            num_scalar_prefetch=2,
            grid=(t // CMB_TM,),
            in_specs=[pl.BlockSpec(memory_space=pl.ANY),
                      pl.BlockSpec((CMB_TM, d), row),
                      pl.BlockSpec((CMB_TM, LANES), row),
                      pl.BlockSpec((1, N_MOD, d), lambda i, pa, pb: (i // per_b, 0, 0)),
                      pl.BlockSpec((1, d), lambda i, pa, pb: (0, 0))],
            out_specs=pl.BlockSpec((CMB_TM, d), row),
            scratch_shapes=[pltpu.VMEM((2, CMB_TM, ROW_TILES, LANES), BF16),
                            pltpu.VMEM((2, CMB_TM, ROW_TILES, LANES), BF16),
                            pltpu.SemaphoreType.DMA((2, 2))]),
        out_shape=jax.ShapeDtypeStruct((t, d), F32),
        compiler_params=_cparams(("arbitrary",)),
        name="moe_combine",
    )(pos_a, pos_b, ys, x1, wts, mod3, npost.reshape(1, d))


def _rope_tables(s):
    inv_freq = ROPE_BASE ** (-np.arange(0, HEAD_DIM, 2, dtype=np.float64) / HEAD_DIM)
    angle = np.arange(s, dtype=np.float64)[:, None] * inv_freq[None, :]
    cos, sin = np.cos(angle), np.sin(angle)
    return (jnp.asarray(np.concatenate([cos, cos], axis=-1), F32),
            jnp.asarray(np.concatenate([-sin, sin], axis=-1), F32))


def _router_weights(router_coarse_w, router_fine_w, router_coarse_b, router_fine_b):
    d = router_fine_w.shape[0]
    w = jnp.concatenate([router_fine_w, router_coarse_w], axis=1)
    n = w.shape[1]
    w_hi = w.astype(BF16)
    w_lo = (w - w_hi.astype(F32)).astype(BF16)
    pad = jnp.zeros((d, LANES // 2 - n), BF16)
    rw = jnp.concatenate([w_hi, pad, w_lo, pad], axis=1)
    rb = jnp.concatenate([router_fine_b, router_coarse_b, jnp.zeros((LANES - n,), F32)]).reshape(1, LANES)
    return rw, rb


def _layer(x2, c, nb, s, cosf, sinf, lg_tab, layer, ada_w, ada_b, norm_pre_mix, norm_post_mix, w_in, fox_f_bias,
           ret_gn_w, fox_qn_w, fox_kn_w, fox_on_w, w_out, norm_pre_ffn, norm_post_ffn,
           router_coarse_w, router_coarse_b, router_fine_w, router_fine_b, w1, w3, w2):
    t, d = x2.shape
    ada_b3 = ada_b.reshape(ada_b.shape[0], 1, N_MOD * d)
    mod_head, ca = _modulation_head(c, ada_w, ada_b3, layer, MOD_HEAD * d)
    mod_head = mod_head.reshape(nb, MOD_HEAD, d)

    proj, lf = _in_projection(x2, mod_head, norm_pre_mix[layer], w_in, fox_f_bias[layer], fox_qn_w[layer],
                              fox_kn_w[layer], cosf, sinf, nb, s, layer)
    cum = _cum_forget(lf)
    proj3 = proj.reshape(nb, s, MAIN_COLS)
    ret = _retention(proj3, ret_gn_w[layer], lg_tab)
    fox, mod_tail, w_out_bf = _fox_attention(proj3, cum, fox_on_w[layer], ca, ada_w, ada_b3, w_out, layer,
                                             MOD_HEAD * d)
    mod3 = jnp.concatenate([mod_head, mod_tail.reshape(nb, N_MOD - MOD_HEAD, d)], axis=1)

    rw, rb = _router_weights(router_coarse_w[layer], router_fine_w[layer], router_coarse_b[layer],
                             router_fine_b[layer])
    x1, h2t, idx, wts, counts = _out_projection(
        ret.reshape(t, RET_WIDTH), fox.reshape(t, FOX_WIDTH), w_out_bf, x2, mod3,
        norm_post_mix[layer], norm_pre_ffn[layer], rw, rb, s)

    pos, plan = _route_plan(idx, counts)
    pos_a, pos_b = pos[0], pos[1]
    ys = _fused_expert_gemm(h2t, pos_a, pos_b, plan, w1, w3, w2, layer)
    return _combine(ys, pos_a, pos_b, x1, wts, mod3, norm_post_ffn[layer], s)


def kernel(x, c, ada_w, ada_b, norm_pre_mix, norm_post_mix, w_in, fox_f_bias, ret_gn_w, fox_qn_w, fox_kn_w,
           fox_on_w, w_out, norm_pre_ffn, norm_post_ffn, router_coarse_w, router_coarse_b, router_fine_w,
           router_fine_b, w1, w3, w2):
    nb, s, d = x.shape
    depth = ada_w.shape[0]
    cosf, sinf = _rope_tables(s)
    log_gamma = jnp.log1p(-jnp.exp2(-5.0 - jnp.arange(RET_HEADS, dtype=F32)))
    lg_tab = jnp.broadcast_to(log_gamma[:, None, None], (RET_HEADS, 1, LANES))
    x2 = x.reshape(nb * s, d)
    for layer in range(depth):
        x2 = _layer(x2, c, nb, s, cosf, sinf, lg_tab, layer, ada_w, ada_b, norm_pre_mix, norm_post_mix,
                    w_in, fox_f_bias, ret_gn_w, fox_qn_w, fox_kn_w, fox_on_w, w_out, norm_pre_ffn,
                    norm_post_ffn, router_coarse_w, router_coarse_b, router_fine_w, router_fine_b, w1, w3, w2)
    return x2.reshape(nb, s, d)
```

```python
import functools

import jax
import jax.numpy as jnp
import numpy as np
from jax import lax
from jax.experimental import pallas as pl
from jax.experimental.pallas import tpu as pltpu

F32 = jnp.float32
BF16 = jnp.bfloat16
I32 = jnp.int32

D_MODEL = 2048
HEAD_DIM = 128
RET_WIDTH = D_MODEL // 2
FOX_WIDTH = D_MODEL - RET_WIDTH
RET_HEADS = RET_WIDTH // HEAD_DIM
FOX_HEADS = FOX_WIDTH // HEAD_DIM
ROPE_BASE = 10000.0
N_GROUPS = 4
EXPERTS_PER_GROUP = 8
N_EXPERTS = N_GROUPS * EXPERTS_PER_GROUP
D_EXPERT = D_MODEL // 4
N_MOD = 6
MOD_HEAD = 2
EPS = 1e-6
MAIN_COLS = 4 * RET_WIDTH + 3 * FOX_WIDTH

LANES = 128
SUBLANES = 8
ROW_TILES = D_MODEL // LANES
NEG = -1e30
LOG2E = 1.4426950408889634

VMEM_LIMIT = 56 * 1024 * 1024
FUSED_VMEM_LIMIT = 62 * 1024 * 1024

MOD_TN = 1024
IN_TM, IN_TN = 2048, 512
IN_RC = 256
IN_XC = 512
RET_CHUNK = 512
RET_HPS = 2
FOX_TQ, FOX_TK = 512, 512
FOX_HPS = 4
OUT_TM = 512
OUT_RC = 256
EXP_TM = 256
DSP_UNROLL = 8
CMB_TM = 256
DMA_UNROLL = 8


def _cparams(sem):
    return pltpu.CompilerParams(dimension_semantics=sem, vmem_limit_bytes=VMEM_LIMIT)


def _nt_dot(a, b):
    return lax.dot_general(a, b, (((1,), (1,)), ((), ())), preferred_element_type=F32)


def _tn_dot(a, b):
    return lax.dot_general(a, b, (((0,), (0,)), ((), ())), preferred_element_type=F32)


def _sigmoid(x):
    return 1.0 / (1.0 + jnp.exp(-x))


def _col_to_row(col):
    n = col.shape[0]
    ri = lax.broadcasted_iota(I32, (n, n), 0)
    ci = lax.broadcasted_iota(I32, (n, n), 1)
    return jnp.sum(jnp.where(ri == ci, col, 0.0), axis=0, keepdims=True)


def _row_to_col(row):
    n = row.shape[1]
    ri = lax.broadcasted_iota(I32, (n, n), 0)
    ci = lax.broadcasted_iota(I32, (n, n), 1)
    return jnp.sum(jnp.where(ri == ci, row, 0.0), axis=1, keepdims=True)


def _mod_columns(ca_ref, w_ref, b_ref, o_ref):
    for b in range(ca_ref.shape[0]):
        ca = ca_ref[b]
        for j in range(w_ref.shape[1] // LANES):
            sl = slice(j * LANES, (j + 1) * LANES)
            col = jnp.sum(w_ref[:, sl] * ca, axis=0, keepdims=True)
            o_ref[b:b + 1, sl] = col + b_ref[:, sl]


def _mod_head_kernel(cb_ref, w_ref, b_ref, o_ref, ca_ref):
    cv = cb_ref[...]
    ca_ref[...] = cv * _sigmoid(cv)
    _mod_columns(ca_ref, w_ref, b_ref, o_ref)


def _modulation_head(c, ada_w, ada_b3, layer, n_cols):
    nb, d = c.shape
    cb = jnp.broadcast_to(c[:, :, None], (nb, d, LANES))
    return pl.pallas_call(
        _mod_head_kernel,
        grid=(n_cols // MOD_TN,),
        in_specs=[pl.BlockSpec((nb, d, LANES), lambda j: (0, 0, 0)),
                  pl.BlockSpec((None, d, MOD_TN), lambda j: (layer, 0, j)),
                  pl.BlockSpec((None, 1, MOD_TN), lambda j: (layer, 0, j))],
        out_specs=[pl.BlockSpec((nb, MOD_TN), lambda j: (0, j)),
                   pl.BlockSpec((nb, d, LANES), lambda j: (0, 0, 0))],
        out_shape=[jax.ShapeDtypeStruct((nb, n_cols), F32),
                   jax.ShapeDtypeStruct((nb, d, LANES), F32)],
        compiler_params=_cparams(("arbitrary",)),
        name="adaln_mod",
    )(cb, ada_w, ada_b3)


def _rms_rows(x):
    return x * lax.rsqrt(jnp.mean(x * x, axis=-1, keepdims=True) + EPS)


def _inproj_kernel(x_hbm, mod_ref, nw_ref, w_ref, wf_ref, fb_ref, cos_ref, sin_ref, qn_ref, kn_ref,
                   o_ref, lf_ref, h_scr, wb_scr, xbuf, xsem):
    i = pl.program_id(0)
    j = pl.program_id(1)
    tiles_per_sec = RET_WIDTH // IN_TN
    heads_per_tile = IN_TN // HEAD_DIM

    @pl.when(j == 0)
    def _():
        shift = mod_ref[0, 0:1, :]
        scale = 1.0 + mod_ref[0, 1:2, :]
        nw = nw_ref[...]
        n_chunks = IN_TM // IN_XC

        def x_copy(c):
            rows = pl.ds(i * IN_TM + c * IN_XC, IN_XC)
            return pltpu.make_async_copy(x_hbm.at[rows, :], xbuf.at[c % 2], xsem.at[c % 2])

        x_copy(0).start()
        for c in range(n_chunks):
            if c + 1 < n_chunks:
                x_copy(c + 1).start()
            x_copy(c).wait()
            rows = 128
            for r in range(IN_XC // rows):
                h = _rms_rows(xbuf[c % 2, r * rows:(r + 1) * rows, :]) * nw * scale + shift
                h_scr[c * IN_XC + r * rows:c * IN_XC + (r + 1) * rows, :] = h.astype(BF16)
        z = _nt_dot(wf_ref[...], h_scr[...]) + fb_ref[...]
        lf_ref[0] = jnp.minimum(z, 0.0) - jnp.log1p(jnp.exp(-jnp.abs(z)))

    wb_scr[...] = w_ref[...].astype(BF16)
    sec = j // tiles_per_sec

    def section(epilogue):
        for r in range(IN_TM // IN_RC):
            rs = slice(r * IN_RC, (r + 1) * IN_RC)
            acc = _nt_dot(h_scr[rs, :], wb_scr[...])
            epilogue(acc, rs)

    def per_head(fn):
        def epilogue(acc, rs):
            for hh in range(heads_per_tile):
                sl = slice(hh * HEAD_DIM, (hh + 1) * HEAD_DIM)
                o_ref[rs, sl] = fn(acc[:, sl], rs).astype(BF16)
        return epilogue

    def whole(fn):
        def epilogue(acc, rs):
            o_ref[rs, :] = fn(acc).astype(BF16)
        return epilogue

    def rope(v, rs):
        return v * cos_ref[rs, :] + pltpu.roll(v, HEAD_DIM // 2, 1) * sin_ref[rs, :]

    @pl.when(sec == 0)
    def _():
        section(per_head(rope))

    @pl.when(sec == 1)
    def _():
        section(per_head(lambda v, rs: rope(v, rs) * (HEAD_DIM ** -0.5)))

    @pl.when((sec == 2) | (sec == 6))
    def _():
        section(whole(lambda a: a))

    @pl.when(sec == 3)
    def _():
        section(whole(lambda a: a * _sigmoid(a)))

    @pl.when(sec == 4)
    def _():
        section(per_head(lambda v, rs: _rms_rows(v) * qn_ref[...] * (LOG2E * HEAD_DIM ** -0.5)))

    @pl.when(sec == 5)
    def _():
        section(per_head(lambda v, rs: _rms_rows(v) * kn_ref[...]))


def _in_projection(x2, mod_head, norm_w, w_in, fox_f_bias, fox_qn_w, fox_kn_w, cosf, sinf, nb, s, layer):
    t, d = x2.shape
    w_t = jnp.swapaxes(w_in, 1, 2)
    wf_t = w_t[layer, MAIN_COLS:, :].astype(BF16)
    per_b = s // IN_TM
    grid = (t // IN_TM, MAIN_COLS // IN_TN)
    return pl.pallas_call(
        _inproj_kernel,
        grid=grid,
        in_specs=[
            pl.BlockSpec(memory_space=pl.ANY),
            pl.BlockSpec((1, MOD_HEAD, d), lambda i, j: (i // per_b, 0, 0)),
            pl.BlockSpec((1, d), lambda i, j: (0, 0)),
            pl.BlockSpec((None, IN_TN, d), lambda i, j: (layer, j, 0)),
            pl.BlockSpec((FOX_HEADS, d), lambda i, j: (0, 0)),
            pl.BlockSpec((FOX_HEADS, 1), lambda i, j: (0, 0)),
            pl.BlockSpec((IN_TM, HEAD_DIM), lambda i, j: (i % per_b, 0)),
            pl.BlockSpec((IN_TM, HEAD_DIM), lambda i, j: (i % per_b, 0)),
            pl.BlockSpec((1, HEAD_DIM), lambda i, j: (0, 0)),
            pl.BlockSpec((1, HEAD_DIM), lambda i, j: (0, 0)),
        ],
        out_specs=[
            pl.BlockSpec((IN_TM, IN_TN), lambda i, j: (i, j)),
            pl.BlockSpec((1, FOX_HEADS, IN_TM), lambda i, j: (i // per_b, 0, i % per_b)),
        ],
        out_shape=[jax.ShapeDtypeStruct((t, MAIN_COLS), BF16),
                   jax.ShapeDtypeStruct((nb, FOX_HEADS, s), F32)],
        scratch_shapes=[pltpu.VMEM((IN_TM, d), BF16), pltpu.VMEM((IN_TN, d), BF16),
                        pltpu.VMEM((2, IN_XC, d), F32), pltpu.SemaphoreType.DMA((2,))],
        compiler_params=_cparams(("arbitrary", "arbitrary")),
        name="in_projection",
    )(x2, mod_head, norm_w.reshape(1, d), w_t, wf_t, fox_f_bias.reshape(FOX_HEADS, 1), cosf, sinf,
      fox_qn_w.reshape(1, HEAD_DIM), fox_kn_w.reshape(1, HEAD_DIM))


def _split3(x):
    hi = x.astype(BF16)
    r1 = x - hi.astype(F32)
    mid = r1.astype(BF16)
    lo = (r1 - mid.astype(F32)).astype(BF16)
    return hi, mid, lo


def _cumsum_kernel(x_ref, o_ref, *, rows_per_seq):
    x = x_ref[...]
    r = x.shape[0]
    ri = lax.broadcasted_iota(I32, (LANES, LANES), 0)
    ci = lax.broadcasted_iota(I32, (LANES, LANES), 1)
    upper = jnp.where(ri <= ci, 1.0, 0.0).astype(BF16)
    rowcum = sum(jnp.dot(p, upper, preferred_element_type=F32) for p in _split3(x))
    tot = jnp.broadcast_to(rowcum[:, LANES - 1:LANES], (r, LANES))
    gi = lax.broadcasted_iota(I32, (r, r), 0)
    gj = lax.broadcasted_iota(I32, (r, r), 1)
    same_seq = (gi // rows_per_seq) == (gj // rows_per_seq)
    lower = jnp.where(same_seq & (gj < gi), 1.0, 0.0).astype(BF16)
    prefix = sum(jnp.dot(lower, p, preferred_element_type=F32) for p in _split3(tot))
    o_ref[...] = rowcum + prefix


def _cum_forget(lf):
    nb, nh, s = lf.shape
    rows = nb * nh * s // LANES
    out = pl.pallas_call(
        functools.partial(_cumsum_kernel, rows_per_seq=s // LANES),
        out_shape=jax.ShapeDtypeStruct((rows, LANES), F32),
        compiler_params=pltpu.CompilerParams(vmem_limit_bytes=VMEM_LIMIT),
        name="forget_cumsum",
    )(lf.reshape(rows, LANES))
    return out.reshape(nb * nh, 1, s)


def _ret_kernel(lg_ref, q_ref, k_ref, v_ref, g_ref, gnw_ref, o_ref):
    c = RET_CHUNK
    s = q_ref.shape[1]
    ri = lax.broadcasted_iota(I32, (c, c), 0)
    ci = lax.broadcasted_iota(I32, (c, c), 1)
    diff = (ri - ci).astype(F32)
    pos = lax.broadcasted_iota(I32, (c, 1), 0).astype(F32)
    consts = []
    for hh in range(RET_HPS):
        lg = lg_ref[hh][:, 0:1]
        decay = jnp.where(diff >= 0.0, jnp.exp(lg * jnp.maximum(diff, 0.0)), 0.0)
        zeta = jnp.exp(lg * (float(c - 1) - pos))
        xi = jnp.exp(lg * (pos + 1.0))
        chunk_decay = jnp.exp(lg * float(c))
        consts.append((decay, zeta, xi, chunk_decay))

    def body(n, states):
        r0 = n * c
        new_states = []
        for hh in range(RET_HPS):
            decay, zeta, xi, chunk_decay = consts[hh]
            sl = slice(hh * HEAD_DIM, (hh + 1) * HEAD_DIM)
            q = q_ref[0, pl.ds(r0, c), sl]
            k = k_ref[0, pl.ds(r0, c), sl]
            v = v_ref[0, pl.ds(r0, c), sl]
            scores = _nt_dot(q, k) * decay
            intra = jnp.dot(scores.astype(BF16), v, preferred_element_type=F32)
            cross = jnp.dot(q, states[hh].astype(BF16), preferred_element_type=F32) * xi
            kz = (k.astype(F32) * zeta).astype(BF16)
            kv = _tn_dot(kz, v)
            o = intra + cross
            mu = jnp.mean(o, axis=-1, keepdims=True)
            oc = o - mu
            var = jnp.mean(oc * oc, axis=-1, keepdims=True)
            y = oc * lax.rsqrt(var + EPS) * gnw_ref[:, sl] * g_ref[0, pl.ds(r0, c), sl].astype(F32)
            o_ref[0, pl.ds(r0, c), sl] = y.astype(BF16)
            new_states.append(states[hh] * chunk_decay + kv)
        return tuple(new_states)

    states = tuple(jnp.zeros((HEAD_DIM, HEAD_DIM), F32) for _ in range(RET_HPS))
    for n in range(s // c):
        states = body(n, states)


def _retention(proj3, ret_gn_w, lg_tab):
    nb, s, _ = proj3.shape
    w = RET_HPS * HEAD_DIM
    gpg = RET_WIDTH // w

    def col(sec):
        return pl.BlockSpec((1, s, w), lambda b, h: (b, 0, sec * gpg + h))

    return pl.pallas_call(
        _ret_kernel,
        grid=(nb, gpg),
        in_specs=[pl.BlockSpec((RET_HPS, 1, LANES), lambda b, h: (h, 0, 0)),
                  col(0), col(1), col(2), col(3),
                  pl.BlockSpec((1, w), lambda b, h: (0, h))],
        out_specs=pl.BlockSpec((1, s, w), lambda b, h: (b, 0, h)),
        out_shape=jax.ShapeDtypeStruct((nb, s, RET_WIDTH), BF16),
        compiler_params=_cparams(("arbitrary", "arbitrary")),
        name="retention",
    )(lg_tab, proj3, proj3, proj3, proj3, ret_gn_w.reshape(1, RET_WIDTH))


def _fox_kernel(q_ref, k_ref, v_ref, cum_ref, onw_ref, ca_ref, aw_ref, ab_ref, wo_ref,
                o_ref, mod_ref, wob_ref):
    tq, t = FOX_TQ, FOX_TK
    for qi in range(k_ref.shape[1] // tq):
        @pl.when(pl.program_id(2) == qi)
        def _():
            _mod_columns(ca_ref, aw_ref, ab_ref, mod_ref)
            wob_ref[...] = wo_ref[...].astype(BF16)
            _fox_query_tile(qi, q_ref, k_ref, v_ref, cum_ref, onw_ref, o_ref)


def _fox_query_tile(qi, q_ref, k_ref, v_ref, cum_ref, onw_ref, o_ref):
    tq, t = FOX_TQ, FOX_TK
    q0 = qi * tq
    last = q0 // t
    rows = slice(None)
    ri = lax.broadcasted_iota(I32, (tq, t), 0)
    ci = lax.broadcasted_iota(I32, (tq, t), 1)

    ones = jnp.ones((t, HEAD_DIM), BF16)

    def step(ki, carry, masked):
        k0 = ki * t
        out = []
        for hh in range(FOX_HPS):
            m, acc = carry[hh]
            sl = slice(hh * HEAD_DIM, (hh + 1) * HEAD_DIM)
            k = k_ref[0, k0:k0 + t, sl]
            v1 = jnp.concatenate([v_ref[0, k0:k0 + t, sl], ones], axis=1)
            sc = _nt_dot(q_ref[0, rows, sl], k) - cum_ref[hh, :, k0:k0 + t] * LOG2E
            if masked:
                sc = jnp.where(ci + k0 <= ri + q0, sc, NEG)
            m_new = jnp.maximum(m, jnp.max(sc, axis=1, keepdims=True))
            p = jnp.exp2((sc - m_new).astype(BF16))
            alpha = jnp.exp2(m - m_new)
            acc = alpha * acc + jnp.dot(p, v1, preferred_element_type=F32)
            out.append((m_new, acc))
        return tuple(out)

    init = tuple((jnp.full((tq, 1), NEG, F32), jnp.zeros((tq, 2 * HEAD_DIM), F32)) for _ in range(FOX_HPS))
    carry = init
    for ki in range(last):
        carry = step(ki, carry, False)
    carry = step(last, carry, True)
    for hh in range(FOX_HPS):
        _, acc = carry[hh]
        sl = slice(hh * HEAD_DIM, (hh + 1) * HEAD_DIM)
        o = acc[:, :HEAD_DIM] / acc[:, HEAD_DIM:]
        o_ref[0, rows, sl] = (_rms_rows(o) * onw_ref[:, sl]).astype(BF16)


def _fox_attention(proj3, cum, fox_on_w, ca, ada_w, ada_b3, w_out, layer, mod_col0):
    nb, s, _ = proj3.shape
    d = ca.shape[1]
    w = FOX_HPS * HEAD_DIM
    base = 4 * RET_WIDTH // w
    gpg = FOX_WIDTH // w
    nq = s // FOX_TQ
    n_steps = nb * gpg * nq
    n_mod = ada_w.shape[2] - mod_col0
    mc = n_mod // n_steps
    assert n_mod == n_steps * mc and mc % LANES == 0 and mod_col0 % mc == 0 and d % n_steps == 0
    wo_rows = d // n_steps
    step = lambda b, h, i: (b * gpg + h) * nq + i
    return pl.pallas_call(
        _fox_kernel,
        grid=(nb, gpg, nq),
        in_specs=[pl.BlockSpec((1, FOX_TQ, w), lambda b, h, i: (b, i, base + h)),
                  pl.BlockSpec((1, s, w), lambda b, h, i: (b, 0, base + gpg + h)),
                  pl.BlockSpec((1, s, w), lambda b, h, i: (b, 0, base + 2 * gpg + h)),
                  pl.BlockSpec((FOX_HPS, 1, s), lambda b, h, i: (b * gpg + h, 0, 0)),
                  pl.BlockSpec((1, w), lambda b, h, i: (0, h)),
                  pl.BlockSpec((nb, d, LANES), lambda b, h, i: (0, 0, 0)),
                  pl.BlockSpec((None, d, mc), lambda b, h, i: (layer, 0, mod_col0 // mc + step(b, h, i))),
                  pl.BlockSpec((None, 1, mc), lambda b, h, i: (layer, 0, mod_col0 // mc + step(b, h, i))),
                  pl.BlockSpec((None, wo_rows, d), lambda b, h, i: (layer, step(b, h, i), 0))],
        out_specs=[pl.BlockSpec((1, FOX_TQ, w), lambda b, h, i: (b, i, h)),
                   pl.BlockSpec((nb, mc), lambda b, h, i: (0, step(b, h, i))),
                   pl.BlockSpec((wo_rows, d), lambda b, h, i: (step(b, h, i), 0))],
        out_shape=[jax.ShapeDtypeStruct((nb, s, FOX_WIDTH), BF16),
                   jax.ShapeDtypeStruct((nb, n_mod), F32),
                   jax.ShapeDtypeStruct((d, d), BF16)],
        compiler_params=_cparams(("arbitrary", "arbitrary", "arbitrary")),
        name="fox_attention",
    )(proj3, proj3, proj3, cum, fox_on_w.reshape(1, FOX_WIDTH), ca, ada_w, ada_b3, w_out)


def _outproj_kernel(ret_ref, fox_ref, wo_ref, x_ref, mod_ref, npost_ref, npre_ref, rw_ref, rb_ref,
                    x1_ref, h2_ref, idx_ref, wts_ref, cnt_ref, carry_scr):
    i = pl.program_id(0)

    @pl.when(i == 0)
    def _():
        carry_scr[...] = jnp.zeros_like(carry_scr)

    idx_ref[...] = jnp.zeros_like(idx_ref)
    subtiles = [slice(r * OUT_RC, (r + 1) * OUT_RC) for r in range(OUT_TM // OUT_RC)]
    ys = [jnp.dot(ret_ref[rs, :], wo_ref[:RET_WIDTH, :], preferred_element_type=F32)
          + jnp.dot(fox_ref[rs, :], wo_ref[RET_WIDTH:, :], preferred_element_type=F32) for rs in subtiles]
    for rs, y in zip(subtiles, ys):
        _outproj_rows(rs, y, x_ref, mod_ref, npost_ref, npre_ref, rw_ref, rb_ref, x1_ref, h2_ref, idx_ref,
                      wts_ref, cnt_ref, carry_scr)


def _outproj_rows(rs, y, x_ref, mod_ref, npost_ref, npre_ref, rw_ref, rb_ref,
                  x1_ref, h2_ref, idx_ref, wts_ref, cnt_ref, carry_scr):
    tm = OUT_RC
    gate1 = mod_ref[0, 2:3, :]
    shift2 = mod_ref[0, 3:4, :]
    scale2 = 1.0 + mod_ref[0, 4:5, :]
    x1 = x_ref[rs, :] + gate1 * (_rms_rows(y) * npost_ref[...])
    x1_ref[rs, :] = x1
    h2 = _rms_rows(x1) * npre_ref[...] * scale2 + shift2
    h2_ref[rs, :, :] = h2.astype(BF16).reshape(tm, ROW_TILES, LANES)

    hi = h2.astype(BF16)
    lo = (h2 - hi.astype(F32)).astype(BF16)
    rw = rw_ref[...]
    lg2 = jnp.dot(hi, rw, preferred_element_type=F32) + jnp.dot(lo, rw, preferred_element_type=F32)
    logits = lg2 + pltpu.roll(lg2, LANES // 2, 1) + rb_ref[...]

    lane = lax.broadcasted_iota(I32, (tm, LANES), 1)
    lanef = lane.astype(F32)
    big = float(LANES)

    def first_argmax(vals):
        vmax = jnp.max(vals, axis=1, keepdims=True)
        idx = jnp.min(jnp.where(vals == vmax, lanef, big), axis=1, keepdims=True)
        return vmax, idx

    is_coarse = (lane >= N_EXPERTS) & (lane < N_EXPERTS + N_GROUPS)
    cvals = jnp.where(is_coarse, logits, NEG)
    cmax, cidx = first_argmax(cvals)
    gidx = cidx - float(N_EXPERTS)
    gprob = 1.0 / jnp.sum(jnp.where(is_coarse, jnp.exp(cvals - cmax), 0.0), axis=1, keepdims=True)
    glo = gidx * float(EXPERTS_PER_GROUP)
    in_group = (lanef >= glo) & (lanef < glo + float(EXPERTS_PER_GROUP))
    fvals = jnp.where(in_group, logits, NEG)
    v1, e1 = first_argmax(fvals)
    fvals2 = jnp.where(lanef == e1, NEG, fvals)
    v2, e2 = first_argmax(fvals2)
    ex = jnp.exp(v2 - v1)
    w1 = gprob * (1.0 / (1.0 + ex))
    w2 = gprob * (ex / (1.0 + ex))

    sel1 = lanef == e1
    sel2 = lanef == e2
    onehot = jnp.where(sel1 | sel2, 1.0, 0.0)
    ri = lax.broadcasted_iota(I32, (tm, tm), 0)
    ci = lax.broadcasted_iota(I32, (tm, tm), 1)
    lower = jnp.where(ci < ri, 1.0, 0.0).astype(BF16)
    before = jnp.dot(lower, onehot.astype(BF16), preferred_element_type=F32) + carry_scr[...]
    r1 = jnp.sum(jnp.where(sel1, before, 0.0), axis=1, keepdims=True)
    r2 = jnp.sum(jnp.where(sel2, before, 0.0), axis=1, keepdims=True)
    carry_new = carry_scr[...] + jnp.sum(onehot, axis=0, keepdims=True)
    carry_scr[...] = carry_new
    cnt_ref[...] = carry_new.astype(I32)

    for row, col in enumerate((e1, e2, r1, r2)):
        idx_ref[row:row + 1, rs] = _col_to_row(col).astype(I32)
    wts_ref[rs, :] = jnp.where(lane == 0, w1, jnp.where(lane == 1, w2, 0.0))


def _out_projection(ret2, fox2, w_out_bf, x2, mod3, npost, npre, rw, rb, s):
    t, d = x2.shape
    per_b = s // OUT_TM
    row = lambda i: (i, 0)
    const = lambda i: (0, 0)
    return pl.pallas_call(
        _outproj_kernel,
        grid=(t // OUT_TM,),
        in_specs=[pl.BlockSpec((OUT_TM, RET_WIDTH), row),
                  pl.BlockSpec((OUT_TM, FOX_WIDTH), row),
                  pl.BlockSpec((d, d), const),
                  pl.BlockSpec((OUT_TM, d), row),
                  pl.BlockSpec((1, N_MOD, d), lambda i: (i // per_b, 0, 0)),
                  pl.BlockSpec((1, d), const),
                  pl.BlockSpec((1, d), const),
                  pl.BlockSpec((d, LANES), const),
                  pl.BlockSpec((1, LANES), const)],
        out_specs=[pl.BlockSpec((OUT_TM, d), row),
                   pl.BlockSpec((OUT_TM, ROW_TILES, LANES), lambda i: (i, 0, 0)),
                   pl.BlockSpec((SUBLANES, OUT_TM), lambda i: (0, i)),
                   pl.BlockSpec((OUT_TM, LANES), row),
                   pl.BlockSpec((1, LANES), const)],
        out_shape=[jax.ShapeDtypeStruct((t, d), F32),
                   jax.ShapeDtypeStruct((t, ROW_TILES, LANES), BF16),
                   jax.ShapeDtypeStruct((SUBLANES, t), I32),
                   jax.ShapeDtypeStruct((t, LANES), F32),
                   jax.ShapeDtypeStruct((1, LANES), I32)],
        scratch_shapes=[pltpu.VMEM((1, LANES), F32)],
        compiler_params=_cparams(("arbitrary",)),
        name="out_projection_router",
    )(ret2, fox2, w_out_bf, x2, mod3, npost.reshape(1, d), npre.reshape(1, d), rw, rb)


PLAN_ROW_START = 0
PLAN_TILE_EXPERT = 1
PLAN_NUM_TILES = 2
PLAN_TILE_END = 3
PLAN_COUNT = 4


def _plan_kernel(idx_ref, cnt_ref, pos_ref, plan_ref):
    lane = lax.broadcasted_iota(I32, (1, LANES), 1)
    cnt = jnp.where(lane < N_EXPERTS, cnt_ref[...], 0).astype(F32)
    tiles = jnp.floor((cnt + float(EXP_TM - 1)) * (1.0 / EXP_TM))
    ri = lax.broadcasted_iota(I32, (LANES, LANES), 0)
    ci = lax.broadcasted_iota(I32, (LANES, LANES), 1)
    upper = jnp.where(ri <= ci, 1.0, 0.0).astype(BF16)
    tiles8 = jnp.broadcast_to(tiles, (SUBLANES, LANES)).astype(BF16)
    tile_end = jnp.dot(tiles8, upper, preferred_element_type=F32)[0:1, :]
    row_start = (tile_end - tiles) * float(EXP_TM)
    num_tiles = jnp.max(tile_end, axis=1, keepdims=True)
    end_col = _row_to_col(tile_end)
    owned = jnp.where((end_col <= ci.astype(F32)) & (ri < N_EXPERTS), 1.0, 0.0)
    tile_expert = jnp.sum(owned, axis=0, keepdims=True)
    last_expert = jnp.max(jnp.where(lane.astype(F32) < num_tiles, tile_expert, 0.0), axis=1, keepdims=True)
    tile_expert = jnp.minimum(tile_expert, last_expert)
    plan_ref[...] = jnp.zeros_like(plan_ref)
    plan_ref[PLAN_ROW_START:PLAN_ROW_START + 1, :] = row_start.astype(I32)
    plan_ref[PLAN_TILE_EXPERT:PLAN_TILE_EXPERT + 1, :] = tile_expert.astype(I32)
    plan_ref[PLAN_NUM_TILES:PLAN_NUM_TILES + 1, :] = jnp.broadcast_to(num_tiles, (1, LANES)).astype(I32)
    plan_ref[PLAN_TILE_END:PLAN_TILE_END + 1, :] = tile_end.astype(I32)
    plan_ref[PLAN_COUNT:PLAN_COUNT + 1, :] = cnt.astype(I32)

    experts = idx_ref[0:2, :]
    start_i = row_start.astype(I32)
    base = jnp.zeros(experts.shape, I32)
    for e in range(N_EXPERTS):
        base = jnp.where(experts == e, start_i[:, e:e + 1], base)
    pos_ref[...] = jnp.zeros_like(pos_ref)
    pos_ref[0:2, :] = idx_ref[2:4, :] + base


def _route_plan(idx, counts):
    t = idx.shape[1]
    return pl.pallas_call(
        _plan_kernel,
        out_shape=[jax.ShapeDtypeStruct((SUBLANES, t), I32),
                   jax.ShapeDtypeStruct((SUBLANES, LANES), I32)],
        compiler_params=pltpu.CompilerParams(vmem_limit_bytes=VMEM_LIMIT),
        name="route_plan",
    )(idx, counts)


def _fused_expert_kernel(pa_ref, pb_ref, plan_ref, h2_hbm, w1_hbm, w3_hbm, w2_hbm, ys_ref,
                         inv, h2v, tsem, xbuf, w1s, w3s, w2s, wsem, w1b, w3b, w2b, *, layer):
    i = pl.program_id(0)
    tm = EXP_TM
    n_tok = h2_hbm.shape[0]
    nt = plan_ref[PLAN_NUM_TILES, 0]
    expert = plan_ref[PLAN_TILE_EXPERT, i]
    table_copy = pltpu.make_async_copy(h2_hbm, h2v, tsem)

    def weight_copies(e):
        return (pltpu.make_async_copy(w1_hbm.at[layer, e], w1s, wsem.at[0]),
                pltpu.make_async_copy(w3_hbm.at[layer, e], w3s, wsem.at[1]),
                pltpu.make_async_copy(w2_hbm.at[layer, e], w2s, wsem.at[2]))

    @pl.when(i == 0)
    def _():
        for cp in weight_copies(plan_ref[PLAN_TILE_EXPERT, 0]):
            cp.start()
        table_copy.start()

        def fill(p, _):
            inv[p] = 0
            return 0
        for e in range(N_EXPERTS):
            lax.fori_loop(plan_ref[PLAN_ROW_START, e] + plan_ref[PLAN_COUNT, e],
                          plan_ref[PLAN_TILE_END, e] * tm, fill, 0)

        def scatter(j, _):
            for u in range(DMA_UNROLL):
                tok = j * DMA_UNROLL + u
                inv[pa_ref[tok]] = tok
                inv[pb_ref[tok]] = tok
            return 0
        lax.fori_loop(0, n_tok // DMA_UNROLL, scatter, 0)
        table_copy.wait()

        def first_rows(j, _):
            for u in range(DSP_UNROLL):
                r = j * DSP_UNROLL + u
                xbuf[0, r] = h2v[inv[r]]
            return 0
        lax.fori_loop(0, tm // DSP_UNROLL, first_rows, 0)

    valid = i < nt
    first = (i == 0) | (expert != plan_ref[PLAN_TILE_EXPERT, jnp.maximum(i - 1, 0)])

    def tile(parity, new_expert):
        if new_expert:
            for cp in weight_copies(expert):
                cp.wait()
        nxt_base = jnp.minimum(i + 1, nt - 1) * tm
        for r in range(tm):
            xbuf[1 - parity, r] = h2v[inv[nxt_base + r]]
        xt = xbuf[parity].reshape(tm, D_MODEL)
        if new_expert:
            w1b[...] = w1s[...].astype(BF16)
        a = jnp.dot(xt, w1b[...], preferred_element_type=F32)
        if new_expert:
            w3b[...] = w3s[...].astype(BF16)
        g = jnp.dot(xt, w3b[...], preferred_element_type=F32)
        act = (a * _sigmoid(a) * g).astype(BF16)
        if new_expert:
            w2b[...] = w2s[...].astype(BF16)
        y = jnp.dot(act, w2b[...], preferred_element_type=F32)
        ys_ref[...] = y.astype(BF16).reshape(tm, ROW_TILES, LANES)
        if new_expert:
            nxt = plan_ref[PLAN_TILE_END, expert]

            @pl.when(nxt < nt)
            def _():
                for cp in weight_copies(plan_ref[PLAN_TILE_EXPERT, nxt]):
                    cp.start()

    for parity in range(2):
        for new_expert in (False, True):
            here = first if new_expert else jnp.logical_not(first)
            pl.when(valid & (i % 2 == parity) & here)(functools.partial(tile, parity, new_expert))

    @pl.when(jnp.logical_not(valid))
    def _():
        ys_ref[...] = jnp.zeros_like(ys_ref)


def _fused_expert_gemm(h2t, pos_a, pos_b, plan, w1, w3, w2, layer):
    t = h2t.shape[0]
    n_tiles = (t * 2) // EXP_TM + N_EXPERTS
    n_rows = n_tiles * EXP_TM
    d, de = w1.shape[2], w1.shape[3]
    any_spec = pl.BlockSpec(memory_space=pl.ANY)
    return pl.pallas_call(
        functools.partial(_fused_expert_kernel, layer=layer),
        grid_spec=pltpu.PrefetchScalarGridSpec(
            num_scalar_prefetch=3,
            grid=(n_tiles,),
            in_specs=[any_spec, any_spec, any_spec, any_spec],
            out_specs=pl.BlockSpec((EXP_TM, ROW_TILES, LANES), lambda i, pa, pb, plan: (i, 0, 0)),
            scratch_shapes=[pltpu.SMEM((n_rows,), I32),
                            pltpu.VMEM((t, ROW_TILES, LANES), BF16),
                            pltpu.SemaphoreType.DMA(()),
                            pltpu.VMEM((2, EXP_TM, ROW_TILES, LANES), BF16),
                            pltpu.VMEM((d, de), F32),
                            pltpu.VMEM((d, de), F32),
                            pltpu.VMEM((de, d), F32),
                            pltpu.SemaphoreType.DMA((3,)),
                            pltpu.VMEM((d, de), BF16),
                            pltpu.VMEM((d, de), BF16),
                            pltpu.VMEM((de, d), BF16)]),
        out_shape=jax.ShapeDtypeStruct((n_rows, ROW_TILES, LANES), BF16),
        compiler_params=pltpu.CompilerParams(dimension_semantics=("arbitrary",),
                                             vmem_limit_bytes=FUSED_VMEM_LIMIT),
        name="moe_expert_gemm",
    )(pos_a, pos_b, plan, h2t, w1, w3, w2)


def _combine_kernel(pa_ref, pb_ref, ys_hbm, x1_ref, wts_ref, mod_ref, npost_ref, o_ref, buf_a, buf_b, sems):
    i = pl.program_id(0)
    n = pl.num_programs(0)
    tm = CMB_TM

    def start_tile(tile, slot):
        def body(j, _):
            for u in range(DMA_UNROLL):
                r = j * DMA_UNROLL + u
                tok = tile * tm + r
                pltpu.make_async_copy(ys_hbm.at[pa_ref[tok]], buf_a.at[slot, r], sems.at[0, slot]).start()
                pltpu.make_async_copy(ys_hbm.at[pb_ref[tok]], buf_b.at[slot, r], sems.at[1, slot]).start()
            return 0
        lax.fori_loop(0, tm // DMA_UNROLL, body, 0)

    def wait_tile(slot):
        pltpu.make_async_copy(ys_hbm.at[pl.ds(0, tm)], buf_a.at[slot], sems.at[0, slot]).wait()
        pltpu.make_async_copy(ys_hbm.at[pl.ds(0, tm)], buf_b.at[slot], sems.at[1, slot]).wait()

    slot = i % 2

    @pl.when(i == 0)
    def _():
        start_tile(0, 0)

    @pl.when(i + 1 < n)
    def _():
        start_tile(i + 1, 1 - slot)

    wait_tile(slot)

    wa = wts_ref[:, 0:1]
    wb = wts_ref[:, 1:2]
    ya = buf_a[slot].reshape(tm, D_MODEL).astype(F32)
    yb = buf_b[slot].reshape(tm, D_MODEL).astype(F32)
    y = wa * ya + wb * yb
    gate2 = mod_ref[0, 5:6, :]
    o_ref[...] = x1_ref[...] + gate2 * (_rms_rows(y) * npost_ref[...])


def _combine(ys, pos_a, pos_b, x1, wts, mod3, npost, s):
    t, d = x1.shape
    per_b = s // CMB_TM
    row = lambda i, pa, pb: (i, 0)
    return pl.pallas_call(
        _combine_kernel,
        grid_spec=pltpu.PrefetchScalarGridSpec(
            num_scalar_prefetch=2,
            grid=(t // CMB_TM,),
            in_specs=[pl.BlockSpec(memory_space=pl.ANY),
                      pl.BlockSpec((CMB_TM, d), row),
                      pl.BlockSpec((CMB_TM, LANES), row),
                      pl.BlockSpec((1, N_MOD, d), lambda i, pa, pb: (i // per_b, 0, 0)),
                      pl.BlockSpec((1, d), lambda i, pa, pb: (0, 0))],
            out_specs=pl.BlockSpec((CMB_TM, d), row),
            scratch_shapes=[pltpu.VMEM((2, CMB_TM, ROW_TILES, LANES), BF16),
                            pltpu.VMEM((2, CMB_TM, ROW_TILES, LANES), BF16),
                            pltpu.SemaphoreType.DMA((2, 2))]),
        out_shape=jax.ShapeDtypeStruct((t, d), F32),
        compiler_params=_cparams(("arbitrary",)),
        name="moe_combine",
    )(pos_a, pos_b, ys, x1, wts, mod3, npost.reshape(1, d))


def _rope_tables(s):
    inv_freq = ROPE_BASE ** (-np.arange(0, HEAD_DIM, 2, dtype=np.float64) / HEAD_DIM)
    angle = np.arange(s, dtype=np.float64)[:, None] * inv_freq[None, :]
    cos, sin = np.cos(angle), np.sin(angle)
    return (jnp.asarray(np.concatenate([cos, cos], axis=-1), F32),
            jnp.asarray(np.concatenate([-sin, sin], axis=-1), F32))


def _router_weights(router_coarse_w, router_fine_w, router_coarse_b, router_fine_b):
    d = router_fine_w.shape[0]
    w = jnp.concatenate([router_fine_w, router_coarse_w], axis=1)
    n = w.shape[1]
    w_hi = w.astype(BF16)
    w_lo = (w - w_hi.astype(F32)).astype(BF16)
    pad = jnp.zeros((d, LANES // 2 - n), BF16)
    rw = jnp.concatenate([w_hi, pad, w_lo, pad], axis=1)
    rb = jnp.concatenate([router_fine_b, router_coarse_b, jnp.zeros((LANES - n,), F32)]).reshape(1, LANES)
    return rw, rb


def _layer(x2, c, nb, s, cosf, sinf, lg_tab, layer, ada_w, ada_b, norm_pre_mix, norm_post_mix, w_in, fox_f_bias,
           ret_gn_w, fox_qn_w, fox_kn_w, fox_on_w, w_out, norm_pre_ffn, norm_post_ffn,
           router_coarse_w, router_coarse_b, router_fine_w, router_fine_b, w1, w3, w2):
    t, d = x2.shape
    ada_b3 = ada_b.reshape(ada_b.shape[0], 1, N_MOD * d)
    mod_head, ca = _modulation_head(c, ada_w, ada_b3, layer, MOD_HEAD * d)
    mod_head = mod_head.reshape(nb, MOD_HEAD, d)

    proj, lf = _in_projection(x2, mod_head, norm_pre_mix[layer], w_in, fox_f_bias[layer], fox_qn_w[layer],
                              fox_kn_w[layer], cosf, sinf, nb, s, layer)
    cum = _cum_forget(lf)
    proj3 = proj.reshape(nb, s, MAIN_COLS)
    ret = _retention(proj3, ret_gn_w[layer], lg_tab)
    fox, mod_tail, w_out_bf = _fox_attention(proj3, cum, fox_on_w[layer], ca, ada_w, ada_b3, w_out, layer,
                                             MOD_HEAD * d)
    mod3 = jnp.concatenate([mod_head, mod_tail.reshape(nb, N_MOD - MOD_HEAD, d)], axis=1)

    rw, rb = _router_weights(router_coarse_w[layer], router_fine_w[layer], router_coarse_b[layer],
                             router_fine_b[layer])
    x1, h2t, idx, wts, counts = _out_projection(
        ret.reshape(t, RET_WIDTH), fox.reshape(t, FOX_WIDTH), w_out_bf, x2, mod3,
        norm_post_mix[layer], norm_pre_ffn[layer], rw, rb, s)

    pos, plan = _route_plan(idx, counts)
    pos_a, pos_b = pos[0], pos[1]
    ys = _fused_expert_gemm(h2t, pos_a, pos_b, plan, w1, w3, w2, layer)
    return _combine(ys, pos_a, pos_b, x1, wts, mod3, norm_post_ffn[layer], s)


def kernel(x, c, ada_w, ada_b, norm_pre_mix, norm_post_mix, w_in, fox_f_bias, ret_gn_w, fox_qn_w, fox_kn_w,
           fox_on_w, w_out, norm_pre_ffn, norm_post_ffn, router_coarse_w, router_coarse_b, router_fine_w,
           router_fine_b, w1, w3, w2):
    nb, s, d = x.shape
    depth = ada_w.shape[0]
    cosf, sinf = _rope_tables(s)
    log_gamma = jnp.log1p(-jnp.exp2(-5.0 - jnp.arange(RET_HEADS, dtype=F32)))
    lg_tab = jnp.broadcast_to(log_gamma[:, None, None], (RET_HEADS, 1, LANES))
    x2 = x.reshape(nb * s, d)
    for layer in range(depth):
        x2 = _layer(x2, c, nb, s, cosf, sinf, lg_tab, layer, ada_w, ada_b, norm_pre_mix, norm_post_mix,
                    w_in, fox_f_bias, ret_gn_w, fox_qn_w, fox_kn_w, fox_on_w, w_out, norm_pre_ffn,
                    norm_post_ffn, router_coarse_w, router_coarse_b, router_fine_w, router_fine_b, w1, w3, w2)
    return x2.reshape(nb, s, d)
```
